```python
import math
import jax
import jax.numpy as jnp
from jax import lax
import numpy as np


D_MODEL = 1024
BATCH = 8
SEQ = 4096
DEPTH = 2

GRID_W = 64
CTX_LEN = 256
EPS = 1e-6
F32 = jnp.float32

MLA_HEADS = 8
MLA_NOPE = 64
MLA_ROPE = 32
MLA_V = 64
MLA_Q_LORA = 256
MLA_KV_LORA = 128
ROPE_NF = MLA_ROPE // 4
ROPE_BASE = 10000.0
Q_BLOCK = 128

HY_CH = 512
HY_ORDER = 2
HY_BANDS = 16
HY_EMB_DIM = 1 + 2 * HY_BANDS
HY_FILTER_WIDTH = 64
HY_SIN_FREQ = 1.0
HY_DECAY_TARGET = 1e-2
HY_FAST_DECAY = 0.3
HY_SLOW_DECAY = 1.5
HY_MIN_DECAY = math.log(HY_DECAY_TARGET) / HY_SLOW_DECAY
HY_MAX_DECAY = math.log(HY_DECAY_TARGET) / HY_FAST_DECAY

AB_IN = MLA_Q_LORA + MLA_KV_LORA + MLA_ROPE + (HY_ORDER + 1) * HY_CH
AB_OUT = MLA_HEADS * MLA_V + HY_CH

HG_EXPAND = 128
HG_HEADS = D_MODEL // HG_EXPAND
HG_WIDTH = HG_HEADS * HG_EXPAND
HG_CHUNK = 64

FFN_HIDDEN = 2816

N_EVEN = (DEPTH + 1) // 2
N_ODD = DEPTH // 2

kernel_name = 'hybrid_mla_hyena_hgrn2_prefix_dit'


def rmsnorm(x, g):
    xf = x.astype(F32)
    y = xf * lax.rsqrt(jnp.mean(xf * xf, axis=-1, keepdims=True) + EPS)
    return (y * g.astype(F32)).astype(x.dtype)


def modulate(x, g, shift, scale):
    return rmsnorm(x, g) * (1.0 + scale) + shift


def dwconv3(x, w, b):
    xp = jnp.pad(x, ((0, 0), (1, 1), (0, 0)))
    return xp[:, :-2] * w[0] + xp[:, 1:-1] * w[1] + xp[:, 2:] * w[2] + b


def axial_rope_tables(n):
    rows = n // GRID_W
    row = jnp.repeat(jnp.arange(rows), GRID_W).astype(F32)
    col = jnp.tile(jnp.arange(GRID_W), rows).astype(F32)
    inv = ROPE_BASE ** (-jnp.arange(ROPE_NF, dtype=F32) / ROPE_NF)
    ang = jnp.stack([row, col], axis=-1)[..., None] * inv
    return jnp.cos(ang), jnp.sin(ang)


def axial_rope(x, cos, sin):
    xf = x.astype(F32).reshape(x.shape[:-1] + (2, 2, ROPE_NF))
    x1, x2 = xf[..., 0, :], xf[..., 1, :]
    out = jnp.stack([x1 * cos - x2 * sin, x2 * cos + x1 * sin], axis=-2)
    return out.reshape(x.shape).astype(x.dtype)


def attend(q, k, v):
    s = jnp.einsum('bqhd,bkhd->bhqk', q.astype(F32), k.astype(F32)) * (1.0 / math.sqrt(q.shape[-1]))
    p = jax.nn.softmax(s, axis=-1)
    return jnp.einsum('bhqk,bkhd->bqhd', p, v.astype(F32)).astype(v.dtype)


def blocked_attend(q, k, v):
    b, n, h, dq = q.shape
    qb = q.reshape(b, n // Q_BLOCK, Q_BLOCK, h, dq).swapaxes(0, 1)
    ob = lax.map(lambda qi: attend(qi, k, v), qb)
    return ob.swapaxes(0, 1).reshape(b, n, h, v.shape[-1])


def hyena_filters(length, w1, b1, w2, b2, w3):
    t = jnp.linspace(0.0, 1.0, length, dtype=F32)[:, None]
    w = 2.0 * math.pi * jnp.arange(length, dtype=F32)[:, None] / length
    f = jnp.linspace(1e-4, HY_BANDS - 1, HY_BANDS, dtype=F32)
    z = jnp.concatenate([t, jnp.cos(f * w), -jnp.sin(f * w)], axis=-1)
    a = jnp.sin(HY_SIN_FREQ * (z @ w1.astype(F32) + b1.astype(F32)))
    a = jnp.sin(HY_SIN_FREQ * (a @ w2.astype(F32) + b2.astype(F32)))
    filt = (a @ w3.astype(F32)).reshape(length, HY_ORDER, 2, HY_CH)
    deltas = jnp.abs(jnp.linspace(HY_MIN_DECAY, HY_MAX_DECAY, HY_CH, dtype=F32))
    return filt * jnp.exp(-t * deltas)[:, None, None, :]


def bidir_long_conv(u, h_fwd, h_bwd, d):
    length = u.shape[1]
    k2 = jnp.concatenate([h_fwd, jnp.zeros_like(h_fwd[:1]), h_bwd[:0:-1]], axis=0)
    spec = jnp.fft.rfft(u, n=2 * length, axis=1) * jnp.fft.rfft(k2, axis=0)[None]
    y = jnp.fft.irfft(spec, n=2 * length, axis=1)[:, :length]
    return y + u * d.astype(F32)


def hyena(u, conv_w, conv_b, w1, b1, w2, b2, w3, hy_bias):
    length = u.shape[1]
    z = dwconv3(u, conv_w, conv_b).astype(F32)
    v, x1, x2 = jnp.split(z, 3, axis=-1)
    filt = hyena_filters(length, w1, b1, w2, b2, w3)
    y = x1 * bidir_long_conv(v, filt[:, 0, 0], filt[:, 0, 1], hy_bias[0])
    y = x2 * bidir_long_conv(y, filt[:, 1, 0], filt[:, 1, 1], hy_bias[1])
    return y.astype(u.dtype)


def mixer_ab(h, hc, rope_cos, rope_sin, w_in, q_norm_g, w_q_b, kv_norm_g, w_kv_b,
             conv_w, conv_b, w1, b1, w2, b2, w3, hy_bias, w_out, need_ctx):
    kv_end = MLA_Q_LORA + MLA_KV_LORA
    r_end = kv_end + MLA_ROPE

    def project(u):
        z = u @ w_in
        lead = u.shape[:2]
        q = (rmsnorm(z[..., :MLA_Q_LORA], q_norm_g) @ w_q_b).reshape(lead + (MLA_HEADS, MLA_NOPE + MLA_ROPE))
        kv = (rmsnorm(z[..., MLA_Q_LORA:kv_end], kv_norm_g) @ w_kv_b).reshape(lead + (MLA_HEADS, MLA_NOPE + MLA_V))
        return q, kv, z[..., kv_end:r_end], z[..., r_end:]

    def keys_values(kv, k_rope):
        k_rope = jnp.broadcast_to(k_rope[:, :, None, :], kv.shape[:3] + (MLA_ROPE,))
        return jnp.concatenate([kv[..., :MLA_NOPE], k_rope], axis=-1), kv[..., MLA_NOPE:]

    def hyena_branch(hy):
        return hyena(hy, conv_w, conv_b, w1, b1, w2, b2, w3, hy_bias)

    q_c, kv_c, kr_c, hy_c = project(hc)
    k_c, v_c = keys_values(kv_c, kr_c)
    q, kv, kr, hy = project(h)
    q = jnp.concatenate([q[..., :MLA_NOPE], axial_rope(q[..., MLA_NOPE:], rope_cos[:, None], rope_sin[:, None])], axis=-1)
    k, v = keys_values(kv, axial_rope(kr, rope_cos, rope_sin))
    o = blocked_attend(q, jnp.concatenate([k_c, k], axis=1), jnp.concatenate([v_c, v], axis=1))
    y = jnp.concatenate([o.reshape(o.shape[:2] + (-1,)), hyena_branch(hy)], axis=-1) @ w_out
    y_c = None
    if need_ctx:
        o_c = attend(q_c, k_c, v_c)
        y_c = jnp.concatenate([o_c.reshape(o_c.shape[:2] + (-1,)), hyena_branch(hy_c)], axis=-1) @ w_out
    return y, y_c


def gla_chunk_scan(q, k, v, log_f, s0):
    b, n, h, _ = q.shape
    nc = n // HG_CHUNK

    def chunks(a):
        return a.reshape(b, nc, HG_CHUNK, h, a.shape[-1]).transpose(1, 0, 3, 2, 4)

    lower = jnp.tril(jnp.ones((HG_CHUNK, HG_CHUNK), dtype=bool))

    def step(state, inp):
        qc, kc, vc, gc = inp
        cum = jnp.cumsum(gc, axis=2)
        diff = cum[:, :, :, None, :] - cum[:, :, None, :, :]
        decay = jnp.exp(jnp.where(lower[:, :, None], diff, -jnp.inf))
        att = jnp.einsum('bhtk,bhtsk,bhsk->bhts', qc, decay, kc)
        o = jnp.einsum('bhts,bhsv->bhtv', att, vc) + jnp.einsum('bhtk,bhkv->bhtv', qc * jnp.exp(cum), state)
        end = cum[:, :, -1]
        state = jnp.exp(end)[..., None] * state + jnp.einsum('bhsk,bhsv->bhkv', kc * jnp.exp(end[:, :, None] - cum), vc)
        return state, o

    s_final, o = lax.scan(step, s0, (chunks(q), chunks(k), chunks(v), chunks(log_f)))
    return s_final, o.transpose(1, 0, 3, 2, 4).reshape(b, n, h, v.shape[-1])


def hgrn2_mixer(h, hc, lb, w_in, norm_g, w_out, need_ctx):
    def heads(a):
        return a.reshape(a.shape[:2] + (HG_HEADS, HG_EXPAND)).astype(F32)

    def project(u):
        q, f_fwd, f_bwd, i, g = jnp.split(u @ w_in, 5, axis=-1)
        gates = []
        for d, f_raw in enumerate((f_fwd, f_bwd)):
            fr = f_raw.astype(F32)
            log_f = jnp.logaddexp(jnp.log(lb[d]), jnp.log1p(-lb[d]) + jax.nn.log_sigmoid(fr))
            k = (1.0 - lb[d]) * jax.nn.sigmoid(-fr)
            gates.append((heads(k), heads(log_f)))
        return heads(jax.nn.silu(q)), heads(i), g, gates

    def readout(o, g):
        o = rmsnorm(o, norm_g)
        o = o.reshape(o.shape[:2] + (HG_WIDTH,)) * jax.nn.silu(g.astype(F32))
        return o.astype(g.dtype) @ w_out

    def rev(a):
        return a[:, ::-1]

    q_c, i_c, g_c, gates_c = project(hc)
    q, i, g, gates = project(h)
    s0 = jnp.zeros((hc.shape[0], HG_HEADS, HG_EXPAND, HG_EXPAND), F32)
    (k_cf, lf_cf), (k_cb, lf_cb) = gates_c
    (k_f, lf_f), (k_b, lf_b) = gates
    s_cf, o_cf = gla_chunk_scan(q_c, k_cf, i_c, lf_cf, s0)
    s_cb, o_cb = gla_chunk_scan(rev(q_c), rev(k_cb), rev(i_c), rev(lf_cb), s0)
    _, o_f = gla_chunk_scan(q, k_f, i, lf_f, s_cf)
    _, o_b = gla_chunk_scan(rev(q), rev(k_b), rev(i), rev(lf_b), s_cb)
    y = readout(o_f + rev(o_b), g)
    y_c = readout(o_cf + rev(o_cb), g_c) if need_ctx else None
    return y, y_c


def conv_ffn(u, w_up, conv_w, conv_b, w_down):
    a = dwconv3(u @ w_up, conv_w, conv_b)
    gate, val = jnp.split(a, 2, axis=-1)
    return (jax.nn.silu(gate) * val) @ w_down


def setup_inputs(seed: int = 0) -> dict:
    key = jax.random.key(seed)
    ks = iter(jax.random.split(key, 40))

    def nrm(shape, scale):
        return jax.random.normal(next(ks), shape, F32) * scale

    def gain(shape):
        return 1.0 + nrm(shape, 0.02)

    d = D_MODEL
    return {
        'x': nrm((BATCH, SEQ, d), 1.0),
        'c': nrm((BATCH, d), 1.0),
        'ctx': nrm((BATCH, CTX_LEN, d), 1.0),
        'c_ctx': nrm((d,), 1.0),
        'mod_w': nrm((DEPTH, d, 6 * d), 0.2 * d ** -0.5),
        'mod_b': nrm((DEPTH, 6 * d), 0.01),
        'norm1_g': gain((DEPTH, d)),
        'norm2_g': gain((DEPTH, d)),
        'ffn_w_up': nrm((DEPTH, d, 2 * FFN_HIDDEN), d ** -0.5),
        'ffn_conv_w': nrm((DEPTH, 3, 2 * FFN_HIDDEN), 3 ** -0.5),
        'ffn_conv_b': nrm((DEPTH, 2 * FFN_HIDDEN), 0.01),
        'ffn_w_down': nrm((DEPTH, FFN_HIDDEN, d), FFN_HIDDEN ** -0.5),
        'ab_w_in': nrm((N_EVEN, d, AB_IN), d ** -0.5),
        'mla_q_norm_g': gain((N_EVEN, MLA_Q_LORA)),
        'mla_w_q_b': nrm((N_EVEN, MLA_Q_LORA, MLA_HEADS * (MLA_NOPE + MLA_ROPE)), MLA_Q_LORA ** -0.5),
        'mla_kv_norm_g': gain((N_EVEN, MLA_KV_LORA)),
        'mla_w_kv_b': nrm((N_EVEN, MLA_KV_LORA, MLA_HEADS * (MLA_NOPE + MLA_V)), MLA_KV_LORA ** -0.5),
        'hy_conv_w': nrm((N_EVEN, 3, (HY_ORDER + 1) * HY_CH), 3 ** -0.5),
        'hy_conv_b': nrm((N_EVEN, (HY_ORDER + 1) * HY_CH), 0.01),
        'hy_w1': nrm((N_EVEN, HY_EMB_DIM, HY_FILTER_WIDTH), HY_EMB_DIM ** -0.5),
        'hy_b1': nrm((N_EVEN, HY_FILTER_WIDTH), 0.1),
        'hy_w2': nrm((N_EVEN, HY_FILTER_WIDTH, HY_FILTER_WIDTH), HY_FILTER_WIDTH ** -0.5),
        'hy_b2': nrm((N_EVEN, HY_FILTER_WIDTH), 0.1),
        'hy_w3': nrm((N_EVEN, HY_FILTER_WIDTH, HY_ORDER * 2 * HY_CH), 0.05 * HY_FILTER_WIDTH ** -0.5),
        'hy_bias': nrm((N_EVEN, HY_ORDER, HY_CH), 1.0),
        'ab_w_out': nrm((N_EVEN, AB_OUT, d), AB_OUT ** -0.5),
        'hg_w_in': nrm((N_ODD, d, 5 * HG_WIDTH), d ** -0.5),
        'hg_lb_logits': nrm((DEPTH, 2, HG_WIDTH), 0.5),
        'hg_norm_g': gain((N_ODD, HG_EXPAND)),
        'hg_w_out': nrm((N_ODD, HG_WIDTH, d), HG_WIDTH ** -0.5),
        'final_norm_g': gain((d,)),
    }


def reference(x, c, ctx, c_ctx, mod_w, mod_b, norm1_g, norm2_g, ffn_w_up, ffn_conv_w, ffn_conv_b,
              ffn_w_down, ab_w_in, mla_q_norm_g, mla_w_q_b, mla_kv_norm_g, mla_w_kv_b, hy_conv_w,
              hy_conv_b, hy_w1, hy_b1, hy_w2, hy_b2, hy_w3, hy_bias, ab_w_out, hg_w_in, hg_lb_logits,
              hg_norm_g, hg_w_out, final_norm_g):
    n = x.shape[1]
    rope_cos, rope_sin = axial_rope_tables(n)
    lb_probs = jax.nn.softmax(hg_lb_logits.astype(F32), axis=0)
    lb_all = jnp.cumsum(lb_probs, axis=0) - lb_probs[0]
    xc = ctx
    for layer in range(DEPTH):
        need_ctx = layer < DEPTH - 1
        j = layer // 2
        m = (jax.nn.silu(c) @ mod_w[layer] + mod_b[layer])[:, None, :]
        mc = jax.nn.silu(c_ctx) @ mod_w[layer] + mod_b[layer]
        sh1, sc1, g1, sh2, sc2, g2 = jnp.split(m, 6, axis=-1)
        sh1c, sc1c, g1c, sh2c, sc2c, g2c = jnp.split(mc, 6, axis=-1)
        h = modulate(x, norm1_g[layer], sh1, sc1)
        hc = modulate(xc, norm1_g[layer], sh1c, sc1c)
        if layer % 2 == 0:
            y, y_c = mixer_ab(h, hc, rope_cos, rope_sin, ab_w_in[j], mla_q_norm_g[j], mla_w_q_b[j],
                              mla_kv_norm_g[j], mla_w_kv_b[j], hy_conv_w[j], hy_conv_b[j], hy_w1[j],
                              hy_b1[j], hy_w2[j], hy_b2[j], hy_w3[j], hy_bias[j], ab_w_out[j], need_ctx)
        else:
            y, y_c = hgrn2_mixer(h, hc, lb_all[layer], hg_w_in[j], hg_norm_g[j], hg_w_out[j], need_ctx)
        x = x + g1 * y
        x = x + g2 * conv_ffn(modulate(x, norm2_g[layer], sh2, sc2), ffn_w_up[layer],
                              ffn_conv_w[layer], ffn_conv_b[layer], ffn_w_down[layer])
        if need_ctx:
            xc = xc + g1c * y_c
            xc = xc + g2c * conv_ffn(modulate(xc, norm2_g[layer], sh2c, sc2c), ffn_w_up[layer],
                                     ffn_conv_w[layer], ffn_conv_b[layer], ffn_w_down[layer])
    return rmsnorm(x, final_norm_g)
```

```python
import functools
import math

import numpy as np
import jax
import jax.numpy as jnp
from jax import lax
from jax.experimental import pallas as pl
from jax.experimental.pallas import tpu as pltpu

F32 = jnp.float32
BF16 = jnp.bfloat16
HIGHEST = lax.Precision.HIGHEST

EPS = 1e-6
GRID_W = 64

MLA_NOPE = 64
MLA_ROPE = 32
MLA_V = 64
ROPE_NF = MLA_ROPE // 4
ROPE_BASE = 10000.0
HEAD_LANES = 128

HY_ORDER = 2
HY_BANDS = 16
HY_SIN_FREQ = 1.0
HY_DECAY_TARGET = 1e-2
HY_FAST_DECAY = 0.3
HY_SLOW_DECAY = 1.5
HY_MIN_DECAY = math.log(HY_DECAY_TARGET) / HY_SLOW_DECAY
HY_MAX_DECAY = math.log(HY_DECAY_TARGET) / HY_FAST_DECAY

HG_EXPAND = 128
HG_CHUNK = 64
HG_LEVELS = (32, 16, 8, 4, 2, 1)

MOD_ROWS = 16
VMEM_LIMIT = 56 * 1024 * 1024

_NT = (((1,), (1,)), ((), ()))
_TN = (((0,), (0,)), ((), ()))


def _cparams(*sem):
    return pltpu.CompilerParams(dimension_semantics=sem, vmem_limit_bytes=VMEM_LIMIT)


def _dot(a, b, dims=(((1,), (0,)), ((), ()))):
    return lax.dot_general(a, b, dims, precision=lax.Precision.DEFAULT, preferred_element_type=F32)


def _rms(x, g):
    return x * lax.rsqrt(jnp.mean(x * x, axis=-1, keepdims=True) + EPS) * g


def _modulate(x, g, shift, scale):
    return _rms(x, g) * (1.0 + scale) + shift


def _silu(x):
    return x * jax.nn.sigmoid(x)


def _mod_spec(row_fn, k, d):
    return pl.BlockSpec((None, None, 1, d), lambda *idx: (row_fn(*idx), k, 0, 0))


def _resident(shape):
    nd = len(shape)
    return pl.BlockSpec(shape, lambda *idx: (0,) * nd, pipeline_mode=pl.Buffered(1))


def _mod_kernel(c_ref, w_ref, b_ref, o_ref):
    a = _silu(c_ref[...])
    o_ref[...] = jnp.dot(a, w_ref[...], precision=HIGHEST, preferred_element_type=F32) + b_ref[...]


def _mod_vectors(cc, mod_w, mod_b):
    depth, d, n6 = mod_w.shape
    tn = 512
    out = pl.pallas_call(
        _mod_kernel,
        grid=(depth, n6 // tn),
        in_specs=[pl.BlockSpec((MOD_ROWS, d), lambda l, j: (0, 0)),
                  pl.BlockSpec((None, d, tn), lambda l, j: (l, 0, j)),
                  pl.BlockSpec((None, 1, tn), lambda l, j: (l, 0, j))],
        out_specs=pl.BlockSpec((None, MOD_ROWS, tn), lambda l, j: (l, 0, j)),
        out_shape=jax.ShapeDtypeStruct((depth, MOD_ROWS, n6), F32),
        compiler_params=_cparams("parallel", "parallel"),
        name="mod_vectors",
    )(cc, mod_w, mod_b.reshape(depth, 1, n6))
    return out.reshape(depth, MOD_ROWS, 6, 1, d)


def _ab_proj_kernel(*refs, rope, q_lora, kv_lora, heads, scale):
    if rope:
        (x_ref, sh_ref, sc_ref, g_ref, win_ref, qg_ref, wqa_ref, wqb_ref, kvg_ref, wk_ref, wv_ref,
         cos_ref, sin_ref, q_ref, k_ref, v_ref, hy_ref) = refs
    else:
        (x_ref, sh_ref, sc_ref, g_ref, win_ref, qg_ref, wqa_ref, kvg_ref, wk_ref, wv_ref,
         q_ref, k_ref, v_ref, hy_ref) = refs
    h = _modulate(x_ref[...], g_ref[...], sh_ref[...], sc_ref[...]).astype(BF16)
    z = _dot(h, win_ref[...])
    o = q_lora + kv_lora
    hy_ref[...] = z[:, o + 2 * HEAD_LANES:]
    qn = _rms(z[:, :q_lora], qg_ref[...]).astype(BF16)
    kvn = _rms(z[:, q_lora:o], kvg_ref[...]).astype(BF16)
    v_ref[...] = _dot(kvn, wv_ref[...]).astype(BF16)
    qa = _dot(qn, wqa_ref[...])
    kk = _dot(kvn, wk_ref[...])
    kr = z[:, o:o + HEAD_LANES]
    if rope:
        cos = cos_ref[...]
        sin = sin_ref[...]
        qb = _dot(qn, wqb_ref[...])
        kr = kr * cos + z[:, o + HEAD_LANES:o + 2 * HEAD_LANES] * sin
    for hh in range(heads):
        sl = slice(hh * HEAD_LANES, (hh + 1) * HEAD_LANES)
        qh = qa[:, sl]
        if rope:
            qh = qh * cos + qb[:, sl] * sin
        q_ref[:, sl] = (qh * scale).astype(BF16)
        k_ref[:, sl] = (kk[:, sl] + kr).astype(BF16)


def _rope_swap(w):
    nf = ROPE_NF
    return jnp.concatenate([-w[:, nf:2 * nf], w[:, :nf], -w[:, 3 * nf:4 * nf], w[:, 2 * nf:3 * nf]], axis=1)


def _ab_weights(w_in, w_q_b, w_kv_b, w_out, q_lora, kv_lora, hy_ch):
    d = w_in.shape[0]
    heads = w_q_b.shape[1] // (MLA_NOPE + MLA_ROPE)
    o = q_lora + kv_lora
    w_kr = w_in[:, o:o + MLA_ROPE]
    zpad = lambda n: jnp.zeros((d, n), F32)
    tail = HEAD_LANES - MLA_NOPE - MLA_ROPE
    win = jnp.concatenate(
        [w_in[:, :o],
         zpad(MLA_NOPE), w_kr, zpad(tail),
         zpad(MLA_NOPE), _rope_swap(w_kr), zpad(tail),
         w_in[:, o + MLA_ROPE:]], axis=1).astype(BF16)
    wq = w_q_b.reshape(q_lora, heads, MLA_NOPE + MLA_ROPE)
    zq = jnp.zeros((q_lora, heads, tail), F32)
    wqa = jnp.concatenate([wq, zq], axis=2).reshape(q_lora, heads * HEAD_LANES).astype(BF16)
    wq_rope = wq[:, :, MLA_NOPE:].reshape(q_lora * heads, MLA_ROPE)
    wqb = jnp.concatenate([jnp.zeros((q_lora, heads, MLA_NOPE), F32),
                           _rope_swap(wq_rope).reshape(q_lora, heads, MLA_ROPE), zq],
                          axis=2).reshape(q_lora, heads * HEAD_LANES).astype(BF16)
    wkv = w_kv_b.reshape(kv_lora, heads, MLA_NOPE + MLA_V)
    zk = jnp.zeros((kv_lora, heads, HEAD_LANES - MLA_NOPE), F32)
    wk = jnp.concatenate([wkv[:, :, :MLA_NOPE], zk], axis=2).reshape(kv_lora, heads * HEAD_LANES).astype(BF16)
    zv = jnp.zeros((kv_lora, heads, HEAD_LANES - MLA_V), F32)
    wv = jnp.concatenate([wkv[:, :, MLA_NOPE:], zv], axis=2).reshape(kv_lora, heads * HEAD_LANES).astype(BF16)
    wo = w_out[:heads * MLA_V].reshape(heads, MLA_V, d)
    wo_a = jnp.concatenate([wo, jnp.zeros((heads, HEAD_LANES - MLA_V, d), F32)],
                           axis=1).reshape(heads * HEAD_LANES, d).astype(BF16)
    wo_h = w_out[heads * MLA_V:].astype(BF16)
    return dict(win=win, wqa=wqa, wqb=wqb, wk=wk, wv=wv, wo_a=wo_a, wo_h=wo_h, heads=heads)


def _rope_tables(n):
    rows = n // GRID_W
    row = jnp.repeat(jnp.arange(rows), GRID_W).astype(F32)
    col = jnp.tile(jnp.arange(GRID_W), rows).astype(F32)
    inv = ROPE_BASE ** (-jnp.arange(ROPE_NF, dtype=F32) / ROPE_NF)
    ar = row[:, None] * inv
    ac = col[:, None] * inv
    ones = jnp.ones((n, MLA_NOPE), F32)
    tail = HEAD_LANES - MLA_NOPE - MLA_ROPE
    cos = jnp.concatenate([ones, jnp.cos(ar), jnp.cos(ar), jnp.cos(ac), jnp.cos(ac),
                           jnp.ones((n, tail), F32)], axis=1)
    sin = jnp.concatenate([0 * ones, jnp.sin(ar), jnp.sin(ar), jnp.sin(ac), jnp.sin(ac),
                           jnp.zeros((n, tail), F32)], axis=1)
    return cos, sin


def _ab_project(x, mods, row_fn, norm_g, w, q_norm_g, kv_norm_g, rope_tabs, hy_cols):
    b, n, d = x.shape
    tm = min(512, n)
    heads = w["heads"]
    q_lora = q_norm_g.shape[-1]
    kv_lora = kv_norm_g.shape[-1]
    hw = heads * HEAD_LANES
    rope = rope_tabs is not None
    tok = lambda width: pl.BlockSpec((None, tm, width), lambda bi, i: (bi, i, 0))
    in_specs = [tok(d), _mod_spec(row_fn, 0, d), _mod_spec(row_fn, 1, d), _resident((1, d)),
                _resident(w["win"].shape), _resident((1, q_lora)), _resident(w["wqa"].shape)]
    args = [x, mods, mods, norm_g.reshape(1, d), w["win"], q_norm_g.reshape(1, q_lora), w["wqa"]]
    if rope:
        in_specs.append(_resident(w["wqb"].shape))
        args.append(w["wqb"])
    in_specs += [_resident((1, kv_lora)), _resident(w["wk"].shape), _resident(w["wv"].shape)]
    args += [kv_norm_g.reshape(1, kv_lora), w["wk"], w["wv"]]
    if rope:
        in_specs += [pl.BlockSpec((tm, HEAD_LANES), lambda bi, i: (i, 0))] * 2
        args += list(rope_tabs)
    kern = functools.partial(_ab_proj_kernel, rope=rope, q_lora=q_lora, kv_lora=kv_lora, heads=heads,
                             scale=1.0 / math.sqrt(MLA_NOPE + MLA_ROPE))
    return pl.pallas_call(
        kern,
        grid=(b, n // tm),
        in_specs=in_specs,
        out_specs=[tok(hw), tok(hw), tok(hw), tok(hy_cols)],
        out_shape=[jax.ShapeDtypeStruct((b, n, hw), BF16)] * 3 + [jax.ShapeDtypeStruct((b, n, hy_cols), F32)],
        compiler_params=_cparams("parallel", "parallel"),
        name="ab_project",
    )(*args)


def _attn_kernel(q_ref, k_ref, v_ref, o_ref):
    s = _dot(q_ref[...], k_ref[...], _NT)
    p = jnp.exp(s - jnp.max(s, axis=-1, keepdims=True))
    l = jnp.sum(p, axis=-1, keepdims=True)
    o_ref[...] = (_dot(p.astype(BF16), v_ref[...]) / l).astype(o_ref.dtype)


def _attention(q, k, v, heads):
    b, n, hw = q.shape
    s = k.shape[1]
    tq = min(512, n)
    return pl.pallas_call(
        _attn_kernel,
        grid=(b, heads, n // tq),
        in_specs=[pl.BlockSpec((None, tq, HEAD_LANES), lambda bi, h, i: (bi, i, h)),
                  pl.BlockSpec((None, s, HEAD_LANES), lambda bi, h, i: (bi, 0, h)),
                  pl.BlockSpec((None, s, HEAD_LANES), lambda bi, h, i: (bi, 0, h))],
        out_specs=pl.BlockSpec((None, tq, HEAD_LANES), lambda bi, h, i: (bi, i, h)),
        out_shape=jax.ShapeDtypeStruct((b, n, hw), BF16),
        compiler_params=_cparams("parallel", "parallel", "parallel"),
        name="attention",
    )(q, k, v)


def _shift_rows(a, direction):
    n = a.shape[0]
    row = lax.broadcasted_iota(jnp.int32, a.shape, 0)
    if direction > 0:
        return jnp.where(row == 0, 0.0, pltpu.roll(a, 1, axis=0))
    return jnp.where(row == n - 1, 0.0, pltpu.roll(a, n - 1, axis=0))


def _dwconv_kernel(x_ref, w_ref, b_ref, o_ref, ob_ref):
    x = x_ref[...]
    w = w_ref[...]
    y = _shift_rows(x, 1) * w[0:1] + x * w[1:2] + _shift_rows(x, -1) * w[2:3] + b_ref[...]
    o_ref[...] = y
    ob_ref[...] = y.astype(BF16)


def _hy_dwconv(hy, conv_w, conv_b):
    b, n, c3 = hy.shape
    tc = 256 if c3 % 256 == 0 else 128
    assert c3 % tc == 0
    blk = pl.BlockSpec((None, n, tc), lambda bi, j: (bi, 0, j))
    return pl.pallas_call(
        _dwconv_kernel,
        grid=(b, c3 // tc),
        in_specs=[blk, pl.BlockSpec((3, tc), lambda bi, j: (0, j)), pl.BlockSpec((1, tc), lambda bi, j: (0, j))],
        out_specs=[blk, blk],
        out_shape=[jax.ShapeDtypeStruct((b, n, c3), F32), jax.ShapeDtypeStruct((b, n, c3), BF16)],
        compiler_params=_cparams("parallel", "parallel"),
        name="hy_dwconv",
    )(hy, conv_w, conv_b.reshape(1, c3))


def _hy_filter_kernel(z_ref, t_ref, w1_ref, b1_ref, w2_ref, b2_ref, w3_ref, dl_ref, o_ref, *, ch):
    hdot = lambda a, w: jnp.dot(a, w, precision=HIGHEST, preferred_element_type=F32)
    a = jnp.sin(HY_SIN_FREQ * (hdot(z_ref[...], w1_ref[...]) + b1_ref[...]))
    a = jnp.sin(HY_SIN_FREQ * (hdot(a, w2_ref[...]) + b2_ref[...]))
    f = hdot(a, w3_ref[...])
    dec = jnp.exp(-t_ref[...] * dl_ref[...])
    tl = z_ref.shape[0]
    row = pl.program_id(0) * tl + lax.broadcasted_iota(jnp.int32, (tl, ch), 0)
    for o in range(HY_ORDER):
        hf = f[:, (2 * o) * ch:(2 * o + 1) * ch] * dec
        hb = jnp.where(row > 0, f[:, (2 * o + 1) * ch:(2 * o + 2) * ch] * dec, 0.0)
        o_ref[:, (2 * o) * ch:(2 * o + 1) * ch] = (hf + hb).astype(BF16)
        o_ref[:, (2 * o + 1) * ch:(2 * o + 2) * ch] = (hf - hb).astype(BF16)


def _hy_filter_inputs(length, w1, b1, w2, b2, w3, ch):
    t = jnp.linspace(0.0, 1.0, length, dtype=F32)[:, None]
    wv = 2.0 * math.pi * jnp.arange(length, dtype=F32)[:, None] / length
    f = jnp.linspace(1e-4, HY_BANDS - 1, HY_BANDS, dtype=F32)
    z = jnp.concatenate([t, jnp.cos(f * wv), -jnp.sin(f * wv)], axis=-1)
    emb = z.shape[1]
    z = jnp.pad(z, ((0, 0), (0, HEAD_LANES - emb)))
    w1p = jnp.pad(w1, ((0, HEAD_LANES - emb), (0, 0)))
    deltas = jnp.abs(jnp.linspace(HY_MIN_DECAY, HY_MAX_DECAY, ch, dtype=F32))[None, :]
    fw = w1.shape[1]
    tl = min(512, length)
    return pl.pallas_call(
        functools.partial(_hy_filter_kernel, ch=ch),
        grid=(length // tl,),
        in_specs=[pl.BlockSpec((tl, HEAD_LANES), lambda i: (i, 0)), pl.BlockSpec((tl, 1), lambda i: (i, 0)),
                  _resident((HEAD_LANES, fw)), _resident((1, fw)), _resident((fw, fw)), _resident((1, fw)),
                  _resident(w3.shape), _resident((1, ch))],
        out_specs=pl.BlockSpec((tl, 2 * HY_ORDER * ch), lambda i: (i, 0)),
        out_shape=jax.ShapeDtypeStruct((length, 2 * HY_ORDER * ch), BF16),
        compiler_params=_cparams("parallel"),
        name="hy_filter",
    )(z, t, w1p, b1.reshape(1, fw), w2, b2.reshape(1, fw), w3, deltas)


def _dft_tables(length):
    n2 = 2 * length
    k = jnp.arange(length, dtype=jnp.int32)
    idx = (k[:, None] * k[None, :]) % n2
    ang = idx.astype(F32) * (2.0 * math.pi / n2)
    fc = jnp.cos(ang)
    nyq = jnp.where(k % 2 == 0, 1.0, -1.0).astype(F32)
    msin = -jnp.sin(ang)
    fs = jnp.where(k[:, None] == 0, nyq[None, :], msin)
    fst = jnp.where(k[None, :] == 0, nyq[:, None], msin)
    return fc.astype(BF16), fs.astype(BF16), fst.astype(BF16)


def _filter_dft_kernel(hs_ref, hd_ref, fc_ref, fs_ref, o_ref):
    hs = hs_ref[...]
    o_ref[0] = _dot(fc_ref[...], hs)
    ki = _dot(fs_ref[...], hd_ref[...])
    nyq = _dot(fs_ref[0:8, :], hs)[0:1]
    tf = ki.shape[0]
    row = pl.program_id(1) * tf + lax.broadcasted_iota(jnp.int32, (tf, 1), 0)
    o_ref[1] = jnp.where(row == 0, nyq, ki)


def _dft_spec_kernel(x_ref, fc_ref, fs_ref, k_ref, o_ref, *, inv_n):
    x = x_ref[...]
    ur = _dot(fc_ref[...], x)
    ui = _dot(fs_ref[...], x)
    kr = k_ref[0]
    ki = k_ref[1]
    tf = ur.shape[0]
    row = pl.program_id(1) * tf + lax.broadcasted_iota(jnp.int32, (tf, 1), 0)
    first = row == 0
    uiki = ui * ki
    yr = ur * kr - jnp.where(first, 0.0, uiki)
    yi = jnp.where(first, uiki, ur * ki + ui * kr)
    scale = jnp.where(first, inv_n, 2.0 * inv_n)
    o_ref[0] = (yr * scale).astype(BF16)
    o_ref[1] = (yi * scale).astype(BF16)


def _filter_spectrum(p, fc, fs, ch):
    length, cols = p.shape
    tf = min(512, length)
    return pl.pallas_call(
        _filter_dft_kernel,
        grid=(HY_ORDER, length // tf),
        in_specs=[pl.BlockSpec((length, ch), lambda o, i: (0, 2 * o)),
                  pl.BlockSpec((length, ch), lambda o, i: (0, 2 * o + 1)),
                  pl.BlockSpec((tf, length), lambda o, i: (i, 0)),
                  pl.BlockSpec((tf, length), lambda o, i: (i, 0))],
        out_specs=pl.BlockSpec((None, 2, tf, ch), lambda o, i: (o, 0, i, 0)),
        out_shape=jax.ShapeDtypeStruct((HY_ORDER, 2, length, ch), F32),
        compiler_params=_cparams("parallel", "arbitrary"),
        name="hy_filter_dft",
    )(p, p, fc, fs)


def _hy_spectral(xb, col, ch, fc, fs, kspec):
    b, length, _ = xb.shape
    tf = min(512, length)
    return pl.pallas_call(
        functools.partial(_dft_spec_kernel, inv_n=1.0 / (2 * length)),
        grid=(b, length // tf),
        in_specs=[pl.BlockSpec((None, length, ch), lambda bi, i: (bi, 0, col)),
                  pl.BlockSpec((tf, length), lambda bi, i: (i, 0)),
                  pl.BlockSpec((tf, length), lambda bi, i: (i, 0)),
                  pl.BlockSpec((2, tf, ch), lambda bi, i: (0, i, 0))],
        out_specs=pl.BlockSpec((None, 2, tf, ch), lambda bi, i: (bi, 0, i, 0)),
        out_shape=jax.ShapeDtypeStruct((b, 2, length, ch), BF16),
        compiler_params=_cparams("parallel", "arbitrary"),
        name="hy_dft_spectral",
    )(xb, fc, fs, kspec)


def _idft_gate_kernel(y_ref, fc_ref, fst_ref, u_ref, xg_ref, d_ref, o_ref, ob_ref):
    conv = _dot(fc_ref[...], y_ref[0]) + _dot(fst_ref[...], y_ref[1])
    out = xg_ref[...] * (conv + u_ref[...] * d_ref[...])
    o_ref[...] = out
    ob_ref[...] = out.astype(BF16)


def _hy_inverse_gate(y, fc, fst, u, u_col, xg, xg_col, dvec, ch):
    b, _, length, _ = y.shape
    tt = min(512, length)
    out = pl.BlockSpec((None, tt, ch), lambda bi, i: (bi, i, 0))
    return pl.pallas_call(
        _idft_gate_kernel,
        grid=(b, length // tt),
        in_specs=[pl.BlockSpec((None, 2, length, ch), lambda bi, i: (bi, 0, 0, 0)),
                  pl.BlockSpec((tt, length), lambda bi, i: (i, 0)),
                  pl.BlockSpec((tt, length), lambda bi, i: (i, 0)),
                  pl.BlockSpec((None, tt, ch), lambda bi, i: (bi, i, u_col)),
                  pl.BlockSpec((None, tt, ch), lambda bi, i: (bi, i, xg_col)),
                  pl.BlockSpec((1, ch), lambda bi, i: (0, 0))],
        out_specs=[out, out],
        out_shape=[jax.ShapeDtypeStruct((b, length, ch), F32), jax.ShapeDtypeStruct((b, length, ch), BF16)],
        compiler_params=_cparams("parallel", "arbitrary"),
        name="hy_idft_gate",
    )(y, fc, fst, u, xg, dvec.reshape(1, ch))


def _hyena(hy, conv_w, conv_b, w1, b1, w2, b2, w3, hy_bias):
    ch = hy_bias.shape[-1]
    length = hy.shape[1]
    hyc, hyb = _hy_dwconv(hy, conv_w, conv_b)
    fc, fs, fst = _dft_tables(length)
    kspec = _filter_spectrum(_hy_filter_inputs(length, w1, b1, w2, b2, w3, ch), fc, fs, ch)
    y = _hy_spectral(hyb, 0, ch, fc, fs, kspec[0])
    y1, y1b = _hy_inverse_gate(y, fc, fst, hyc, 0, hyc, 1, hy_bias[0], ch)
    y = _hy_spectral(y1b, 0, ch, fc, fs, kspec[1])
    y2, _ = _hy_inverse_gate(y, fc, fst, y1, 0, hyc, 2, hy_bias[1], ch)
    return y2


def _ab_out_kernel(o_ref, y_ref, x_ref, g1_ref, woa_ref, woh_ref, out_ref):
    y = _dot(o_ref[...], woa_ref[...]) + _dot(y_ref[...].astype(BF16), woh_ref[...])
    out_ref[...] = x_ref[...] + g1_ref[...] * y


def _ab_out(o, y2, x, mods, row_fn, w):
    b, n, d = x.shape
    tm = min(512, n)
    tok = lambda width: pl.BlockSpec((None, tm, width), lambda bi, i: (bi, i, 0))
    return pl.pallas_call(
        _ab_out_kernel,
        grid=(b, n // tm),
        in_specs=[tok(o.shape[-1]), tok(y2.shape[-1]), tok(d), _mod_spec(row_fn, 2, d),
                  _resident(w["wo_a"].shape), _resident(w["wo_h"].shape)],
        out_specs=tok(d),
        out_shape=jax.ShapeDtypeStruct((b, n, d), F32),
        compiler_params=_cparams("parallel", "parallel"),
        name="ab_out",
    )(o, y2, x, mods, w["wo_a"], w["wo_h"])


HALO = 8


def _ffn_kernel(*refs, hidden, hc, final):
    if final:
        (x_ref, xp_ref, xn_ref, sh_ref, sc_ref, g2_ref, ng_ref, wup_ref, cw_ref, cb_ref, wdn_ref,
         fg_ref, o_ref) = refs
    else:
        (x_ref, xp_ref, xn_ref, sh_ref, sc_ref, g2_ref, ng_ref, wup_ref, cw_ref, cb_ref, wdn_ref,
         o_ref) = refs
    i = pl.program_id(1)
    last = pl.num_programs(1) - 1
    x = x_ref[...]
    tm, d = x.shape
    mod = lambda a: _modulate(a, ng_ref[...], sh_ref[...], sc_ref[...])
    hp = mod(xp_ref[...]) * (i > 0).astype(F32)
    hn = mod(xn_ref[...]) * (i < last).astype(F32)
    he = jnp.concatenate([hp, mod(x), hn], axis=0).astype(BF16)
    ne = tm + 2 * HALO

    def conv(a, col):
        w = cw_ref[:, col:col + hc]
        prev = pltpu.roll(a, 1, axis=0)[HALO:HALO + tm]
        nxt = pltpu.roll(a, ne - 1, axis=0)[HALO:HALO + tm]
        return prev * w[0:1] + a[HALO:HALO + tm] * w[1:2] + nxt * w[2:3] + cb_ref[:, col:col + hc]

    acc = jnp.zeros((tm, d), F32)
    for j in range(hidden // hc):
        gate = conv(_dot(he, wup_ref[:, j * hc:(j + 1) * hc]), j * hc)
        val = conv(_dot(he, wup_ref[:, hidden + j * hc:hidden + (j + 1) * hc]), hidden + j * hc)
        mid = (_silu(gate) * val).astype(BF16)
        acc = acc + _dot(mid, wdn_ref[j * hc:(j + 1) * hc, :])
    out = x + g2_ref[...] * acc
    if final:
        out = _rms(out, fg_ref[...])
    o_ref[...] = out


def _ffn_chunk(hidden):
    for hc in (512, 256, 128):
        if hidden % hc == 0:
            return hc
    raise ValueError(f"FFN hidden width {hidden} is not a multiple of 128")


def _conv_ffn(x, mods, row_fn, norm_g, w_up, conv_w, conv_b, w_down, final_g=None):
    b, n, d = x.shape
    hidden = w_down.shape[0]
    tm = min(512, n)
    nh = n // HALO
    tpb = tm // HALO
    final = final_g is not None
    tok = pl.BlockSpec((None, tm, d), lambda bi, i: (bi, i, 0))
    in_specs = [tok,
                pl.BlockSpec((None, HALO, d), lambda bi, i: (bi, jnp.maximum(i * tpb - 1, 0), 0)),
                pl.BlockSpec((None, HALO, d), lambda bi, i: (bi, jnp.minimum((i + 1) * tpb, nh - 1), 0)),
                _mod_spec(row_fn, 3, d), _mod_spec(row_fn, 4, d), _mod_spec(row_fn, 5, d),
                _resident((1, d)), _resident(w_up.shape), _resident(conv_w.shape),
                _resident((1, 2 * hidden)), _resident(w_down.shape)]
    args = [x, x, x, mods, mods, mods, norm_g.reshape(1, d), w_up, conv_w, conv_b.reshape(1, 2 * hidden), w_down]
    if final:
        in_specs.append(_resident((1, d)))
        args.append(final_g.reshape(1, d))
    return pl.pallas_call(
        functools.partial(_ffn_kernel, hidden=hidden, hc=_ffn_chunk(hidden), final=final),
        grid=(b, n // tm),
        in_specs=in_specs,
        out_specs=tok,
        out_shape=jax.ShapeDtypeStruct((b, n, d), F32),
        compiler_params=_cparams("parallel", "parallel"),
        name="conv_ffn",
    )(*args)


def _hg_proj_kernel(x_ref, sh_ref, sc_ref, g_ref, w_ref, o_ref, *, width):
    h = _modulate(x_ref[...], g_ref[...], sh_ref[...], sc_ref[...]).astype(BF16)
    z = _dot(h, w_ref[...])
    o_ref[:, :width] = _silu(z[:, :width])
    o_ref[:, width:] = z[:, width:]


def _hg_project(x, mods, row_fn, norm_g, w_in):
    b, n, d = x.shape
    cols = w_in.shape[1]
    tm = min(512, n)
    tok = lambda width: pl.BlockSpec((None, tm, width), lambda bi, i: (bi, i, 0))
    return pl.pallas_call(
        functools.partial(_hg_proj_kernel, width=cols // 5),
        grid=(b, n // tm),
        in_specs=[tok(d), _mod_spec(row_fn, 0, d), _mod_spec(row_fn, 1, d), _resident((1, d)),
                  _resident(w_in.shape)],
        out_specs=tok(cols),
        out_shape=jax.ShapeDtypeStruct((b, n, cols), F32),
        compiler_params=_cparams("parallel", "parallel"),
        name="hg_project",
    )(x, mods, mods, norm_g.reshape(1, d), w_in)


def _scan_tables(direction):
    c = HG_CHUNK
    idx = np.arange(c)
    ops = []
    level = np.full((c, c), -1, np.int32)
    for li, m in enumerate(HG_LEVELS):
        qa = np.zeros((c, c), np.float32)
        ka = np.zeros((c, c), np.float32)
        for r in range(c):
            lo = (r // m) * m
            hi = lo + m - 1
            upper = (r % (2 * m)) >= m
            if direction == 0:
                if upper:
                    qa[r, lo:r + 1] = 1
                else:
                    ka[r, r + 1:hi + 1] = 1
            else:
                if not upper:
                    qa[r, r:hi + 1] = 1
                else:
                    ka[r, lo:r] = 1
        ops += [qa, ka]
        same = (idx[:, None] // (2 * m)) == (idx[None, :] // (2 * m))
        up = (idx % (2 * m)) >= m
        pair = same & (up[:, None] & ~up[None, :] if direction == 0 else ~up[:, None] & up[None, :])
        level[pair] = li
    level[idx, idx] = len(HG_LEVELS)
    if direction == 0:
        to_q = (idx[None, :] <= idx[:, None])
        from_k = (idx[None, :] > idx[:, None])
    else:
        to_q = (idx[None, :] >= idx[:, None])
        from_k = (idx[None, :] < idx[:, None])
    ops += [to_q.astype(np.float32), from_k.astype(np.float32), np.ones((8, c), np.float32)]
    return jnp.asarray(np.concatenate(ops, axis=0), BF16), jnp.asarray(level)


def _split3(g):
    hi = g.astype(BF16)
    r = g - hi.astype(F32)
    mid = r.astype(BF16)
    lo = (r - mid.astype(F32)).astype(BF16)
    return hi, mid, lo


def _gla_chunk(q, fr, val, lb, em, lv, st):
    c = HG_CHUNK
    nl = len(HG_LEVELS)
    log_f = jnp.logaddexp(jnp.log(lb), jnp.log1p(-lb) + jax.nn.log_sigmoid(fr))
    key = (1.0 - lb) * jax.nn.sigmoid(-fr)
    e = sum(_dot(em, part) for part in _split3(log_f))
    att = jnp.zeros((c, c), F32)
    for li in range(nl):
        ql = (q * jnp.exp(e[(2 * li) * c:(2 * li + 1) * c])).astype(BF16)
        kl = (key * jnp.exp(e[(2 * li + 1) * c:(2 * li + 2) * c])).astype(BF16)
        att = jnp.where(lv == li, _dot(ql, kl, _NT), att)
    att = jnp.where(lv == nl, jnp.sum(q * key, axis=1, keepdims=True), att)
    vb = val.astype(BF16)
    base = 2 * nl * c
    q_in = (q * jnp.exp(e[base:base + c])).astype(BF16)
    o = _dot(att.astype(BF16), vb) + _dot(q_in, st.astype(BF16), _NT)
    k_out = (key * jnp.exp(e[base + c:base + 2 * c])).astype(BF16)
    st = st * jnp.exp(e[base + 2 * c:base + 2 * c + 1]) + _dot(vb, k_out, _TN)
    return o, st


def _hg_scan_kernel(*refs, layer, need_ctx):
    (lg_ref, emf_ref, emb_ref, lvf_ref, lvb_ref,
     qc_ref, ffc_ref, fbc_ref, ic_ref, q_ref, ff_ref, fb_ref, i_ref) = refs[:13]
    if need_ctx:
        o_ref, oc_ref, st_ref = refs[13:]
    else:
        o_ref, st_ref = refs[13:]
        oc_ref = None
    c = HG_CHUNK
    lg = lg_ref[...]
    ex = jnp.exp(lg - jnp.max(lg, axis=0, keepdims=True))
    probs = ex / jnp.sum(ex, axis=0, keepdims=True)
    acc = probs[0]
    for l in range(1, layer + 1):
        acc = acc + probs[l]
    lb_all = acc - probs[0]
    lb_f = lb_all[0:1]
    lb_b = lb_all[1:2]

    st_ref[...] = jnp.zeros_like(st_ref)
    o_ref[...] = jnp.zeros_like(o_ref)
    if need_ctx:
        oc_ref[...] = jnp.zeros_like(oc_ref)

    def run(qr, ffr, fbr, ir, outr):
        nc = qr.shape[0] // c

        def body(ci, carry):
            rf = pl.multiple_of(ci * c, c)
            rb = pl.multiple_of((nc - 1 - ci) * c, c)
            of, stf = _gla_chunk(qr[pl.ds(rf, c), :], ffr[pl.ds(rf, c), :], ir[pl.ds(rf, c), :],
                                 lb_f, emf_ref[...], lvf_ref[...], st_ref[0])
            st_ref[0] = stf
            ob, stb = _gla_chunk(qr[pl.ds(rb, c), :], fbr[pl.ds(rb, c), :], ir[pl.ds(rb, c), :],
                                 lb_b, emb_ref[...], lvb_ref[...], st_ref[1])
            st_ref[1] = stb
            if outr is not None:
                outr[pl.ds(rf, c), :] += of
                outr[pl.ds(rb, c), :] += ob
            return carry

        lax.fori_loop(0, nc, body, 0)

    run(qc_ref, ffc_ref, fbc_ref, ic_ref, oc_ref)
    run(q_ref, ff_ref, fb_ref, i_ref, o_ref)


def _hg_scan(z, zc, lb_logits, layer, need_ctx):
    b, n, cols = z.shape
    nc_ = zc.shape[1]
    width = cols // 5
    heads = width // HG_EXPAND
    e = HG_EXPAND
    depth = lb_logits.shape[0]
    emf, lvf = _scan_tables(0)
    emb, lvb = _scan_tables(1)
    col = lambda rows, group: pl.BlockSpec((None, rows, e), lambda bi, h: (bi, 0, group * heads + h))
    in_specs = [pl.BlockSpec((depth, 2, e), lambda bi, h: (0, 0, h)),
                _resident(emf.shape), _resident(emb.shape), _resident(lvf.shape), _resident(lvb.shape),
                col(nc_, 0), col(nc_, 1), col(nc_, 2), col(nc_, 3),
                col(n, 0), col(n, 1), col(n, 2), col(n, 3)]
    out_specs = [pl.BlockSpec((None, n, e), lambda bi, h: (bi, 0, h))]
    out_shape = [jax.ShapeDtypeStruct((b, n, width), F32)]
    if need_ctx:
        out_specs.append(pl.BlockSpec((None, nc_, e), lambda bi, h: (bi, 0, h)))
        out_shape.append(jax.ShapeDtypeStruct((b, nc_, width), F32))
    res = pl.pallas_call(
        functools.partial(_hg_scan_kernel, layer=layer, need_ctx=need_ctx),
        grid=(b, heads),
        in_specs=in_specs,
        out_specs=out_specs,
        out_shape=out_shape,
        scratch_shapes=[pltpu.VMEM((2, e, e), F32)],
        compiler_params=_cparams("parallel", "parallel"),
        name="hg_scan",
    )(lb_logits, emf, emb, lvf, lvb, zc, zc, zc, zc, z, z, z, z)
    return (res[0], res[1]) if need_ctx else (res[0], None)


def _hg_out_kernel(o_ref, g_ref, x_ref, g1_ref, ng_ref, w_ref, out_ref, *, heads):
    ng = ng_ref[...]
    parts = [_rms(o_ref[:, h * HG_EXPAND:(h + 1) * HG_EXPAND], ng) for h in range(heads)]
    o = jnp.concatenate(parts, axis=1) * _silu(g_ref[...])
    out_ref[...] = x_ref[...] + g1_ref[...] * _dot(o.astype(BF16), w_ref[...])


def _hg_out(o, z, x, mods, row_fn, norm_g, w_out):
    b, n, d = x.shape
    width = o.shape[-1]
    tm = min(512, n)
    tok = lambda wd: pl.BlockSpec((None, tm, wd), lambda bi, i: (bi, i, 0))
    return pl.pallas_call(
        functools.partial(_hg_out_kernel, heads=width // HG_EXPAND),
        grid=(b, n // tm),
        in_specs=[tok(width), pl.BlockSpec((None, tm, width), lambda bi, i: (bi, i, 4)), tok(d),
                  _mod_spec(row_fn, 2, d), _resident((1, HG_EXPAND)), _resident(w_out.shape)],
        out_specs=tok(d),
        out_shape=jax.ShapeDtypeStruct((b, n, d), F32),
        compiler_params=_cparams("parallel", "parallel"),
        name="hg_out",
    )(o, z, x, mods, norm_g.reshape(1, HG_EXPAND), w_out)


def kernel(x, c, ctx, c_ctx, mod_w, mod_b, norm1_g, norm2_g, ffn_w_up, ffn_conv_w, ffn_conv_b, ffn_w_down, ab_w_in, mla_q_norm_g, mla_w_q_b, mla_kv_norm_g, mla_w_kv_b, hy_conv_w, hy_conv_b, hy_w1, hy_b1, hy_w2, hy_b2, hy_w3, hy_bias, ab_w_out, hg_w_in, hg_lb_logits, hg_norm_g, hg_w_out, final_norm_g):
    batch, n, d = x.shape
    depth = mod_w.shape[0]
    assert batch < MOD_ROWS
    ctx_row = batch
    cc = jnp.zeros((MOD_ROWS, d), F32).at[:batch].set(c).at[ctx_row].set(c_ctx)
    mods_all = _mod_vectors(cc, mod_w, mod_b)
    lat_row = lambda bi, *_: bi
    ctx_row_fn = lambda bi, *_: ctx_row
    rope_tabs = _rope_tables(n)

    xc = ctx
    for layer in range(depth):
        need_ctx = layer < depth - 1
        j = layer // 2
        mods = mods_all[layer]
        if layer % 2 == 0:
            q_lora = mla_q_norm_g.shape[-1]
            kv_lora = mla_kv_norm_g.shape[-1]
            hy_ch = hy_bias.shape[-1]
            w = _ab_weights(ab_w_in[j], mla_w_q_b[j], mla_w_kv_b[j], ab_w_out[j], q_lora, kv_lora, hy_ch)
            hy_cols = (HY_ORDER + 1) * hy_ch
            hy_args = (hy_conv_w[j], hy_conv_b[j], hy_w1[j], hy_b1[j], hy_w2[j], hy_b2[j], hy_w3[j], hy_bias[j])
            q_c, k_c, v_c, hyin_c = _ab_project(xc, mods, ctx_row_fn, norm1_g[layer], w, mla_q_norm_g[j],
                                                mla_kv_norm_g[j], None, hy_cols)
            q, k, v, hyin = _ab_project(x, mods, lat_row, norm1_g[layer], w, mla_q_norm_g[j],
                                        mla_kv_norm_g[j], rope_tabs, hy_cols)
            o = _attention(q, jnp.concatenate([k_c, k], axis=1), jnp.concatenate([v_c, v], axis=1), w["heads"])
            x_new = _ab_out(o, _hyena(hyin, *hy_args), x, mods, lat_row, w)
            if need_ctx:
                o_c = _attention(q_c, k_c, v_c, w["heads"])
                xc = _ab_out(o_c, _hyena(hyin_c, *hy_args), xc, mods, ctx_row_fn, w)
            x = x_new
        else:
            w_in = hg_w_in[j].astype(BF16)
            z_c = _hg_project(xc, mods, ctx_row_fn, norm1_g[layer], w_in)
            z = _hg_project(x, mods, lat_row, norm1_g[layer], w_in)
            o, o_c = _hg_scan(z, z_c, hg_lb_logits, layer, need_ctx)
            w_out = hg_w_out[j].astype(BF16)
            x = _hg_out(o, z, x, mods, lat_row, hg_norm_g[j], w_out)
            if need_ctx:
                xc = _hg_out(o_c, z_c, xc, mods, ctx_row_fn, hg_norm_g[j], w_out)
        w_up = ffn_w_up[layer].astype(BF16)
        w_dn = ffn_w_down[layer].astype(BF16)
        last = layer == depth - 1
        x = _conv_ffn(x, mods, lat_row, norm2_g[layer], w_up, ffn_conv_w[layer], ffn_conv_b[layer], w_dn,
                      final_norm_g if last else None)
        if need_ctx:
            xc = _conv_ffn(xc, mods, ctx_row_fn, norm2_g[layer], w_up, ffn_conv_w[layer], ffn_conv_b[layer], w_dn)
    return x
```

```python
import functools
import math

import numpy as np
import jax
import jax.numpy as jnp
from jax import lax
from jax.experimental import pallas as pl
from jax.experimental.pallas import tpu as pltpu

F32 = jnp.float32
BF16 = jnp.bfloat16
HIGHEST = lax.Precision.HIGHEST

EPS = 1e-6
GRID_W = 64

MLA_NOPE = 64
MLA_ROPE = 32
MLA_V = 64
ROPE_NF = MLA_ROPE // 4
ROPE_BASE = 10000.0
HEAD_LANES = 128

HY_ORDER = 2
HY_BANDS = 16
HY_SIN_FREQ = 1.0
HY_DECAY_TARGET = 1e-2
HY_FAST_DECAY = 0.3
HY_SLOW_DECAY = 1.5
HY_MIN_DECAY = math.log(HY_DECAY_TARGET) / HY_SLOW_DECAY
HY_MAX_DECAY = math.log(HY_DECAY_TARGET) / HY_FAST_DECAY

HG_EXPAND = 128
HG_CHUNK = 64
HG_LEVELS = (32, 16, 8, 4, 2, 1)
HG_GROUP = 2
HG_UNROLL = 2

MOD_ROWS = 16
VMEM_LIMIT = 56 * 1024 * 1024

_NT = (((1,), (1,)), ((), ()))
_TN = (((0,), (0,)), ((), ()))


def _cparams(*sem):
    return pltpu.CompilerParams(dimension_semantics=sem, vmem_limit_bytes=VMEM_LIMIT)


def _dot(a, b, dims=(((1,), (0,)), ((), ()))):
    return lax.dot_general(a, b, dims, precision=lax.Precision.DEFAULT, preferred_element_type=F32)


def _rms(x, g):
    return x * lax.rsqrt(jnp.mean(x * x, axis=-1, keepdims=True) + EPS) * g


def _modulate(x, g, shift, scale):
    return _rms(x, g) * (1.0 + scale) + shift


def _silu(x):
    return x * jax.nn.sigmoid(x)


def _mod_spec(row_fn, k, d):
    return pl.BlockSpec((None, None, 1, d), lambda *idx: (row_fn(*idx), k, 0, 0))


def _resident(shape):
    nd = len(shape)
    return pl.BlockSpec(shape, lambda *idx: (0,) * nd, pipeline_mode=pl.Buffered(1))


def _mod_kernel(c_ref, w_ref, b_ref, o_ref):
    a = _silu(c_ref[...])
    o_ref[...] = jnp.dot(a, w_ref[...], precision=HIGHEST, preferred_element_type=F32) + b_ref[...]


def _mod_vectors(cc, mod_w, mod_b):
    depth, d, n6 = mod_w.shape
    tn = 512
    out = pl.pallas_call(
        _mod_kernel,
        grid=(depth, n6 // tn),
        in_specs=[pl.BlockSpec((MOD_ROWS, d), lambda l, j: (0, 0)),
                  pl.BlockSpec((None, d, tn), lambda l, j: (l, 0, j)),
                  pl.BlockSpec((None, 1, tn), lambda l, j: (l, 0, j))],
        out_specs=pl.BlockSpec((None, MOD_ROWS, tn), lambda l, j: (l, 0, j)),
        out_shape=jax.ShapeDtypeStruct((depth, MOD_ROWS, n6), F32),
        compiler_params=_cparams("parallel", "parallel"),
        name="mod_vectors",
    )(cc, mod_w, mod_b.reshape(depth, 1, n6))
    return out.reshape(depth, MOD_ROWS, 6, 1, d)


def _ab_proj_kernel(*refs, rope, q_lora, kv_lora, heads, scale):
    if rope:
        (x_ref, sh_ref, sc_ref, g_ref, win_ref, qg_ref, wqa_ref, wqb_ref, kvg_ref, wk_ref, wv_ref,
         cos_ref, sin_ref, q_ref, k_ref, v_ref, hy_ref) = refs
    else:
        (x_ref, sh_ref, sc_ref, g_ref, win_ref, qg_ref, wqa_ref, kvg_ref, wk_ref, wv_ref,
         q_ref, k_ref, v_ref, hy_ref) = refs
    h = _modulate(x_ref[...], g_ref[...], sh_ref[...], sc_ref[...]).astype(BF16)
    z = _dot(h, win_ref[...])
    o = q_lora + kv_lora
    hy_ref[...] = z[:, o + 2 * HEAD_LANES:]
    qn = _rms(z[:, :q_lora], qg_ref[...]).astype(BF16)
    kvn = _rms(z[:, q_lora:o], kvg_ref[...]).astype(BF16)
    v_ref[...] = _dot(kvn, wv_ref[...]).astype(BF16)
    qa = _dot(qn, wqa_ref[...])
    kk = _dot(kvn, wk_ref[...])
    kr = z[:, o:o + HEAD_LANES]
    if rope:
        cos = cos_ref[...]
        sin = sin_ref[...]
        qb = _dot(qn, wqb_ref[...])
        kr = kr * cos + z[:, o + HEAD_LANES:o + 2 * HEAD_LANES] * sin
    for hh in range(heads):
        sl = slice(hh * HEAD_LANES, (hh + 1) * HEAD_LANES)
        qh = qa[:, sl]
        if rope:
            qh = qh * cos + qb[:, sl] * sin
        q_ref[:, sl] = (qh * scale).astype(BF16)
        k_ref[:, sl] = (kk[:, sl] + kr).astype(BF16)


def _rope_swap(w):
    nf = ROPE_NF
    return jnp.concatenate([-w[:, nf:2 * nf], w[:, :nf], -w[:, 3 * nf:4 * nf], w[:, 2 * nf:3 * nf]], axis=1)


def _ab_weights(w_in, w_q_b, w_kv_b, w_out, q_lora, kv_lora, hy_ch):
    d = w_in.shape[0]
    heads = w_q_b.shape[1] // (MLA_NOPE + MLA_ROPE)
    o = q_lora + kv_lora
    w_kr = w_in[:, o:o + MLA_ROPE]
    zpad = lambda n: jnp.zeros((d, n), F32)
    tail = HEAD_LANES - MLA_NOPE - MLA_ROPE
    win = jnp.concatenate(
        [w_in[:, :o],
         zpad(MLA_NOPE), w_kr, zpad(tail),
         zpad(MLA_NOPE), _rope_swap(w_kr), zpad(tail),
         w_in[:, o + MLA_ROPE:]], axis=1).astype(BF16)
    wq = w_q_b.reshape(q_lora, heads, MLA_NOPE + MLA_ROPE)
    zq = jnp.zeros((q_lora, heads, tail), F32)
    wqa = jnp.concatenate([wq, zq], axis=2).reshape(q_lora, heads * HEAD_LANES).astype(BF16)
    wq_rope = wq[:, :, MLA_NOPE:].reshape(q_lora * heads, MLA_ROPE)
    wqb = jnp.concatenate([jnp.zeros((q_lora, heads, MLA_NOPE), F32),
                           _rope_swap(wq_rope).reshape(q_lora, heads, MLA_ROPE), zq],
                          axis=2).reshape(q_lora, heads * HEAD_LANES).astype(BF16)
    wkv = w_kv_b.reshape(kv_lora, heads, MLA_NOPE + MLA_V)
    zk = jnp.zeros((kv_lora, heads, HEAD_LANES - MLA_NOPE), F32)
    wk = jnp.concatenate([wkv[:, :, :MLA_NOPE], zk], axis=2).reshape(kv_lora, heads * HEAD_LANES).astype(BF16)
    zv = jnp.zeros((kv_lora, heads, HEAD_LANES - MLA_V), F32)
    wv = jnp.concatenate([wkv[:, :, MLA_NOPE:], zv], axis=2).reshape(kv_lora, heads * HEAD_LANES).astype(BF16)
    wo = w_out[:heads * MLA_V].reshape(heads, MLA_V, d)
    wo_a = jnp.concatenate([wo, jnp.zeros((heads, HEAD_LANES - MLA_V, d), F32)],
                           axis=1).reshape(heads * HEAD_LANES, d).astype(BF16)
    wo_h = w_out[heads * MLA_V:].astype(BF16)
    return dict(win=win, wqa=wqa, wqb=wqb, wk=wk, wv=wv, wo_a=wo_a, wo_h=wo_h, heads=heads)


def _rope_tables(n):
    rows = n // GRID_W
    row = jnp.repeat(jnp.arange(rows), GRID_W).astype(F32)
    col = jnp.tile(jnp.arange(GRID_W), rows).astype(F32)
    inv = ROPE_BASE ** (-jnp.arange(ROPE_NF, dtype=F32) / ROPE_NF)
    ar = row[:, None] * inv
    ac = col[:, None] * inv
    ones = jnp.ones((n, MLA_NOPE), F32)
    tail = HEAD_LANES - MLA_NOPE - MLA_ROPE
    cos = jnp.concatenate([ones, jnp.cos(ar), jnp.cos(ar), jnp.cos(ac), jnp.cos(ac),
                           jnp.ones((n, tail), F32)], axis=1)
    sin = jnp.concatenate([0 * ones, jnp.sin(ar), jnp.sin(ar), jnp.sin(ac), jnp.sin(ac),
                           jnp.zeros((n, tail), F32)], axis=1)
    return cos, sin


def _ab_project(x, mods, row_fn, norm_g, w, q_norm_g, kv_norm_g, rope_tabs, hy_cols):
    b, n, d = x.shape
    tm = min(512, n)
    heads = w["heads"]
    q_lora = q_norm_g.shape[-1]
    kv_lora = kv_norm_g.shape[-1]
    hw = heads * HEAD_LANES
    rope = rope_tabs is not None
    tok = lambda width: pl.BlockSpec((None, tm, width), lambda bi, i: (bi, i, 0))
    in_specs = [tok(d), _mod_spec(row_fn, 0, d), _mod_spec(row_fn, 1, d), _resident((1, d)),
                _resident(w["win"].shape), _resident((1, q_lora)), _resident(w["wqa"].shape)]
    args = [x, mods, mods, norm_g.reshape(1, d), w["win"], q_norm_g.reshape(1, q_lora), w["wqa"]]
    if rope:
        in_specs.append(_resident(w["wqb"].shape))
        args.append(w["wqb"])
    in_specs += [_resident((1, kv_lora)), _resident(w["wk"].shape), _resident(w["wv"].shape)]
    args += [kv_norm_g.reshape(1, kv_lora), w["wk"], w["wv"]]
    if rope:
        in_specs += [pl.BlockSpec((tm, HEAD_LANES), lambda bi, i: (i, 0))] * 2
        args += list(rope_tabs)
    kern = functools.partial(_ab_proj_kernel, rope=rope, q_lora=q_lora, kv_lora=kv_lora, heads=heads,
                             scale=1.0 / math.sqrt(MLA_NOPE + MLA_ROPE))
    return pl.pallas_call(
        kern,
        grid=(b, n // tm),
        in_specs=in_specs,
        out_specs=[tok(hw), tok(hw), tok(hw), tok(hy_cols)],
        out_shape=[jax.ShapeDtypeStruct((b, n, hw), BF16)] * 3 + [jax.ShapeDtypeStruct((b, n, hy_cols), F32)],
        compiler_params=_cparams("parallel", "parallel"),
        name="ab_project",
    )(*args)


def _attn_kernel(q_ref, k_ref, v_ref, o_ref):
    s = _dot(q_ref[...], k_ref[...], _NT)
    p = jnp.exp(s - jnp.max(s, axis=-1, keepdims=True))
    l = jnp.sum(p, axis=-1, keepdims=True)
    o_ref[...] = (_dot(p.astype(BF16), v_ref[...]) / l).astype(o_ref.dtype)


def _attention(q, k, v, heads):
    b, n, hw = q.shape
    s = k.shape[1]
    tq = min(512, n)
    return pl.pallas_call(
        _attn_kernel,
        grid=(b, heads, n // tq),
        in_specs=[pl.BlockSpec((None, tq, HEAD_LANES), lambda bi, h, i: (bi, i, h)),
                  pl.BlockSpec((None, s, HEAD_LANES), lambda bi, h, i: (bi, 0, h)),
                  pl.BlockSpec((None, s, HEAD_LANES), lambda bi, h, i: (bi, 0, h))],
        out_specs=pl.BlockSpec((None, tq, HEAD_LANES), lambda bi, h, i: (bi, i, h)),
        out_shape=jax.ShapeDtypeStruct((b, n, hw), BF16),
        compiler_params=_cparams("parallel", "parallel", "parallel"),
        name="attention",
    )(q, k, v)


def _shift_rows(a, direction):
    n = a.shape[0]
    row = lax.broadcasted_iota(jnp.int32, a.shape, 0)
    if direction > 0:
        return jnp.where(row == 0, 0.0, pltpu.roll(a, 1, axis=0))
    return jnp.where(row == n - 1, 0.0, pltpu.roll(a, n - 1, axis=0))


def _dwconv_kernel(x_ref, w_ref, b_ref, o_ref, ob_ref):
    x = x_ref[...]
    w = w_ref[...]
    y = _shift_rows(x, 1) * w[0:1] + x * w[1:2] + _shift_rows(x, -1) * w[2:3] + b_ref[...]
    o_ref[...] = y
    ob_ref[...] = y.astype(BF16)


def _hy_dwconv(hy, conv_w, conv_b):
    b, n, c3 = hy.shape
    tc = 256 if c3 % 256 == 0 else 128
    assert c3 % tc == 0
    blk = pl.BlockSpec((None, n, tc), lambda bi, j: (bi, 0, j))
    return pl.pallas_call(
        _dwconv_kernel,
        grid=(b, c3 // tc),
        in_specs=[blk, pl.BlockSpec((3, tc), lambda bi, j: (0, j)), pl.BlockSpec((1, tc), lambda bi, j: (0, j))],
        out_specs=[blk, blk],
        out_shape=[jax.ShapeDtypeStruct((b, n, c3), F32), jax.ShapeDtypeStruct((b, n, c3), BF16)],
        compiler_params=_cparams("parallel", "parallel"),
        name="hy_dwconv",
    )(hy, conv_w, conv_b.reshape(1, c3))


def _hy_filter_kernel(z_ref, t_ref, w1_ref, b1_ref, w2_ref, b2_ref, w3_ref, dl_ref, o_ref, *, ch):
    hdot = lambda a, w: jnp.dot(a, w, precision=HIGHEST, preferred_element_type=F32)
    a = jnp.sin(HY_SIN_FREQ * (hdot(z_ref[...], w1_ref[...]) + b1_ref[...]))
    a = jnp.sin(HY_SIN_FREQ * (hdot(a, w2_ref[...]) + b2_ref[...]))
    f = hdot(a, w3_ref[...])
    dec = jnp.exp(-t_ref[...] * dl_ref[...])
    tl = z_ref.shape[0]
    row = pl.program_id(0) * tl + lax.broadcasted_iota(jnp.int32, (tl, ch), 0)
    for o in range(HY_ORDER):
        hf = f[:, (2 * o) * ch:(2 * o + 1) * ch] * dec
        hb = jnp.where(row > 0, f[:, (2 * o + 1) * ch:(2 * o + 2) * ch] * dec, 0.0)
        o_ref[:, (2 * o) * ch:(2 * o + 1) * ch] = (hf + hb).astype(BF16)
        o_ref[:, (2 * o + 1) * ch:(2 * o + 2) * ch] = (hf - hb).astype(BF16)


def _hy_filter_inputs(length, w1, b1, w2, b2, w3, ch):
    t = jnp.linspace(0.0, 1.0, length, dtype=F32)[:, None]
    wv = 2.0 * math.pi * jnp.arange(length, dtype=F32)[:, None] / length
    f = jnp.linspace(1e-4, HY_BANDS - 1, HY_BANDS, dtype=F32)
    z = jnp.concatenate([t, jnp.cos(f * wv), -jnp.sin(f * wv)], axis=-1)
    emb = z.shape[1]
    z = jnp.pad(z, ((0, 0), (0, HEAD_LANES - emb)))
    w1p = jnp.pad(w1, ((0, HEAD_LANES - emb), (0, 0)))
    deltas = jnp.abs(jnp.linspace(HY_MIN_DECAY, HY_MAX_DECAY, ch, dtype=F32))[None, :]
    fw = w1.shape[1]
    tl = min(512, length)
    return pl.pallas_call(
        functools.partial(_hy_filter_kernel, ch=ch),
        grid=(length // tl,),
        in_specs=[pl.BlockSpec((tl, HEAD_LANES), lambda i: (i, 0)), pl.BlockSpec((tl, 1), lambda i: (i, 0)),
                  _resident((HEAD_LANES, fw)), _resident((1, fw)), _resident((fw, fw)), _resident((1, fw)),
                  _resident(w3.shape), _resident((1, ch))],
        out_specs=pl.BlockSpec((tl, 2 * HY_ORDER * ch), lambda i: (i, 0)),
        out_shape=jax.ShapeDtypeStruct((length, 2 * HY_ORDER * ch), BF16),
        compiler_params=_cparams("parallel"),
        name="hy_filter",
    )(z, t, w1p, b1.reshape(1, fw), w2, b2.reshape(1, fw), w3, deltas)


def _dft_tables(length):
    n2 = 2 * length
    k = jnp.arange(length, dtype=jnp.int32)
    idx = (k[:, None] * k[None, :]) % n2
    ang = idx.astype(F32) * (2.0 * math.pi / n2)
    fc = jnp.cos(ang)
    nyq = jnp.where(k % 2 == 0, 1.0, -1.0).astype(F32)
    msin = -jnp.sin(ang)
    fs = jnp.where(k[:, None] == 0, nyq[None, :], msin)
    fst = jnp.where(k[None, :] == 0, nyq[:, None], msin)
    return fc.astype(BF16), fs.astype(BF16), fst.astype(BF16)


def _filter_dft_kernel(hs_ref, hd_ref, fc_ref, fs_ref, o_ref):
    hs = hs_ref[...]
    o_ref[0] = _dot(fc_ref[...], hs)
    ki = _dot(fs_ref[...], hd_ref[...])
    nyq = _dot(fs_ref[0:8, :], hs)[0:1]
    tf = ki.shape[0]
    row = pl.program_id(1) * tf + lax.broadcasted_iota(jnp.int32, (tf, 1), 0)
    o_ref[1] = jnp.where(row == 0, nyq, ki)


def _dft_spec_kernel(x_ref, fc_ref, fs_ref, k_ref, o_ref, *, inv_n):
    x = x_ref[...]
    ur = _dot(fc_ref[...], x)
    ui = _dot(fs_ref[...], x)
    kr = k_ref[0]
    ki = k_ref[1]
    tf = ur.shape[0]
    row = pl.program_id(1) * tf + lax.broadcasted_iota(jnp.int32, (tf, 1), 0)
    first = row == 0
    uiki = ui * ki
    yr = ur * kr - jnp.where(first, 0.0, uiki)
    yi = jnp.where(first, uiki, ur * ki + ui * kr)
    scale = jnp.where(first, inv_n, 2.0 * inv_n)
    o_ref[0] = (yr * scale).astype(BF16)
    o_ref[1] = (yi * scale).astype(BF16)


def _filter_spectrum(p, fc, fs, ch):
    length, cols = p.shape
    tf = min(512, length)
    return pl.pallas_call(
        _filter_dft_kernel,
        grid=(HY_ORDER, length // tf),
        in_specs=[pl.BlockSpec((length, ch), lambda o, i: (0, 2 * o)),
                  pl.BlockSpec((length, ch), lambda o, i: (0, 2 * o + 1)),
                  pl.BlockSpec((tf, length), lambda o, i: (i, 0)),
                  pl.BlockSpec((tf, length), lambda o, i: (i, 0))],
        out_specs=pl.BlockSpec((None, 2, tf, ch), lambda o, i: (o, 0, i, 0)),
        out_shape=jax.ShapeDtypeStruct((HY_ORDER, 2, length, ch), F32),
        compiler_params=_cparams("parallel", "arbitrary"),
        name="hy_filter_dft",
    )(p, p, fc, fs)


def _hy_spectral(xb, col, ch, fc, fs, kspec):
    b, length, _ = xb.shape
    tf = min(512, length)
    return pl.pallas_call(
        functools.partial(_dft_spec_kernel, inv_n=1.0 / (2 * length)),
        grid=(b, length // tf),
        in_specs=[pl.BlockSpec((None, length, ch), lambda bi, i: (bi, 0, col)),
                  pl.BlockSpec((tf, length), lambda bi, i: (i, 0)),
                  pl.BlockSpec((tf, length), lambda bi, i: (i, 0)),
                  pl.BlockSpec((2, tf, ch), lambda bi, i: (0, i, 0))],
        out_specs=pl.BlockSpec((None, 2, tf, ch), lambda bi, i: (bi, 0, i, 0)),
        out_shape=jax.ShapeDtypeStruct((b, 2, length, ch), BF16),
        compiler_params=_cparams("parallel", "arbitrary"),
        name="hy_dft_spectral",
    )(xb, fc, fs, kspec)


def _idft_gate_kernel(y_ref, fc_ref, fst_ref, u_ref, xg_ref, d_ref, o_ref, ob_ref):
    conv = _dot(fc_ref[...], y_ref[0]) + _dot(fst_ref[...], y_ref[1])
    out = xg_ref[...] * (conv + u_ref[...] * d_ref[...])
    o_ref[...] = out
    ob_ref[...] = out.astype(BF16)


def _hy_inverse_gate(y, fc, fst, u, u_col, xg, xg_col, dvec, ch):
    b, _, length, _ = y.shape
    tt = min(512, length)
    out = pl.BlockSpec((None, tt, ch), lambda bi, i: (bi, i, 0))
    return pl.pallas_call(
        _idft_gate_kernel,
        grid=(b, length // tt),
        in_specs=[pl.BlockSpec((None, 2, length, ch), lambda bi, i: (bi, 0, 0, 0)),
                  pl.BlockSpec((tt, length), lambda bi, i: (i, 0)),
                  pl.BlockSpec((tt, length), lambda bi, i: (i, 0)),
                  pl.BlockSpec((None, tt, ch), lambda bi, i: (bi, i, u_col)),
                  pl.BlockSpec((None, tt, ch), lambda bi, i: (bi, i, xg_col)),
                  pl.BlockSpec((1, ch), lambda bi, i: (0, 0))],
        out_specs=[out, out],
        out_shape=[jax.ShapeDtypeStruct((b, length, ch), F32), jax.ShapeDtypeStruct((b, length, ch), BF16)],
        compiler_params=_cparams("parallel", "arbitrary"),
        name="hy_idft_gate",
    )(y, fc, fst, u, xg, dvec.reshape(1, ch))


def _hyena(hy, conv_w, conv_b, w1, b1, w2, b2, w3, hy_bias):
    ch = hy_bias.shape[-1]
    length = hy.shape[1]
    hyc, hyb = _hy_dwconv(hy, conv_w, conv_b)
    fc, fs, fst = _dft_tables(length)
    kspec = _filter_spectrum(_hy_filter_inputs(length, w1, b1, w2, b2, w3, ch), fc, fs, ch)
    y = _hy_spectral(hyb, 0, ch, fc, fs, kspec[0])
    y1, y1b = _hy_inverse_gate(y, fc, fst, hyc, 0, hyc, 1, hy_bias[0], ch)
    y = _hy_spectral(y1b, 0, ch, fc, fs, kspec[1])
    y2, _ = _hy_inverse_gate(y, fc, fst, y1, 0, hyc, 2, hy_bias[1], ch)
    return y2


def _ab_out_kernel(o_ref, y_ref, x_ref, g1_ref, woa_ref, woh_ref, out_ref):
    y = _dot(o_ref[...], woa_ref[...]) + _dot(y_ref[...].astype(BF16), woh_ref[...])
    out_ref[...] = x_ref[...] + g1_ref[...] * y


def _ab_out(o, y2, x, mods, row_fn, w):
    b, n, d = x.shape
    tm = min(512, n)
    tok = lambda width: pl.BlockSpec((None, tm, width), lambda bi, i: (bi, i, 0))
    return pl.pallas_call(
        _ab_out_kernel,
        grid=(b, n // tm),
        in_specs=[tok(o.shape[-1]), tok(y2.shape[-1]), tok(d), _mod_spec(row_fn, 2, d),
                  _resident(w["wo_a"].shape), _resident(w["wo_h"].shape)],
        out_specs=tok(d),
        out_shape=jax.ShapeDtypeStruct((b, n, d), F32),
        compiler_params=_cparams("parallel", "parallel"),
        name="ab_out",
    )(o, y2, x, mods, w["wo_a"], w["wo_h"])


HALO = 8


def _ffn_kernel(*refs, hidden, hc, final):
    if final:
        (x_ref, xp_ref, xn_ref, sh_ref, sc_ref, g2_ref, ng_ref, wup_ref, cw_ref, cb_ref, wdn_ref,
         fg_ref, o_ref) = refs
    else:
        (x_ref, xp_ref, xn_ref, sh_ref, sc_ref, g2_ref, ng_ref, wup_ref, cw_ref, cb_ref, wdn_ref,
         o_ref) = refs
    i = pl.program_id(1)
    last = pl.num_programs(1) - 1
    x = x_ref[...]
    tm, d = x.shape
    mod = lambda a: _modulate(a, ng_ref[...], sh_ref[...], sc_ref[...])
    hp = mod(xp_ref[...]) * (i > 0).astype(F32)
    hn = mod(xn_ref[...]) * (i < last).astype(F32)
    he = jnp.concatenate([hp, mod(x), hn], axis=0).astype(BF16)
    ne = tm + 2 * HALO

    def conv(a, col):
        w = cw_ref[:, col:col + hc]
        prev = pltpu.roll(a, 1, axis=0)[HALO:HALO + tm]
        nxt = pltpu.roll(a, ne - 1, axis=0)[HALO:HALO + tm]
        return prev * w[0:1] + a[HALO:HALO + tm] * w[1:2] + nxt * w[2:3] + cb_ref[:, col:col + hc]

    acc = jnp.zeros((tm, d), F32)
    for j in range(hidden // hc):
        gate = conv(_dot(he, wup_ref[:, j * hc:(j + 1) * hc]), j * hc)
        val = conv(_dot(he, wup_ref[:, hidden + j * hc:hidden + (j + 1) * hc]), hidden + j * hc)
        mid = (_silu(gate) * val).astype(BF16)
        acc = acc + _dot(mid, wdn_ref[j * hc:(j + 1) * hc, :])
    out = x + g2_ref[...] * acc
    if final:
        out = _rms(out, fg_ref[...])
    o_ref[...] = out


def _ffn_chunk(hidden):
    for hc in (512, 256, 128):
        if hidden % hc == 0:
            return hc
    raise ValueError(f"FFN hidden width {hidden} is not a multiple of 128")


def _conv_ffn(x, mods, row_fn, norm_g, w_up, conv_w, conv_b, w_down, final_g=None):
    b, n, d = x.shape
    hidden = w_down.shape[0]
    tm = min(512, n)
    nh = n // HALO
    tpb = tm // HALO
    final = final_g is not None
    tok = pl.BlockSpec((None, tm, d), lambda bi, i: (bi, i, 0))
    in_specs = [tok,
                pl.BlockSpec((None, HALO, d), lambda bi, i: (bi, jnp.maximum(i * tpb - 1, 0), 0)),
                pl.BlockSpec((None, HALO, d), lambda bi, i: (bi, jnp.minimum((i + 1) * tpb, nh - 1), 0)),
                _mod_spec(row_fn, 3, d), _mod_spec(row_fn, 4, d), _mod_spec(row_fn, 5, d),
                _resident((1, d)), _resident(w_up.shape), _resident(conv_w.shape),
                _resident((1, 2 * hidden)), _resident(w_down.shape)]
    args = [x, x, x, mods, mods, mods, norm_g.reshape(1, d), w_up, conv_w, conv_b.reshape(1, 2 * hidden), w_down]
    if final:
        in_specs.append(_resident((1, d)))
        args.append(final_g.reshape(1, d))
    return pl.pallas_call(
        functools.partial(_ffn_kernel, hidden=hidden, hc=_ffn_chunk(hidden), final=final),
        grid=(b, n // tm),
        in_specs=in_specs,
        out_specs=tok,
        out_shape=jax.ShapeDtypeStruct((b, n, d), F32),
        compiler_params=_cparams("parallel", "parallel"),
        name="conv_ffn",
    )(*args)


def _hg_proj_kernel(x_ref, sh_ref, sc_ref, g_ref, w_ref, o_ref, *, width):
    h = _modulate(x_ref[...], g_ref[...], sh_ref[...], sc_ref[...]).astype(BF16)
    z = _dot(h, w_ref[...])
    o_ref[:, :width] = _silu(z[:, :width])
    o_ref[:, width:] = z[:, width:]


def _hg_project(x, mods, row_fn, norm_g, w_in):
    b, n, d = x.shape
    cols = w_in.shape[1]
    tm = min(512, n)
    tok = lambda width: pl.BlockSpec((None, tm, width), lambda bi, i: (bi, i, 0))
    return pl.pallas_call(
        functools.partial(_hg_proj_kernel, width=cols // 5),
        grid=(b, n // tm),
        in_specs=[tok(d), _mod_spec(row_fn, 0, d), _mod_spec(row_fn, 1, d), _resident((1, d)),
                  _resident(w_in.shape)],
        out_specs=tok(cols),
        out_shape=jax.ShapeDtypeStruct((b, n, cols), F32),
        compiler_params=_cparams("parallel", "parallel"),
        name="hg_project",
    )(x, mods, mods, norm_g.reshape(1, d), w_in)


def _scan_tables(direction):
    c = HG_CHUNK
    idx = np.arange(c)
    ops = []
    level = np.full((c, c), -1, np.int32)
    for li, m in enumerate(HG_LEVELS):
        op = np.zeros((c, c), np.float32)
        for r in range(c):
            lo = (r // m) * m
            hi = lo + m - 1
            upper = (r % (2 * m)) >= m
            if direction == 0:
                if upper:
                    op[r, lo:r + 1] = 1
                else:
                    op[r, r + 1:hi + 1] = 1
            else:
                if not upper:
                    op[r, r:hi + 1] = 1
                else:
                    op[r, lo:r] = 1
        ops.append(op)
        same = (idx[:, None] // (2 * m)) == (idx[None, :] // (2 * m))
        up = (idx % (2 * m)) >= m
        pair = same & (up[:, None] & ~up[None, :] if direction == 0 else ~up[:, None] & up[None, :])
        level[pair] = li
    level[idx, idx] = len(HG_LEVELS)
    if direction == 0:
        to_q = (idx[None, :] <= idx[:, None])
    else:
        to_q = (idx[None, :] >= idx[:, None])
    ops += [to_q.astype(np.float32), np.ones((8, c), np.float32)]
    em = np.concatenate(ops, axis=0)
    return jnp.asarray(np.concatenate([em, em], axis=1), BF16), jnp.asarray(level)


def _gla_group(chunks, lb, em2, lv, st):
    c = HG_CHUNK
    e_ = HG_EXPAND
    nl = len(HG_LEVELS)
    log_lb = jnp.log(lb)
    log_1m = jnp.log1p(-lb)
    log_f = [jnp.logaddexp(log_lb, log_1m + jax.nn.log_sigmoid(fr)) for _, fr, _ in chunks]
    keys = [(1.0 - lb) * jax.nn.sigmoid(-fr) for _, fr, _ in chunks]
    g = jnp.concatenate(log_f, axis=1)
    hi = g.astype(BF16)
    lo = (g - hi.astype(F32)).astype(BF16)
    e_all = _dot(em2, jnp.concatenate([hi, lo], axis=0))
    outs = []
    for gi, (q, _, val) in enumerate(chunks):
        e = e_all[:, gi * e_:(gi + 1) * e_]
        key = keys[gi]
        att = jnp.zeros((c, c), F32)
        for li in range(nl):
            dec = jnp.exp(e[li * c:(li + 1) * c])
            att = jnp.where(lv == li, _dot((q * dec).astype(BF16), (key * dec).astype(BF16), _NT), att)
        att = jnp.where(lv == nl, jnp.sum(q * key, axis=1, keepdims=True), att)
        vb = val.astype(BF16)
        to_q = e[nl * c:(nl + 1) * c]
        total = e[(nl + 1) * c:(nl + 1) * c + 1]
        q_in = (q * jnp.exp(to_q)).astype(BF16)
        outs.append(_dot(att.astype(BF16), vb) + _dot(q_in, st.astype(BF16), _NT))
        k_out = (key * jnp.exp(total - to_q)).astype(BF16)
        st = st * jnp.exp(total) + _dot(vb, k_out, _TN)
    return outs, st


def _hg_scan_kernel(*refs, layer, need_ctx):
    (lg_ref, emf_ref, emb_ref, lvf_ref, lvb_ref,
     qc_ref, ffc_ref, fbc_ref, ic_ref, q_ref, ff_ref, fb_ref, i_ref) = refs[:13]
    if need_ctx:
        o_ref, oc_ref, st_ref = refs[13:]
    else:
        o_ref, st_ref = refs[13:]
        oc_ref = None
    c = HG_CHUNK
    lg = lg_ref[...]
    ex = jnp.exp(lg - jnp.max(lg, axis=0, keepdims=True))
    probs = ex / jnp.sum(ex, axis=0, keepdims=True)
    acc = probs[0]
    for l in range(1, layer + 1):
        acc = acc + probs[l]
    lb_all = acc - probs[0]
    lb_f = lb_all[0:1]
    lb_b = lb_all[1:2]

    st_ref[...] = jnp.zeros_like(st_ref)
    o_ref[...] = jnp.zeros_like(o_ref)
    if need_ctx:
        oc_ref[...] = jnp.zeros_like(oc_ref)

    def run(qr, ffr, fbr, ir, outr):
        nc = qr.shape[0] // c
        ng = nc // HG_GROUP
        assert ng * HG_GROUP == nc

        def body(gi, carry):
            rows_f = [pl.multiple_of((gi * HG_GROUP + k) * c, c) for k in range(HG_GROUP)]
            rows_b = [pl.multiple_of((nc - 1 - gi * HG_GROUP - k) * c, c) for k in range(HG_GROUP)]
            load = lambda gate_ref, r: (qr[pl.ds(r, c), :], gate_ref[pl.ds(r, c), :], ir[pl.ds(r, c), :])
            of, stf = _gla_group([load(ffr, r) for r in rows_f], lb_f, emf_ref[...], lvf_ref[...], st_ref[0])
            st_ref[0] = stf
            ob, stb = _gla_group([load(fbr, r) for r in rows_b], lb_b, emb_ref[...], lvb_ref[...], st_ref[1])
            st_ref[1] = stb
            if outr is not None:
                for r, o in zip(rows_f + rows_b, of + ob):
                    outr[pl.ds(r, c), :] += o
            return carry

        lax.fori_loop(0, ng, body, 0, unroll=min(HG_UNROLL, ng))

    run(qc_ref, ffc_ref, fbc_ref, ic_ref, oc_ref)
    run(q_ref, ff_ref, fb_ref, i_ref, o_ref)


def _hg_scan(z, zc, lb_logits, layer, need_ctx):
    b, n, cols = z.shape
    nc_ = zc.shape[1]
    width = cols // 5
    heads = width // HG_EXPAND
    e = HG_EXPAND
    depth = lb_logits.shape[0]
    emf, lvf = _scan_tables(0)
    emb, lvb = _scan_tables(1)
    col = lambda rows, group: pl.BlockSpec((None, rows, e), lambda bi, h: (bi, 0, group * heads + h))
    in_specs = [pl.BlockSpec((depth, 2, e), lambda bi, h: (0, 0, h)),
                _resident(emf.shape), _resident(emb.shape), _resident(lvf.shape), _resident(lvb.shape),
                col(nc_, 0), col(nc_, 1), col(nc_, 2), col(nc_, 3),
                col(n, 0), col(n, 1), col(n, 2), col(n, 3)]
    out_specs = [pl.BlockSpec((None, n, e), lambda bi, h: (bi, 0, h))]
    out_shape = [jax.ShapeDtypeStruct((b, n, width), F32)]
    if need_ctx:
        out_specs.append(pl.BlockSpec((None, nc_, e), lambda bi, h: (bi, 0, h)))
        out_shape.append(jax.ShapeDtypeStruct((b, nc_, width), F32))
    res = pl.pallas_call(
        functools.partial(_hg_scan_kernel, layer=layer, need_ctx=need_ctx),
        grid=(b, heads),
        in_specs=in_specs,
        out_specs=out_specs,
        out_shape=out_shape,
        scratch_shapes=[pltpu.VMEM((2, e, e), F32)],
        compiler_params=_cparams("parallel", "parallel"),
        name="hg_scan",
    )(lb_logits, emf, emb, lvf, lvb, zc, zc, zc, zc, z, z, z, z)
    return (res[0], res[1]) if need_ctx else (res[0], None)


def _hg_out_kernel(o_ref, g_ref, x_ref, g1_ref, ng_ref, w_ref, out_ref, *, heads):
    ng = ng_ref[...]
    parts = [_rms(o_ref[:, h * HG_EXPAND:(h + 1) * HG_EXPAND], ng) for h in range(heads)]
    o = jnp.concatenate(parts, axis=1) * _silu(g_ref[...])
    out_ref[...] = x_ref[...] + g1_ref[...] * _dot(o.astype(BF16), w_ref[...])


def _hg_out(o, z, x, mods, row_fn, norm_g, w_out):
    b, n, d = x.shape
    width = o.shape[-1]
    tm = min(512, n)
    tok = lambda wd: pl.BlockSpec((None, tm, wd), lambda bi, i: (bi, i, 0))
    return pl.pallas_call(
        functools.partial(_hg_out_kernel, heads=width // HG_EXPAND),
        grid=(b, n // tm),
        in_specs=[tok(width), pl.BlockSpec((None, tm, width), lambda bi, i: (bi, i, 4)), tok(d),
                  _mod_spec(row_fn, 2, d), _resident((1, HG_EXPAND)), _resident(w_out.shape)],
        out_specs=tok(d),
        out_shape=jax.ShapeDtypeStruct((b, n, d), F32),
        compiler_params=_cparams("parallel", "parallel"),
        name="hg_out",
    )(o, z, x, mods, norm_g.reshape(1, HG_EXPAND), w_out)


def kernel(x, c, ctx, c_ctx, mod_w, mod_b, norm1_g, norm2_g, ffn_w_up, ffn_conv_w, ffn_conv_b, ffn_w_down, ab_w_in, mla_q_norm_g, mla_w_q_b, mla_kv_norm_g, mla_w_kv_b, hy_conv_w, hy_conv_b, hy_w1, hy_b1, hy_w2, hy_b2, hy_w3, hy_bias, ab_w_out, hg_w_in, hg_lb_logits, hg_norm_g, hg_w_out, final_norm_g):
    batch, n, d = x.shape
    depth = mod_w.shape[0]
    assert batch < MOD_ROWS
    ctx_row = batch
    cc = jnp.zeros((MOD_ROWS, d), F32).at[:batch].set(c).at[ctx_row].set(c_ctx)
    mods_all = _mod_vectors(cc, mod_w, mod_b)
    lat_row = lambda bi, *_: bi
    ctx_row_fn = lambda bi, *_: ctx_row
    rope_tabs = _rope_tables(n)

    xc = ctx
    for layer in range(depth):
        need_ctx = layer < depth - 1
        j = layer // 2
        mods = mods_all[layer]
        if layer % 2 == 0:
            q_lora = mla_q_norm_g.shape[-1]
            kv_lora = mla_kv_norm_g.shape[-1]
            hy_ch = hy_bias.shape[-1]
            w = _ab_weights(ab_w_in[j], mla_w_q_b[j], mla_w_kv_b[j], ab_w_out[j], q_lora, kv_lora, hy_ch)
            hy_cols = (HY_ORDER + 1) * hy_ch
            hy_args = (hy_conv_w[j], hy_conv_b[j], hy_w1[j], hy_b1[j], hy_w2[j], hy_b2[j], hy_w3[j], hy_bias[j])
            q_c, k_c, v_c, hyin_c = _ab_project(xc, mods, ctx_row_fn, norm1_g[layer], w, mla_q_norm_g[j],
                                                mla_kv_norm_g[j], None, hy_cols)
            q, k, v, hyin = _ab_project(x, mods, lat_row, norm1_g[layer], w, mla_q_norm_g[j],
                                        mla_kv_norm_g[j], rope_tabs, hy_cols)
            o = _attention(q, jnp.concatenate([k_c, k], axis=1), jnp.concatenate([v_c, v], axis=1), w["heads"])
            x_new = _ab_out(o, _hyena(hyin, *hy_args), x, mods, lat_row, w)
            if need_ctx:
                o_c = _attention(q_c, k_c, v_c, w["heads"])
                xc = _ab_out(o_c, _hyena(hyin_c, *hy_args), xc, mods, ctx_row_fn, w)
            x = x_new
        else:
            w_in = hg_w_in[j].astype(BF16)
            z_c = _hg_project(xc, mods, ctx_row_fn, norm1_g[layer], w_in)
            z = _hg_project(x, mods, lat_row, norm1_g[layer], w_in)
            o, o_c = _hg_scan(z, z_c, hg_lb_logits, layer, need_ctx)
            w_out = hg_w_out[j].astype(BF16)
            x = _hg_out(o, z, x, mods, lat_row, hg_norm_g[j], w_out)
            if need_ctx:
                xc = _hg_out(o_c, z_c, xc, mods, ctx_row_fn, hg_norm_g[j], w_out)
        w_up = ffn_w_up[layer].astype(BF16)
        w_dn = ffn_w_down[layer].astype(BF16)
        last = layer == depth - 1
        x = _conv_ffn(x, mods, lat_row, norm2_g[layer], w_up, ffn_conv_w[layer], ffn_conv_b[layer], w_dn,
                      final_norm_g if last else None)
        if need_ctx:
            xc = _conv_ffn(xc, mods, ctx_row_fn, norm2_g[layer], w_up, ffn_conv_w[layer], ffn_conv_b[layer], w_dn)
    return x
```

```python
import functools
import math

import numpy as np
import jax
import jax.numpy as jnp
from jax import lax
from jax.experimental import pallas as pl
from jax.experimental.pallas import tpu as pltpu

F32 = jnp.float32
BF16 = jnp.bfloat16
HIGHEST = lax.Precision.HIGHEST

EPS = 1e-6
GRID_W = 64

MLA_NOPE = 64
MLA_ROPE = 32
MLA_V = 64
ROPE_NF = MLA_ROPE // 4
ROPE_BASE = 10000.0
HEAD_LANES = 128
ATTN_TQ = 1024
ATTN_SUB = 256

HY_ORDER = 2
HY_BANDS = 16
HY_SIN_FREQ = 1.0
HY_DECAY_TARGET = 1e-2
HY_FAST_DECAY = 0.3
HY_SLOW_DECAY = 1.5
HY_MIN_DECAY = math.log(HY_DECAY_TARGET) / HY_SLOW_DECAY
HY_MAX_DECAY = math.log(HY_DECAY_TARGET) / HY_FAST_DECAY

HG_EXPAND = 128
HG_CHUNK = 64
HG_LEVELS = (32, 16, 8, 4, 2, 1)
HG_GROUP = 2
HG_UNROLL = 4

MOD_ROWS = 16
VMEM_LIMIT = 56 * 1024 * 1024

_NT = (((1,), (1,)), ((), ()))
_TN = (((0,), (0,)), ((), ()))


def _cparams(*sem):
    return pltpu.CompilerParams(dimension_semantics=sem, vmem_limit_bytes=VMEM_LIMIT)


def _dot(a, b, dims=(((1,), (0,)), ((), ()))):
    return lax.dot_general(a, b, dims, precision=lax.Precision.DEFAULT, preferred_element_type=F32)


def _rms(x, g):
    return x * lax.rsqrt(jnp.mean(x * x, axis=-1, keepdims=True) + EPS) * g


def _modulate(x, g, shift, scale):
    return _rms(x, g) * (1.0 + scale) + shift


def _silu(x):
    return x * jax.nn.sigmoid(x)


def _mod_spec(row_fn, k, d):
    return pl.BlockSpec((None, None, 1, d), lambda *idx: (row_fn(*idx), k, 0, 0))


def _resident(shape):
    nd = len(shape)
    return pl.BlockSpec(shape, lambda *idx: (0,) * nd, pipeline_mode=pl.Buffered(1))


def _mod_kernel(c_ref, w_ref, b_ref, o_ref):
    a = _silu(c_ref[...])
    o_ref[...] = jnp.dot(a, w_ref[...], precision=HIGHEST, preferred_element_type=F32) + b_ref[...]


def _mod_vectors(cc, mod_w, mod_b):
    depth, d, n6 = mod_w.shape
    tn = 512
    out = pl.pallas_call(
        _mod_kernel,
        grid=(depth, n6 // tn),
        in_specs=[pl.BlockSpec((MOD_ROWS, d), lambda l, j: (0, 0)),
                  pl.BlockSpec((None, d, tn), lambda l, j: (l, 0, j)),
                  pl.BlockSpec((None, 1, tn), lambda l, j: (l, 0, j))],
        out_specs=pl.BlockSpec((None, MOD_ROWS, tn), lambda l, j: (l, 0, j)),
        out_shape=jax.ShapeDtypeStruct((depth, MOD_ROWS, n6), F32),
        compiler_params=_cparams("parallel", "parallel"),
        name="mod_vectors",
    )(cc, mod_w, mod_b.reshape(depth, 1, n6))
    return out.reshape(depth, MOD_ROWS, 6, 1, d)


def _ab_proj_kernel(*refs, rope, q_lora, kv_lora, heads, scale):
    if rope:
        (x_ref, sh_ref, sc_ref, g_ref, win_ref, qg_ref, wqa_ref, wqb_ref, kvg_ref, wk_ref, wv_ref,
         cos_ref, sin_ref, q_ref, k_ref, v_ref, hy_ref) = refs
    else:
        (x_ref, sh_ref, sc_ref, g_ref, win_ref, qg_ref, wqa_ref, kvg_ref, wk_ref, wv_ref,
         q_ref, k_ref, v_ref, hy_ref) = refs
    h = _modulate(x_ref[...], g_ref[...], sh_ref[...], sc_ref[...]).astype(BF16)
    z = _dot(h, win_ref[...])
    o = q_lora + kv_lora
    hy_ref[...] = z[:, o + 2 * HEAD_LANES:]
    qn = _rms(z[:, :q_lora], qg_ref[...]).astype(BF16)
    kvn = _rms(z[:, q_lora:o], kvg_ref[...]).astype(BF16)
    v_ref[...] = _dot(kvn, wv_ref[...]).astype(BF16)
    qa = _dot(qn, wqa_ref[...])
    kk = _dot(kvn, wk_ref[...])
    kr = z[:, o:o + HEAD_LANES]
    if rope:
        cos = cos_ref[...]
        sin = sin_ref[...]
        qb = _dot(qn, wqb_ref[...])
        kr = kr * cos + z[:, o + HEAD_LANES:o + 2 * HEAD_LANES] * sin
    for hh in range(heads):
        sl = slice(hh * HEAD_LANES, (hh + 1) * HEAD_LANES)
        qh = qa[:, sl]
        if rope:
            qh = qh * cos + qb[:, sl] * sin
        q_ref[:, sl] = (qh * scale).astype(BF16)
        k_ref[:, sl] = (kk[:, sl] + kr).astype(BF16)


def _rope_swap(w):
    nf = ROPE_NF
    return jnp.concatenate([-w[:, nf:2 * nf], w[:, :nf], -w[:, 3 * nf:4 * nf], w[:, 2 * nf:3 * nf]], axis=1)


def _ab_weights(w_in, w_q_b, w_kv_b, w_out, q_lora, kv_lora, hy_ch):
    d = w_in.shape[0]
    heads = w_q_b.shape[1] // (MLA_NOPE + MLA_ROPE)
    o = q_lora + kv_lora
    w_kr = w_in[:, o:o + MLA_ROPE]
    zpad = lambda n: jnp.zeros((d, n), F32)
    tail = HEAD_LANES - MLA_NOPE - MLA_ROPE
    win = jnp.concatenate(
        [w_in[:, :o],
         zpad(MLA_NOPE), w_kr, zpad(tail),
         zpad(MLA_NOPE), _rope_swap(w_kr), zpad(tail),
         w_in[:, o + MLA_ROPE:]], axis=1).astype(BF16)
    wq = w_q_b.reshape(q_lora, heads, MLA_NOPE + MLA_ROPE)
    zq = jnp.zeros((q_lora, heads, tail), F32)
    wqa = jnp.concatenate([wq, zq], axis=2).reshape(q_lora, heads * HEAD_LANES).astype(BF16)
    wq_rope = wq[:, :, MLA_NOPE:].reshape(q_lora * heads, MLA_ROPE)
    wqb = jnp.concatenate([jnp.zeros((q_lora, heads, MLA_NOPE), F32),
                           _rope_swap(wq_rope).reshape(q_lora, heads, MLA_ROPE), zq],
                          axis=2).reshape(q_lora, heads * HEAD_LANES).astype(BF16)
    wkv = w_kv_b.reshape(kv_lora, heads, MLA_NOPE + MLA_V)
    zk = jnp.zeros((kv_lora, heads, HEAD_LANES - MLA_NOPE), F32)
    wk = jnp.concatenate([wkv[:, :, :MLA_NOPE], zk], axis=2).reshape(kv_lora, heads * HEAD_LANES).astype(BF16)
    zv = jnp.zeros((kv_lora, heads, HEAD_LANES - MLA_V), F32)
    wv = jnp.concatenate([wkv[:, :, MLA_NOPE:], zv], axis=2).reshape(kv_lora, heads * HEAD_LANES).astype(BF16)
    wo = w_out[:heads * MLA_V].reshape(heads, MLA_V, d)
    wo_a = jnp.concatenate([wo, jnp.zeros((heads, HEAD_LANES - MLA_V, d), F32)],
                           axis=1).reshape(heads * HEAD_LANES, d).astype(BF16)
    wo_h = w_out[heads * MLA_V:].astype(BF16)
    return dict(win=win, wqa=wqa, wqb=wqb, wk=wk, wv=wv, wo_a=wo_a, wo_h=wo_h, heads=heads)


def _rope_tables(n):
    rows = n // GRID_W
    row = jnp.repeat(jnp.arange(rows), GRID_W).astype(F32)
    col = jnp.tile(jnp.arange(GRID_W), rows).astype(F32)
    inv = ROPE_BASE ** (-jnp.arange(ROPE_NF, dtype=F32) / ROPE_NF)
    ar = row[:, None] * inv
    ac = col[:, None] * inv
    ones = jnp.ones((n, MLA_NOPE), F32)
    tail = HEAD_LANES - MLA_NOPE - MLA_ROPE
    cos = jnp.concatenate([ones, jnp.cos(ar), jnp.cos(ar), jnp.cos(ac), jnp.cos(ac),
                           jnp.ones((n, tail), F32)], axis=1)
    sin = jnp.concatenate([0 * ones, jnp.sin(ar), jnp.sin(ar), jnp.sin(ac), jnp.sin(ac),
                           jnp.zeros((n, tail), F32)], axis=1)
    return cos, sin


def _ab_project(x, mods, row_fn, norm_g, w, q_norm_g, kv_norm_g, rope_tabs, hy_cols):
    b, n, d = x.shape
    tm = min(512, n)
    heads = w["heads"]
    q_lora = q_norm_g.shape[-1]
    kv_lora = kv_norm_g.shape[-1]
    hw = heads * HEAD_LANES
    rope = rope_tabs is not None
    tok = lambda width: pl.BlockSpec((None, tm, width), lambda bi, i: (bi, i, 0))
    in_specs = [tok(d), _mod_spec(row_fn, 0, d), _mod_spec(row_fn, 1, d), _resident((1, d)),
                _resident(w["win"].shape), _resident((1, q_lora)), _resident(w["wqa"].shape)]
    args = [x, mods, mods, norm_g.reshape(1, d), w["win"], q_norm_g.reshape(1, q_lora), w["wqa"]]
    if rope:
        in_specs.append(_resident(w["wqb"].shape))
        args.append(w["wqb"])
    in_specs += [_resident((1, kv_lora)), _resident(w["wk"].shape), _resident(w["wv"].shape)]
    args += [kv_norm_g.reshape(1, kv_lora), w["wk"], w["wv"]]
    if rope:
        in_specs += [pl.BlockSpec((tm, HEAD_LANES), lambda bi, i: (i, 0))] * 2
        args += list(rope_tabs)
    kern = functools.partial(_ab_proj_kernel, rope=rope, q_lora=q_lora, kv_lora=kv_lora, heads=heads,
                             scale=math.log2(math.e) / math.sqrt(MLA_NOPE + MLA_ROPE))
    return pl.pallas_call(
        kern,
        grid=(b, n // tm),
        in_specs=in_specs,
        out_specs=[tok(hw), tok(hw), tok(hw), tok(hy_cols)],
        out_shape=[jax.ShapeDtypeStruct((b, n, hw), BF16)] * 3 + [jax.ShapeDtypeStruct((b, n, hy_cols), F32)],
        compiler_params=_cparams("parallel", "parallel"),
        name="ab_project",
    )(*args)


def _attn_kernel(q_ref, k_ref, v_ref, o_ref, *, sub):
    k = k_ref[...]
    v = v_ref[...]
    for r in range(0, q_ref.shape[0], sub):
        s = _dot(q_ref[r:r + sub, :], k, _NT)
        p = jnp.exp2(s - jnp.max(s, axis=-1, keepdims=True))
        l = jnp.sum(p, axis=-1, keepdims=True)
        o_ref[r:r + sub, :] = (_dot(p.astype(BF16), v) / l).astype(o_ref.dtype)


def _attention(q, k, v, heads):
    b, n, hw = q.shape
    s = k.shape[1]
    tq = min(ATTN_TQ, n)
    return pl.pallas_call(
        functools.partial(_attn_kernel, sub=min(ATTN_SUB, tq)),
        grid=(b, heads, n // tq),
        in_specs=[pl.BlockSpec((None, tq, HEAD_LANES), lambda bi, h, i: (bi, i, h)),
                  pl.BlockSpec((None, s, HEAD_LANES), lambda bi, h, i: (bi, 0, h)),
                  pl.BlockSpec((None, s, HEAD_LANES), lambda bi, h, i: (bi, 0, h))],
        out_specs=pl.BlockSpec((None, tq, HEAD_LANES), lambda bi, h, i: (bi, i, h)),
        out_shape=jax.ShapeDtypeStruct((b, n, hw), BF16),
        compiler_params=_cparams("parallel", "parallel", "parallel"),
        name="attention",
    )(q, k, v)


def _shift_rows(a, direction):
    n = a.shape[0]
    row = lax.broadcasted_iota(jnp.int32, a.shape, 0)
    if direction > 0:
        return jnp.where(row == 0, 0.0, pltpu.roll(a, 1, axis=0))
    return jnp.where(row == n - 1, 0.0, pltpu.roll(a, n - 1, axis=0))


def _dwconv_kernel(x_ref, w_ref, b_ref, o_ref, ob_ref):
    x = x_ref[...]
    w = w_ref[...]
    y = _shift_rows(x, 1) * w[0:1] + x * w[1:2] + _shift_rows(x, -1) * w[2:3] + b_ref[...]
    o_ref[...] = y
    ob_ref[...] = y.astype(BF16)


def _hy_dwconv(hy, conv_w, conv_b):
    b, n, c3 = hy.shape
    tc = 256 if c3 % 256 == 0 else 128
    assert c3 % tc == 0
    blk = pl.BlockSpec((None, n, tc), lambda bi, j: (bi, 0, j))
    return pl.pallas_call(
        _dwconv_kernel,
        grid=(b, c3 // tc),
        in_specs=[blk, pl.BlockSpec((3, tc), lambda bi, j: (0, j)), pl.BlockSpec((1, tc), lambda bi, j: (0, j))],
        out_specs=[blk, blk],
        out_shape=[jax.ShapeDtypeStruct((b, n, c3), F32), jax.ShapeDtypeStruct((b, n, c3), BF16)],
        compiler_params=_cparams("parallel", "parallel"),
        name="hy_dwconv",
    )(hy, conv_w, conv_b.reshape(1, c3))


def _hy_filter_kernel(z_ref, t_ref, w1_ref, b1_ref, w2_ref, b2_ref, w3_ref, dl_ref, o_ref, *, ch):
    hdot = lambda a, w: jnp.dot(a, w, precision=HIGHEST, preferred_element_type=F32)
    a = jnp.sin(HY_SIN_FREQ * (hdot(z_ref[...], w1_ref[...]) + b1_ref[...]))
    a = jnp.sin(HY_SIN_FREQ * (hdot(a, w2_ref[...]) + b2_ref[...]))
    f = hdot(a, w3_ref[...])
    dec = jnp.exp(-t_ref[...] * dl_ref[...])
    tl = z_ref.shape[0]
    row = pl.program_id(0) * tl + lax.broadcasted_iota(jnp.int32, (tl, ch), 0)
    for o in range(HY_ORDER):
        hf = f[:, (2 * o) * ch:(2 * o + 1) * ch] * dec
        hb = jnp.where(row > 0, f[:, (2 * o + 1) * ch:(2 * o + 2) * ch] * dec, 0.0)
        o_ref[:, (2 * o) * ch:(2 * o + 1) * ch] = (hf + hb).astype(BF16)
        o_ref[:, (2 * o + 1) * ch:(2 * o + 2) * ch] = (hf - hb).astype(BF16)


def _hy_filter_inputs(length, w1, b1, w2, b2, w3, ch):
    t = jnp.linspace(0.0, 1.0, length, dtype=F32)[:, None]
    wv = 2.0 * math.pi * jnp.arange(length, dtype=F32)[:, None] / length
    f = jnp.linspace(1e-4, HY_BANDS - 1, HY_BANDS, dtype=F32)
    z = jnp.concatenate([t, jnp.cos(f * wv), -jnp.sin(f * wv)], axis=-1)
    emb = z.shape[1]
    z = jnp.pad(z, ((0, 0), (0, HEAD_LANES - emb)))
    w1p = jnp.pad(w1, ((0, HEAD_LANES - emb), (0, 0)))
    deltas = jnp.abs(jnp.linspace(HY_MIN_DECAY, HY_MAX_DECAY, ch, dtype=F32))[None, :]
    fw = w1.shape[1]
    tl = min(512, length)
    return pl.pallas_call(
        functools.partial(_hy_filter_kernel, ch=ch),
        grid=(length // tl,),
        in_specs=[pl.BlockSpec((tl, HEAD_LANES), lambda i: (i, 0)), pl.BlockSpec((tl, 1), lambda i: (i, 0)),
                  _resident((HEAD_LANES, fw)), _resident((1, fw)), _resident((fw, fw)), _resident((1, fw)),
                  _resident(w3.shape), _resident((1, ch))],
        out_specs=pl.BlockSpec((tl, 2 * HY_ORDER * ch), lambda i: (i, 0)),
        out_shape=jax.ShapeDtypeStruct((length, 2 * HY_ORDER * ch), BF16),
        compiler_params=_cparams("parallel"),
        name="hy_filter",
    )(z, t, w1p, b1.reshape(1, fw), w2, b2.reshape(1, fw), w3, deltas)


def _dft_tables(length):
    n2 = 2 * length
    theta = 2.0 * math.pi / n2
    n = jnp.arange(length, dtype=jnp.int32)[None, :]
    a = 8 * jnp.arange(length // 8, dtype=jnp.int32)[:, None]
    b = jnp.arange(8, dtype=jnp.int32)[:, None]
    ang_a = ((a * n) % n2).astype(F32) * theta
    ang_b = ((b * n) % n2).astype(F32) * theta
    ca, sa = jnp.cos(ang_a)[:, None, :], jnp.sin(ang_a)[:, None, :]
    cb, sb = jnp.cos(ang_b)[None, :, :], jnp.sin(ang_b)[None, :, :]
    fc = (ca * cb - sa * sb).reshape(length, length)
    fs = -(sa * cb + ca * sb).reshape(length, length)
    return fc.astype(BF16), fs.astype(BF16)


def _alternating_sum(x):
    col = lax.broadcasted_iota(jnp.int32, (8, x.shape[0]), 1)
    alt = (1 - 2 * (col & 1)).astype(F32).astype(BF16)
    return _dot(alt, x)[0:1]


def _filter_dft_kernel(hs_ref, hd_ref, fc_ref, fs_ref, o_ref):
    hs = hs_ref[...]
    o_ref[0] = _dot(fc_ref[...], hs)
    ki = _dot(fs_ref[...], hd_ref[...])
    nyq = _alternating_sum(hs)
    tf = ki.shape[0]
    row = pl.program_id(1) * tf + lax.broadcasted_iota(jnp.int32, (tf, 1), 0)
    o_ref[1] = jnp.where(row == 0, nyq, ki)


def _dft_spec_kernel(x_ref, fc_ref, fs_ref, k_ref, o_ref, *, inv_n):
    x = x_ref[...]
    ur = _dot(fc_ref[...], x)
    kr = k_ref[0]
    ki = k_ref[1]
    tf = ur.shape[0]
    row = pl.program_id(1) * tf + lax.broadcasted_iota(jnp.int32, (tf, 1), 0)
    first = row == 0
    ui = jnp.where(first, _alternating_sum(x), _dot(fs_ref[...], x))
    uiki = ui * ki
    yr = ur * kr - jnp.where(first, 0.0, uiki)
    yi = jnp.where(first, uiki, ur * ki + ui * kr)
    scale = jnp.where(first, inv_n, 2.0 * inv_n)
    o_ref[0] = (yr * scale).astype(BF16)
    o_ref[1] = (yi * scale).astype(BF16)


def _filter_spectrum(p, fc, fs, ch):
    length, cols = p.shape
    tf = min(512, length)
    return pl.pallas_call(
        _filter_dft_kernel,
        grid=(HY_ORDER, length // tf),
        in_specs=[pl.BlockSpec((length, ch), lambda o, i: (0, 2 * o)),
                  pl.BlockSpec((length, ch), lambda o, i: (0, 2 * o + 1)),
                  pl.BlockSpec((tf, length), lambda o, i: (i, 0)),
                  pl.BlockSpec((tf, length), lambda o, i: (i, 0))],
        out_specs=pl.BlockSpec((None, 2, tf, ch), lambda o, i: (o, 0, i, 0)),
        out_shape=jax.ShapeDtypeStruct((HY_ORDER, 2, length, ch), F32),
        compiler_params=_cparams("parallel", "arbitrary"),
        name="hy_filter_dft",
    )(p, p, fc, fs)


def _hy_spectral(xb, col, ch, fc, fs, kspec):
    b, length, _ = xb.shape
    tf = min(512, length)
    return pl.pallas_call(
        functools.partial(_dft_spec_kernel, inv_n=1.0 / (2 * length)),
        grid=(b, length // tf),
        in_specs=[pl.BlockSpec((None, length, ch), lambda bi, i: (bi, 0, col)),
                  pl.BlockSpec((tf, length), lambda bi, i: (i, 0)),
                  pl.BlockSpec((tf, length), lambda bi, i: (i, 0)),
                  pl.BlockSpec((2, tf, ch), lambda bi, i: (0, i, 0))],
        out_specs=pl.BlockSpec((None, 2, tf, ch), lambda bi, i: (bi, 0, i, 0)),
        out_shape=jax.ShapeDtypeStruct((b, 2, length, ch), BF16),
        compiler_params=_cparams("parallel", "arbitrary"),
        name="hy_dft_spectral",
    )(xb, fc, fs, kspec)


def _idft_gate_kernel(y_ref, fc_ref, fs_ref, u_ref, xg_ref, d_ref, o_ref, ob_ref):
    tt = fc_ref.shape[0]
    t = pl.program_id(1) * tt + lax.broadcasted_iota(jnp.int32, (tt, 1), 0)
    sign = (1 - 2 * (t & 1)).astype(F32)
    conv = (_dot(fc_ref[...], y_ref[0]) + _dot(fs_ref[...], y_ref[1])
            + sign * y_ref[1, 0:1, :].astype(F32))
    out = xg_ref[...] * (conv + u_ref[...] * d_ref[...])
    o_ref[...] = out
    ob_ref[...] = out.astype(BF16)


def _hy_inverse_gate(y, fc, fs, u, u_col, xg, xg_col, dvec, ch):
    b, _, length, _ = y.shape
    tt = min(512, length)
    out = pl.BlockSpec((None, tt, ch), lambda bi, i: (bi, i, 0))
    return pl.pallas_call(
        _idft_gate_kernel,
        grid=(b, length // tt),
        in_specs=[pl.BlockSpec((None, 2, length, ch), lambda bi, i: (bi, 0, 0, 0)),
                  pl.BlockSpec((tt, length), lambda bi, i: (i, 0)),
                  pl.BlockSpec((tt, length), lambda bi, i: (i, 0)),
                  pl.BlockSpec((None, tt, ch), lambda bi, i: (bi, i, u_col)),
                  pl.BlockSpec((None, tt, ch), lambda bi, i: (bi, i, xg_col)),
                  pl.BlockSpec((1, ch), lambda bi, i: (0, 0))],
        out_specs=[out, out],
        out_shape=[jax.ShapeDtypeStruct((b, length, ch), F32), jax.ShapeDtypeStruct((b, length, ch), BF16)],
        compiler_params=_cparams("parallel", "arbitrary"),
        name="hy_idft_gate",
    )(y, fc, fs, u, xg, dvec.reshape(1, ch))


def _hyena(hy, conv_w, conv_b, w1, b1, w2, b2, w3, hy_bias):
    ch = hy_bias.shape[-1]
    length = hy.shape[1]
    hyc, hyb = _hy_dwconv(hy, conv_w, conv_b)
    fc, fs = _dft_tables(length)
    kspec = _filter_spectrum(_hy_filter_inputs(length, w1, b1, w2, b2, w3, ch), fc, fs, ch)
    y = _hy_spectral(hyb, 0, ch, fc, fs, kspec[0])
    y1, y1b = _hy_inverse_gate(y, fc, fs, hyc, 0, hyc, 1, hy_bias[0], ch)
    y = _hy_spectral(y1b, 0, ch, fc, fs, kspec[1])
    y2, _ = _hy_inverse_gate(y, fc, fs, y1, 0, hyc, 2, hy_bias[1], ch)
    return y2


def _ab_out_kernel(o_ref, y_ref, x_ref, g1_ref, woa_ref, woh_ref, out_ref):
    y = _dot(o_ref[...], woa_ref[...]) + _dot(y_ref[...].astype(BF16), woh_ref[...])
    out_ref[...] = x_ref[...] + g1_ref[...] * y


def _ab_out(o, y2, x, mods, row_fn, w):
    b, n, d = x.shape
    tm = min(512, n)
    tok = lambda width: pl.BlockSpec((None, tm, width), lambda bi, i: (bi, i, 0))
    return pl.pallas_call(
        _ab_out_kernel,
        grid=(b, n // tm),
        in_specs=[tok(o.shape[-1]), tok(y2.shape[-1]), tok(d), _mod_spec(row_fn, 2, d),
                  _resident(w["wo_a"].shape), _resident(w["wo_h"].shape)],
        out_specs=tok(d),
        out_shape=jax.ShapeDtypeStruct((b, n, d), F32),
        compiler_params=_cparams("parallel", "parallel"),
        name="ab_out",
    )(o, y2, x, mods, w["wo_a"], w["wo_h"])


HALO = 8


def _ffn_kernel(*refs, hidden, hc, final):
    if final:
        (x_ref, xp_ref, xn_ref, sh_ref, sc_ref, g2_ref, ng_ref, wup_ref, cw_ref, cb_ref, wdn_ref,
         fg_ref, o_ref) = refs
    else:
        (x_ref, xp_ref, xn_ref, sh_ref, sc_ref, g2_ref, ng_ref, wup_ref, cw_ref, cb_ref, wdn_ref,
         o_ref) = refs
    i = pl.program_id(1)
    last = pl.num_programs(1) - 1
    x = x_ref[...]
    tm, d = x.shape
    mod = lambda a: _modulate(a, ng_ref[...], sh_ref[...], sc_ref[...])
    hp = mod(xp_ref[...]) * (i > 0).astype(F32)
    hn = mod(xn_ref[...]) * (i < last).astype(F32)
    he = jnp.concatenate([hp, mod(x), hn], axis=0).astype(BF16)
    ne = tm + 2 * HALO

    def conv(a, col):
        w = cw_ref[:, col:col + hc]
        prev = pltpu.roll(a, 1, axis=0)[HALO:HALO + tm]
        nxt = pltpu.roll(a, ne - 1, axis=0)[HALO:HALO + tm]
        return prev * w[0:1] + a[HALO:HALO + tm] * w[1:2] + nxt * w[2:3] + cb_ref[:, col:col + hc]

    acc = jnp.zeros((tm, d), F32)
    for j in range(hidden // hc):
        gate = conv(_dot(he, wup_ref[:, j * hc:(j + 1) * hc]), j * hc)
        val = conv(_dot(he, wup_ref[:, hidden + j * hc:hidden + (j + 1) * hc]), hidden + j * hc)
        mid = (_silu(gate) * val).astype(BF16)
        acc = acc + _dot(mid, wdn_ref[j * hc:(j + 1) * hc, :])
    out = x + g2_ref[...] * acc
    if final:
        out = _rms(out, fg_ref[...])
    o_ref[...] = out


def _ffn_chunk(hidden):
    for hc in (512, 256, 128):
        if hidden % hc == 0:
            return hc
    raise ValueError(f"FFN hidden width {hidden} is not a multiple of 128")


def _conv_ffn(x, mods, row_fn, norm_g, w_up, conv_w, conv_b, w_down, final_g=None):
    b, n, d = x.shape
    hidden = w_down.shape[0]
    tm = min(512, n)
    nh = n // HALO
    tpb = tm // HALO
    final = final_g is not None
    tok = pl.BlockSpec((None, tm, d), lambda bi, i: (bi, i, 0))
    in_specs = [tok,
                pl.BlockSpec((None, HALO, d), lambda bi, i: (bi, jnp.maximum(i * tpb - 1, 0), 0)),
                pl.BlockSpec((None, HALO, d), lambda bi, i: (bi, jnp.minimum((i + 1) * tpb, nh - 1), 0)),
                _mod_spec(row_fn, 3, d), _mod_spec(row_fn, 4, d), _mod_spec(row_fn, 5, d),
                _resident((1, d)), _resident(w_up.shape), _resident(conv_w.shape),
                _resident((1, 2 * hidden)), _resident(w_down.shape)]
    args = [x, x, x, mods, mods, mods, norm_g.reshape(1, d), w_up, conv_w, conv_b.reshape(1, 2 * hidden), w_down]
    if final:
        in_specs.append(_resident((1, d)))
        args.append(final_g.reshape(1, d))
    return pl.pallas_call(
        functools.partial(_ffn_kernel, hidden=hidden, hc=_ffn_chunk(hidden), final=final),
        grid=(b, n // tm),
        in_specs=in_specs,
        out_specs=tok,
        out_shape=jax.ShapeDtypeStruct((b, n, d), F32),
        compiler_params=_cparams("parallel", "parallel"),
        name="conv_ffn",
    )(*args)


def _hg_proj_kernel(x_ref, sh_ref, sc_ref, g_ref, w_ref, o_ref, *, width):
    h = _modulate(x_ref[...], g_ref[...], sh_ref[...], sc_ref[...]).astype(BF16)
    z = _dot(h, w_ref[...])
    o_ref[:, :width] = _silu(z[:, :width])
    o_ref[:, width:] = z[:, width:]


def _hg_project(x, mods, row_fn, norm_g, w_in):
    b, n, d = x.shape
    cols = w_in.shape[1]
    tm = min(512, n)
    tok = lambda width: pl.BlockSpec((None, tm, width), lambda bi, i: (bi, i, 0))
    return pl.pallas_call(
        functools.partial(_hg_proj_kernel, width=cols // 5),
        grid=(b, n // tm),
        in_specs=[tok(d), _mod_spec(row_fn, 0, d), _mod_spec(row_fn, 1, d), _resident((1, d)),
                  _resident(w_in.shape)],
        out_specs=tok(cols),
        out_shape=jax.ShapeDtypeStruct((b, n, cols), F32),
        compiler_params=_cparams("parallel", "parallel"),
        name="hg_project",
    )(x, mods, mods, norm_g.reshape(1, d), w_in)


def _scan_tables(direction):
    c = HG_CHUNK
    idx = np.arange(c)
    ops = []
    level = np.full((c, c), -1, np.int32)
    for li, m in enumerate(HG_LEVELS):
        op = np.zeros((c, c), np.float32)
        for r in range(c):
            lo = (r // m) * m
            hi = lo + m - 1
            upper = (r % (2 * m)) >= m
            if direction == 0:
                if upper:
                    op[r, lo:r + 1] = 1
                else:
                    op[r, r + 1:hi + 1] = 1
            else:
                if not upper:
                    op[r, r:hi + 1] = 1
                else:
                    op[r, lo:r] = 1
        ops.append(op)
        same = (idx[:, None] // (2 * m)) == (idx[None, :] // (2 * m))
        up = (idx % (2 * m)) >= m
        pair = same & (up[:, None] & ~up[None, :] if direction == 0 else ~up[:, None] & up[None, :])
        level[pair] = li
    level[idx, idx] = len(HG_LEVELS)
    if direction == 0:
        to_q = (idx[None, :] <= idx[:, None])
    else:
        to_q = (idx[None, :] >= idx[:, None])
    ops += [to_q.astype(np.float32), np.ones((8, c), np.float32)]
    em = np.concatenate(ops, axis=0)
    return jnp.asarray(np.concatenate([em, em], axis=1), BF16), jnp.asarray(level)


def _gla_group(chunks, lb, em2, lv, st):
    c = HG_CHUNK
    e_ = HG_EXPAND
    nl = len(HG_LEVELS)
    log_lb = jnp.log(lb)
    log_1m = jnp.log1p(-lb)
    log_f = []
    keys = []
    for _, fr, _ in chunks:
        t = jnp.exp(-jnp.abs(fr))
        d = 1.0 + t
        keys.append((1.0 - lb) * (jnp.where(fr > 0, t, 1.0) / d))
        b = log_1m + (jnp.minimum(fr, 0.0) - jnp.log(d))
        log_f.append(jnp.maximum(log_lb, b) + jnp.log(1.0 + jnp.exp(-jnp.abs(log_lb - b))))
    g = jnp.concatenate(log_f, axis=1)
    hi = g.astype(BF16)
    lo = (g - hi.astype(F32)).astype(BF16)
    e_all = _dot(em2, jnp.concatenate([hi, lo], axis=0))
    outs = []
    for gi, (q, _, val) in enumerate(chunks):
        e = e_all[:, gi * e_:(gi + 1) * e_]
        key = keys[gi]
        att = jnp.zeros((c, c), F32)
        for li in range(nl):
            dec = jnp.exp(e[li * c:(li + 1) * c])
            att = jnp.where(lv == li, _dot((q * dec).astype(BF16), (key * dec).astype(BF16), _NT), att)
        att = jnp.where(lv == nl, jnp.sum(q * key, axis=1, keepdims=True), att)
        vb = val.astype(BF16)
        to_q = e[nl * c:(nl + 1) * c]
        total = e[(nl + 1) * c:(nl + 1) * c + 1]
        q_in = (q * jnp.exp(to_q)).astype(BF16)
        outs.append(_dot(att.astype(BF16), vb) + _dot(q_in, st.astype(BF16), _NT))
        k_out = (key * jnp.exp(total - to_q)).astype(BF16)
        st = st * jnp.exp(total) + _dot(vb, k_out, _TN)
    return outs, st


def _hg_scan_kernel(*refs, layer, need_ctx):
    (lg_ref, emf_ref, emb_ref, lvf_ref, lvb_ref,
     qc_ref, ffc_ref, fbc_ref, ic_ref, q_ref, ff_ref, fb_ref, i_ref) = refs[:13]
    if need_ctx:
        o_ref, oc_ref, st_ref = refs[13:]
    else:
        o_ref, st_ref = refs[13:]
        oc_ref = None
    c = HG_CHUNK
    lg = lg_ref[...]
    ex = jnp.exp(lg - jnp.max(lg, axis=0, keepdims=True))
    probs = ex / jnp.sum(ex, axis=0, keepdims=True)
    acc = probs[0]
    for l in range(1, layer + 1):
        acc = acc + probs[l]
    lb_all = acc - probs[0]
    lb_f = lb_all[0:1]
    lb_b = lb_all[1:2]

    st_ref[...] = jnp.zeros_like(st_ref)
    o_ref[...] = jnp.zeros_like(o_ref)
    if need_ctx:
        oc_ref[...] = jnp.zeros_like(oc_ref)

    def run(qr, ffr, fbr, ir, outr):
        nc = qr.shape[0] // c
        ng = nc // HG_GROUP
        assert ng * HG_GROUP == nc

        def body(gi, carry):
            rows_f = [pl.multiple_of((gi * HG_GROUP + k) * c, c) for k in range(HG_GROUP)]
            rows_b = [pl.multiple_of((nc - 1 - gi * HG_GROUP - k) * c, c) for k in range(HG_GROUP)]
            load = lambda gate_ref, r: (qr[pl.ds(r, c), :], gate_ref[pl.ds(r, c), :], ir[pl.ds(r, c), :])
            of, stf = _gla_group([load(ffr, r) for r in rows_f], lb_f, emf_ref[...], lvf_ref[...], st_ref[0])
            st_ref[0] = stf
            ob, stb = _gla_group([load(fbr, r) for r in rows_b], lb_b, emb_ref[...], lvb_ref[...], st_ref[1])
            st_ref[1] = stb
            if outr is not None:
                for r, o in zip(rows_f + rows_b, of + ob):
                    outr[pl.ds(r, c), :] += o
            return carry

        lax.fori_loop(0, ng, body, 0, unroll=min(HG_UNROLL, ng))

    run(qc_ref, ffc_ref, fbc_ref, ic_ref, oc_ref)
    run(q_ref, ff_ref, fb_ref, i_ref, o_ref)


def _hg_scan(z, zc, lb_logits, layer, need_ctx):
    b, n, cols = z.shape
    nc_ = zc.shape[1]
    width = cols // 5
    heads = width // HG_EXPAND
    e = HG_EXPAND
    depth = lb_logits.shape[0]
    emf, lvf = _scan_tables(0)
    emb, lvb = _scan_tables(1)
    col = lambda rows, group: pl.BlockSpec((None, rows, e), lambda bi, h: (bi, 0, group * heads + h))
    in_specs = [pl.BlockSpec((depth, 2, e), lambda bi, h: (0, 0, h)),
                _resident(emf.shape), _resident(emb.shape), _resident(lvf.shape), _resident(lvb.shape),
                col(nc_, 0), col(nc_, 1), col(nc_, 2), col(nc_, 3),
                col(n, 0), col(n, 1), col(n, 2), col(n, 3)]
    out_specs = [pl.BlockSpec((None, n, e), lambda bi, h: (bi, 0, h))]
    out_shape = [jax.ShapeDtypeStruct((b, n, width), F32)]
    if need_ctx:
        out_specs.append(pl.BlockSpec((None, nc_, e), lambda bi, h: (bi, 0, h)))
        out_shape.append(jax.ShapeDtypeStruct((b, nc_, width), F32))
    res = pl.pallas_call(
        functools.partial(_hg_scan_kernel, layer=layer, need_ctx=need_ctx),
        grid=(b, heads),
        in_specs=in_specs,
        out_specs=out_specs,
        out_shape=out_shape,
        scratch_shapes=[pltpu.VMEM((2, e, e), F32)],
        compiler_params=_cparams("parallel", "parallel"),
        name="hg_scan",
    )(lb_logits, emf, emb, lvf, lvb, zc, zc, zc, zc, z, z, z, z)
    return (res[0], res[1]) if need_ctx else (res[0], None)


def _hg_out_kernel(o_ref, g_ref, x_ref, g1_ref, ng_ref, w_ref, out_ref, *, heads):
    ng = ng_ref[...]
    parts = [_rms(o_ref[:, h * HG_EXPAND:(h + 1) * HG_EXPAND], ng) for h in range(heads)]
    o = jnp.concatenate(parts, axis=1) * _silu(g_ref[...])
    out_ref[...] = x_ref[...] + g1_ref[...] * _dot(o.astype(BF16), w_ref[...])


def _hg_out(o, z, x, mods, row_fn, norm_g, w_out):
    b, n, d = x.shape
    width = o.shape[-1]
    tm = min(512, n)
    tok = lambda wd: pl.BlockSpec((None, tm, wd), lambda bi, i: (bi, i, 0))
    return pl.pallas_call(
        functools.partial(_hg_out_kernel, heads=width // HG_EXPAND),
        grid=(b, n // tm),
        in_specs=[tok(width), pl.BlockSpec((None, tm, width), lambda bi, i: (bi, i, 4)), tok(d),
                  _mod_spec(row_fn, 2, d), _resident((1, HG_EXPAND)), _resident(w_out.shape)],
        out_specs=tok(d),
        out_shape=jax.ShapeDtypeStruct((b, n, d), F32),
        compiler_params=_cparams("parallel", "parallel"),
        name="hg_out",
    )(o, z, x, mods, norm_g.reshape(1, HG_EXPAND), w_out)


def kernel(x, c, ctx, c_ctx, mod_w, mod_b, norm1_g, norm2_g, ffn_w_up, ffn_conv_w, ffn_conv_b, ffn_w_down, ab_w_in, mla_q_norm_g, mla_w_q_b, mla_kv_norm_g, mla_w_kv_b, hy_conv_w, hy_conv_b, hy_w1, hy_b1, hy_w2, hy_b2, hy_w3, hy_bias, ab_w_out, hg_w_in, hg_lb_logits, hg_norm_g, hg_w_out, final_norm_g):
    batch, n, d = x.shape
    depth = mod_w.shape[0]
    assert batch < MOD_ROWS
    ctx_row = batch
    cc = jnp.zeros((MOD_ROWS, d), F32).at[:batch].set(c).at[ctx_row].set(c_ctx)
    mods_all = _mod_vectors(cc, mod_w, mod_b)
    lat_row = lambda bi, *_: bi
    ctx_row_fn = lambda bi, *_: ctx_row
    rope_tabs = _rope_tables(n)

    xc = ctx
    for layer in range(depth):
        need_ctx = layer < depth - 1
        j = layer // 2
        mods = mods_all[layer]
        if layer % 2 == 0:
            q_lora = mla_q_norm_g.shape[-1]
            kv_lora = mla_kv_norm_g.shape[-1]
            hy_ch = hy_bias.shape[-1]
            w = _ab_weights(ab_w_in[j], mla_w_q_b[j], mla_w_kv_b[j], ab_w_out[j], q_lora, kv_lora, hy_ch)
            hy_cols = (HY_ORDER + 1) * hy_ch
            hy_args = (hy_conv_w[j], hy_conv_b[j], hy_w1[j], hy_b1[j], hy_w2[j], hy_b2[j], hy_w3[j], hy_bias[j])
            q_c, k_c, v_c, hyin_c = _ab_project(xc, mods, ctx_row_fn, norm1_g[layer], w, mla_q_norm_g[j],
                                                mla_kv_norm_g[j], None, hy_cols)
            q, k, v, hyin = _ab_project(x, mods, lat_row, norm1_g[layer], w, mla_q_norm_g[j],
                                        mla_kv_norm_g[j], rope_tabs, hy_cols)
            o = _attention(q, jnp.concatenate([k_c, k], axis=1), jnp.concatenate([v_c, v], axis=1), w["heads"])
            x_new = _ab_out(o, _hyena(hyin, *hy_args), x, mods, lat_row, w)
            if need_ctx:
                o_c = _attention(q_c, k_c, v_c, w["heads"])
                xc = _ab_out(o_c, _hyena(hyin_c, *hy_args), xc, mods, ctx_row_fn, w)
            x = x_new
        else:
            w_in = hg_w_in[j].astype(BF16)
            z_c = _hg_project(xc, mods, ctx_row_fn, norm1_g[layer], w_in)
            z = _hg_project(x, mods, lat_row, norm1_g[layer], w_in)
            o, o_c = _hg_scan(z, z_c, hg_lb_logits, layer, need_ctx)
            w_out = hg_w_out[j].astype(BF16)
            x = _hg_out(o, z, x, mods, lat_row, hg_norm_g[j], w_out)
            if need_ctx:
                xc = _hg_out(o_c, z_c, xc, mods, ctx_row_fn, hg_norm_g[j], w_out)
        w_up = ffn_w_up[layer].astype(BF16)
        w_dn = ffn_w_down[layer].astype(BF16)
        last = layer == depth - 1
        x = _conv_ffn(x, mods, lat_row, norm2_g[layer], w_up, ffn_conv_w[layer], ffn_conv_b[layer], w_dn,
                      final_norm_g if last else None)
        if need_ctx:
            xc = _conv_ffn(xc, mods, ctx_row_fn, norm2_g[layer], w_up, ffn_conv_w[layer], ffn_conv_b[layer], w_dn)
    return x
```

```python
import functools
import math

import numpy as np
import jax
import jax.numpy as jnp
from jax import lax
from jax.experimental import pallas as pl
from jax.experimental.pallas import tpu as pltpu

F32 = jnp.float32
BF16 = jnp.bfloat16
HIGHEST = lax.Precision.HIGHEST

EPS = 1e-6
GRID_W = 64

MLA_NOPE = 64
MLA_ROPE = 32
MLA_V = 64
ROPE_NF = MLA_ROPE // 4
ROPE_BASE = 10000.0
HEAD_LANES = 128
ATTN_TQ = 1024
ATTN_SUB = 256

HY_ORDER = 2
HY_BANDS = 16
HY_SIN_FREQ = 1.0
HY_DECAY_TARGET = 1e-2
HY_FAST_DECAY = 0.3
HY_SLOW_DECAY = 1.5
HY_MIN_DECAY = math.log(HY_DECAY_TARGET) / HY_SLOW_DECAY
HY_MAX_DECAY = math.log(HY_DECAY_TARGET) / HY_FAST_DECAY

HG_EXPAND = 128
HG_CHUNK = 64
HG_LEVELS = (32, 16, 8, 4, 2, 1)
HG_GROUP = 2
HG_UNROLL = 4

MOD_ROWS = 16
VMEM_LIMIT = 56 * 1024 * 1024

_NT = (((1,), (1,)), ((), ()))
_TN = (((0,), (0,)), ((), ()))


def _cparams(*sem):
    return pltpu.CompilerParams(dimension_semantics=sem, vmem_limit_bytes=VMEM_LIMIT)


def _dot(a, b, dims=(((1,), (0,)), ((), ()))):
    return lax.dot_general(a, b, dims, precision=lax.Precision.DEFAULT, preferred_element_type=F32)


def _rms(x, g):
    return x * lax.rsqrt(jnp.mean(x * x, axis=-1, keepdims=True) + EPS) * g


def _modulate(x, g, shift, scale):
    return _rms(x, g) * (1.0 + scale) + shift


def _silu(x):
    return x * jax.nn.sigmoid(x)


def _mod_spec(row_fn, k, d):
    return pl.BlockSpec((None, None, 1, d), lambda *idx: (row_fn(*idx), k, 0, 0))


def _resident(shape):
    nd = len(shape)
    return pl.BlockSpec(shape, lambda *idx: (0,) * nd, pipeline_mode=pl.Buffered(1))


def _mod_kernel(c_ref, w_ref, b_ref, o_ref):
    a = _silu(c_ref[...])
    o_ref[...] = jnp.dot(a, w_ref[...], precision=HIGHEST, preferred_element_type=F32) + b_ref[...]


def _mod_vectors(cc, mod_w, mod_b):
    depth, d, n6 = mod_w.shape
    tn = 512
    out = pl.pallas_call(
        _mod_kernel,
        grid=(depth, n6 // tn),
        in_specs=[pl.BlockSpec((MOD_ROWS, d), lambda l, j: (0, 0)),
                  pl.BlockSpec((None, d, tn), lambda l, j: (l, 0, j)),
                  pl.BlockSpec((None, 1, tn), lambda l, j: (l, 0, j))],
        out_specs=pl.BlockSpec((None, MOD_ROWS, tn), lambda l, j: (l, 0, j)),
        out_shape=jax.ShapeDtypeStruct((depth, MOD_ROWS, n6), F32),
        compiler_params=_cparams("parallel", "parallel"),
        name="mod_vectors",
    )(cc, mod_w, mod_b.reshape(depth, 1, n6))
    return out.reshape(depth, MOD_ROWS, 6, 1, d)


def _ab_proj_kernel(*refs, rope, q_lora, kv_lora, heads, scale):
    if rope:
        (x_ref, sh_ref, sc_ref, g_ref, win_ref, qg_ref, wqa_ref, wqb_ref, kvg_ref, wk_ref, wv_ref,
         cos_ref, sin_ref, q_ref, k_ref, v_ref, hy_ref) = refs
    else:
        (x_ref, sh_ref, sc_ref, g_ref, win_ref, qg_ref, wqa_ref, kvg_ref, wk_ref, wv_ref,
         q_ref, k_ref, v_ref, hy_ref) = refs
    h = _modulate(x_ref[...], g_ref[...], sh_ref[...], sc_ref[...]).astype(BF16)
    z = _dot(h, win_ref[...])
    o = q_lora + kv_lora
    hy_ref[...] = z[:, o + 2 * HEAD_LANES:]
    qn = _rms(z[:, :q_lora], qg_ref[...]).astype(BF16)
    kvn = _rms(z[:, q_lora:o], kvg_ref[...]).astype(BF16)
    v_ref[...] = _dot(kvn, wv_ref[...]).astype(BF16)
    qa = _dot(qn, wqa_ref[...])
    kk = _dot(kvn, wk_ref[...])
    kr = z[:, o:o + HEAD_LANES]
    if rope:
        cos = cos_ref[...]
        sin = sin_ref[...]
        qb = _dot(qn, wqb_ref[...])
        kr = kr * cos + z[:, o + HEAD_LANES:o + 2 * HEAD_LANES] * sin
    for hh in range(heads):
        sl = slice(hh * HEAD_LANES, (hh + 1) * HEAD_LANES)
        qh = qa[:, sl]
        if rope:
            qh = qh * cos + qb[:, sl] * sin
        q_ref[:, sl] = (qh * scale).astype(BF16)
        k_ref[:, sl] = (kk[:, sl] + kr).astype(BF16)


def _rope_swap(w):
    nf = ROPE_NF
    return jnp.concatenate([-w[:, nf:2 * nf], w[:, :nf], -w[:, 3 * nf:4 * nf], w[:, 2 * nf:3 * nf]], axis=1)


def _ab_weights(w_in, w_q_b, w_kv_b, w_out, q_lora, kv_lora, hy_ch):
    d = w_in.shape[0]
    heads = w_q_b.shape[1] // (MLA_NOPE + MLA_ROPE)
    o = q_lora + kv_lora
    w_kr = w_in[:, o:o + MLA_ROPE]
    zpad = lambda n: jnp.zeros((d, n), F32)
    tail = HEAD_LANES - MLA_NOPE - MLA_ROPE
    win = jnp.concatenate(
        [w_in[:, :o],
         zpad(MLA_NOPE), w_kr, zpad(tail),
         zpad(MLA_NOPE), _rope_swap(w_kr), zpad(tail),
         w_in[:, o + MLA_ROPE:]], axis=1).astype(BF16)
    wq = w_q_b.reshape(q_lora, heads, MLA_NOPE + MLA_ROPE)
    zq = jnp.zeros((q_lora, heads, tail), F32)
    wqa = jnp.concatenate([wq, zq], axis=2).reshape(q_lora, heads * HEAD_LANES).astype(BF16)
    wq_rope = wq[:, :, MLA_NOPE:].reshape(q_lora * heads, MLA_ROPE)
    wqb = jnp.concatenate([jnp.zeros((q_lora, heads, MLA_NOPE), F32),
                           _rope_swap(wq_rope).reshape(q_lora, heads, MLA_ROPE), zq],
                          axis=2).reshape(q_lora, heads * HEAD_LANES).astype(BF16)
    wkv = w_kv_b.reshape(kv_lora, heads, MLA_NOPE + MLA_V)
    zk = jnp.zeros((kv_lora, heads, HEAD_LANES - MLA_NOPE), F32)
    wk = jnp.concatenate([wkv[:, :, :MLA_NOPE], zk], axis=2).reshape(kv_lora, heads * HEAD_LANES).astype(BF16)
    zv = jnp.zeros((kv_lora, heads, HEAD_LANES - MLA_V), F32)
    wv = jnp.concatenate([wkv[:, :, MLA_NOPE:], zv], axis=2).reshape(kv_lora, heads * HEAD_LANES).astype(BF16)
    wo = w_out[:heads * MLA_V].reshape(heads, MLA_V, d)
    wo_a = jnp.concatenate([wo, jnp.zeros((heads, HEAD_LANES - MLA_V, d), F32)],
                           axis=1).reshape(heads * HEAD_LANES, d).astype(BF16)
    wo_h = w_out[heads * MLA_V:].astype(BF16)
    return dict(win=win, wqa=wqa, wqb=wqb, wk=wk, wv=wv, wo_a=wo_a, wo_h=wo_h, heads=heads)


def _rope_tables(n):
    rows = n // GRID_W
    row = jnp.repeat(jnp.arange(rows), GRID_W).astype(F32)
    col = jnp.tile(jnp.arange(GRID_W), rows).astype(F32)
    inv = ROPE_BASE ** (-jnp.arange(ROPE_NF, dtype=F32) / ROPE_NF)
    ar = row[:, None] * inv
    ac = col[:, None] * inv
    ones = jnp.ones((n, MLA_NOPE), F32)
    tail = HEAD_LANES - MLA_NOPE - MLA_ROPE
    cos = jnp.concatenate([ones, jnp.cos(ar), jnp.cos(ar), jnp.cos(ac), jnp.cos(ac),
                           jnp.ones((n, tail), F32)], axis=1)
    sin = jnp.concatenate([0 * ones, jnp.sin(ar), jnp.sin(ar), jnp.sin(ac), jnp.sin(ac),
                           jnp.zeros((n, tail), F32)], axis=1)
    return cos, sin


def _ab_project(x, mods, row_fn, norm_g, w, q_norm_g, kv_norm_g, rope_tabs, hy_cols):
    b, n, d = x.shape
    tm = min(512, n)
    heads = w["heads"]
    q_lora = q_norm_g.shape[-1]
    kv_lora = kv_norm_g.shape[-1]
    hw = heads * HEAD_LANES
    rope = rope_tabs is not None
    tok = lambda width: pl.BlockSpec((None, tm, width), lambda bi, i: (bi, i, 0))
    in_specs = [tok(d), _mod_spec(row_fn, 0, d), _mod_spec(row_fn, 1, d), _resident((1, d)),
                _resident(w["win"].shape), _resident((1, q_lora)), _resident(w["wqa"].shape)]
    args = [x, mods, mods, norm_g.reshape(1, d), w["win"], q_norm_g.reshape(1, q_lora), w["wqa"]]
    if rope:
        in_specs.append(_resident(w["wqb"].shape))
        args.append(w["wqb"])
    in_specs += [_resident((1, kv_lora)), _resident(w["wk"].shape), _resident(w["wv"].shape)]
    args += [kv_norm_g.reshape(1, kv_lora), w["wk"], w["wv"]]
    if rope:
        in_specs += [pl.BlockSpec((tm, HEAD_LANES), lambda bi, i: (i, 0))] * 2
        args += list(rope_tabs)
    kern = functools.partial(_ab_proj_kernel, rope=rope, q_lora=q_lora, kv_lora=kv_lora, heads=heads,
                             scale=math.log2(math.e) / math.sqrt(MLA_NOPE + MLA_ROPE))
    return pl.pallas_call(
        kern,
        grid=(b, n // tm),
        in_specs=in_specs,
        out_specs=[tok(hw), tok(hw), tok(hw), tok(hy_cols)],
        out_shape=[jax.ShapeDtypeStruct((b, n, hw), BF16)] * 3 + [jax.ShapeDtypeStruct((b, n, hy_cols), F32)],
        compiler_params=_cparams("parallel", "parallel"),
        name="ab_project",
    )(*args)


def _attn_kernel(*refs, sub, nseg):
    q_ref, o_ref = refs[0], refs[-1]
    ks = [r[...] for r in refs[1:1 + nseg]]
    vs = [r[...] for r in refs[1 + nseg:1 + 2 * nseg]]
    for r in range(0, q_ref.shape[0], sub):
        q = q_ref[r:r + sub, :]
        ss = [_dot(q, k, _NT) for k in ks]
        m = functools.reduce(jnp.maximum, [jnp.max(s, axis=-1, keepdims=True) for s in ss])
        ps = [jnp.exp2(s - m) for s in ss]
        l = sum(jnp.sum(p, axis=-1, keepdims=True) for p in ps)
        o = sum(_dot(p.astype(BF16), v) for p, v in zip(ps, vs))
        o_ref[r:r + sub, :] = (o / l).astype(o_ref.dtype)


def _attention(q, ks, vs, heads):
    b, n, hw = q.shape
    tq = min(ATTN_TQ, n)
    tile = pl.BlockSpec((None, tq, HEAD_LANES), lambda bi, h, i: (bi, i, h))
    whole = lambda a: pl.BlockSpec((None, a.shape[1], HEAD_LANES), lambda bi, h, i: (bi, 0, h))
    return pl.pallas_call(
        functools.partial(_attn_kernel, sub=min(ATTN_SUB, tq), nseg=len(ks)),
        grid=(b, heads, n // tq),
        in_specs=[tile] + [whole(a) for a in ks] + [whole(a) for a in vs],
        out_specs=tile,
        out_shape=jax.ShapeDtypeStruct((b, n, hw), BF16),
        compiler_params=_cparams("parallel", "parallel", "parallel"),
        name="attention",
    )(q, *ks, *vs)


def _shift_rows(a, direction):
    n = a.shape[0]
    row = lax.broadcasted_iota(jnp.int32, a.shape, 0)
    if direction > 0:
        return jnp.where(row == 0, 0.0, pltpu.roll(a, 1, axis=0))
    return jnp.where(row == n - 1, 0.0, pltpu.roll(a, n - 1, axis=0))


def _dwconv_kernel(x_ref, w_ref, b_ref, o_ref, ob_ref):
    x = x_ref[...]
    w = w_ref[...]
    y = _shift_rows(x, 1) * w[0:1] + x * w[1:2] + _shift_rows(x, -1) * w[2:3] + b_ref[...]
    o_ref[...] = y
    ob_ref[...] = y.astype(BF16)


def _hy_dwconv(hy, conv_w, conv_b):
    b, n, c3 = hy.shape
    tc = 256 if c3 % 256 == 0 else 128
    assert c3 % tc == 0
    blk = pl.BlockSpec((None, n, tc), lambda bi, j: (bi, 0, j))
    return pl.pallas_call(
        _dwconv_kernel,
        grid=(b, c3 // tc),
        in_specs=[blk, pl.BlockSpec((3, tc), lambda bi, j: (0, j)), pl.BlockSpec((1, tc), lambda bi, j: (0, j))],
        out_specs=[blk, blk],
        out_shape=[jax.ShapeDtypeStruct((b, n, c3), F32), jax.ShapeDtypeStruct((b, n, c3), BF16)],
        compiler_params=_cparams("parallel", "parallel"),
        name="hy_dwconv",
    )(hy, conv_w, conv_b.reshape(1, c3))


def _hy_filter_kernel(z_ref, t_ref, w1_ref, b1_ref, w2_ref, b2_ref, w3_ref, dl_ref, o_ref, *, ch):
    hdot = lambda a, w: jnp.dot(a, w, precision=HIGHEST, preferred_element_type=F32)
    a = jnp.sin(HY_SIN_FREQ * (hdot(z_ref[...], w1_ref[...]) + b1_ref[...]))
    a = jnp.sin(HY_SIN_FREQ * (hdot(a, w2_ref[...]) + b2_ref[...]))
    f = hdot(a, w3_ref[...])
    dec = jnp.exp(-t_ref[...] * dl_ref[...])
    tl = z_ref.shape[0]
    row = pl.program_id(0) * tl + lax.broadcasted_iota(jnp.int32, (tl, ch), 0)
    for o in range(HY_ORDER):
        hf = f[:, (2 * o) * ch:(2 * o + 1) * ch] * dec
        hb = jnp.where(row > 0, f[:, (2 * o + 1) * ch:(2 * o + 2) * ch] * dec, 0.0)
        o_ref[:, (2 * o) * ch:(2 * o + 1) * ch] = (hf + hb).astype(BF16)
        o_ref[:, (2 * o + 1) * ch:(2 * o + 2) * ch] = (hf - hb).astype(BF16)


def _hy_filter_inputs(length, w1, b1, w2, b2, w3, ch):
    t = jnp.linspace(0.0, 1.0, length, dtype=F32)[:, None]
    wv = 2.0 * math.pi * jnp.arange(length, dtype=F32)[:, None] / length
    f = jnp.linspace(1e-4, HY_BANDS - 1, HY_BANDS, dtype=F32)
    z = jnp.concatenate([t, jnp.cos(f * wv), -jnp.sin(f * wv)], axis=-1)
    emb = z.shape[1]
    z = jnp.pad(z, ((0, 0), (0, HEAD_LANES - emb)))
    w1p = jnp.pad(w1, ((0, HEAD_LANES - emb), (0, 0)))
    deltas = jnp.abs(jnp.linspace(HY_MIN_DECAY, HY_MAX_DECAY, ch, dtype=F32))[None, :]
    fw = w1.shape[1]
    tl = min(512, length)
    return pl.pallas_call(
        functools.partial(_hy_filter_kernel, ch=ch),
        grid=(length // tl,),
        in_specs=[pl.BlockSpec((tl, HEAD_LANES), lambda i: (i, 0)), pl.BlockSpec((tl, 1), lambda i: (i, 0)),
                  _resident((HEAD_LANES, fw)), _resident((1, fw)), _resident((fw, fw)), _resident((1, fw)),
                  _resident(w3.shape), _resident((1, ch))],
        out_specs=pl.BlockSpec((tl, 2 * HY_ORDER * ch), lambda i: (i, 0)),
        out_shape=jax.ShapeDtypeStruct((length, 2 * HY_ORDER * ch), BF16),
        compiler_params=_cparams("parallel"),
        name="hy_filter",
    )(z, t, w1p, b1.reshape(1, fw), w2, b2.reshape(1, fw), w3, deltas)


def _dft_tables(length):
    n2 = 2 * length
    theta = 2.0 * math.pi / n2
    n = jnp.arange(length, dtype=jnp.int32)[None, :]
    a = 8 * jnp.arange(length // 8, dtype=jnp.int32)[:, None]
    b = jnp.arange(8, dtype=jnp.int32)[:, None]
    ang_a = ((a * n) % n2).astype(F32) * theta
    ang_b = ((b * n) % n2).astype(F32) * theta
    ca, sa = jnp.cos(ang_a)[:, None, :], jnp.sin(ang_a)[:, None, :]
    cb, sb = jnp.cos(ang_b)[None, :, :], jnp.sin(ang_b)[None, :, :]
    fc = (ca * cb - sa * sb).reshape(length, length)
    fs = -(sa * cb + ca * sb).reshape(length, length)
    return fc.astype(BF16), fs.astype(BF16)


def _alternating_sum(x):
    col = lax.broadcasted_iota(jnp.int32, (8, x.shape[0]), 1)
    alt = (1 - 2 * (col & 1)).astype(F32).astype(BF16)
    return _dot(alt, x)[0:1]


def _filter_dft_kernel(hs_ref, hd_ref, fc_ref, fs_ref, o_ref):
    hs = hs_ref[...]
    o_ref[0] = _dot(fc_ref[...], hs)
    ki = _dot(fs_ref[...], hd_ref[...])
    nyq = _alternating_sum(hs)
    tf = ki.shape[0]
    row = pl.program_id(1) * tf + lax.broadcasted_iota(jnp.int32, (tf, 1), 0)
    o_ref[1] = jnp.where(row == 0, nyq, ki)


def _dft_spec_kernel(x_ref, fc_ref, fs_ref, k_ref, o_ref, *, inv_n):
    x = x_ref[...]
    ur = _dot(fc_ref[...], x)
    kr = k_ref[0]
    ki = k_ref[1]
    tf = ur.shape[0]
    row = pl.program_id(1) * tf + lax.broadcasted_iota(jnp.int32, (tf, 1), 0)
    first = row == 0
    ui = jnp.where(first, _alternating_sum(x), _dot(fs_ref[...], x))
    uiki = ui * ki
    yr = ur * kr - jnp.where(first, 0.0, uiki)
    yi = jnp.where(first, uiki, ur * ki + ui * kr)
    scale = jnp.where(first, inv_n, 2.0 * inv_n)
    o_ref[0] = (yr * scale).astype(BF16)
    o_ref[1] = (yi * scale).astype(BF16)


def _filter_spectrum(p, fc, fs, ch):
    length, cols = p.shape
    tf = min(512, length)
    return pl.pallas_call(
        _filter_dft_kernel,
        grid=(HY_ORDER, length // tf),
        in_specs=[pl.BlockSpec((length, ch), lambda o, i: (0, 2 * o)),
                  pl.BlockSpec((length, ch), lambda o, i: (0, 2 * o + 1)),
                  pl.BlockSpec((tf, length), lambda o, i: (i, 0)),
                  pl.BlockSpec((tf, length), lambda o, i: (i, 0))],
        out_specs=pl.BlockSpec((None, 2, tf, ch), lambda o, i: (o, 0, i, 0)),
        out_shape=jax.ShapeDtypeStruct((HY_ORDER, 2, length, ch), F32),
        compiler_params=_cparams("parallel", "arbitrary"),
        name="hy_filter_dft",
    )(p, p, fc, fs)


def _hy_spectral(xb, col, ch, fc, fs, kspec):
    b, length, _ = xb.shape
    tf = min(512, length)
    return pl.pallas_call(
        functools.partial(_dft_spec_kernel, inv_n=1.0 / (2 * length)),
        grid=(b, length // tf),
        in_specs=[pl.BlockSpec((None, length, ch), lambda bi, i: (bi, 0, col)),
                  pl.BlockSpec((tf, length), lambda bi, i: (i, 0)),
                  pl.BlockSpec((tf, length), lambda bi, i: (i, 0)),
                  pl.BlockSpec((2, tf, ch), lambda bi, i: (0, i, 0))],
        out_specs=pl.BlockSpec((None, 2, tf, ch), lambda bi, i: (bi, 0, i, 0)),
        out_shape=jax.ShapeDtypeStruct((b, 2, length, ch), BF16),
        compiler_params=_cparams("parallel", "arbitrary"),
        name="hy_dft_spectral",
    )(xb, fc, fs, kspec)


def _idft_gate_kernel(y_ref, fc_ref, fs_ref, u_ref, xg_ref, d_ref, o_ref, ob_ref):
    tt = fc_ref.shape[0]
    t = pl.program_id(1) * tt + lax.broadcasted_iota(jnp.int32, (tt, 1), 0)
    sign = (1 - 2 * (t & 1)).astype(F32)
    conv = (_dot(fc_ref[...], y_ref[0]) + _dot(fs_ref[...], y_ref[1])
            + sign * y_ref[1, 0:1, :].astype(F32))
    out = xg_ref[...] * (conv + u_ref[...] * d_ref[...])
    o_ref[...] = out
    ob_ref[...] = out.astype(BF16)


def _hy_inverse_gate(y, fc, fs, u, u_col, xg, xg_col, dvec, ch):
    b, _, length, _ = y.shape
    tt = min(512, length)
    out = pl.BlockSpec((None, tt, ch), lambda bi, i: (bi, i, 0))
    return pl.pallas_call(
        _idft_gate_kernel,
        grid=(b, length // tt),
        in_specs=[pl.BlockSpec((None, 2, length, ch), lambda bi, i: (bi, 0, 0, 0)),
                  pl.BlockSpec((tt, length), lambda bi, i: (i, 0)),
                  pl.BlockSpec((tt, length), lambda bi, i: (i, 0)),
                  pl.BlockSpec((None, tt, ch), lambda bi, i: (bi, i, u_col)),
                  pl.BlockSpec((None, tt, ch), lambda bi, i: (bi, i, xg_col)),
                  pl.BlockSpec((1, ch), lambda bi, i: (0, 0))],
        out_specs=[out, out],
        out_shape=[jax.ShapeDtypeStruct((b, length, ch), F32), jax.ShapeDtypeStruct((b, length, ch), BF16)],
        compiler_params=_cparams("parallel", "arbitrary"),
        name="hy_idft_gate",
    )(y, fc, fs, u, xg, dvec.reshape(1, ch))


def _hyena(hy, conv_w, conv_b, w1, b1, w2, b2, w3, hy_bias):
    ch = hy_bias.shape[-1]
    length = hy.shape[1]
    hyc, hyb = _hy_dwconv(hy, conv_w, conv_b)
    fc, fs = _dft_tables(length)
    kspec = _filter_spectrum(_hy_filter_inputs(length, w1, b1, w2, b2, w3, ch), fc, fs, ch)
    y = _hy_spectral(hyb, 0, ch, fc, fs, kspec[0])
    y1, y1b = _hy_inverse_gate(y, fc, fs, hyc, 0, hyc, 1, hy_bias[0], ch)
    y = _hy_spectral(y1b, 0, ch, fc, fs, kspec[1])
    y2, _ = _hy_inverse_gate(y, fc, fs, y1, 0, hyc, 2, hy_bias[1], ch)
    return y2


def _ab_out_kernel(o_ref, y_ref, x_ref, g1_ref, woa_ref, woh_ref, out_ref):
    y = _dot(o_ref[...], woa_ref[...]) + _dot(y_ref[...].astype(BF16), woh_ref[...])
    out_ref[...] = x_ref[...] + g1_ref[...] * y


def _ab_out(o, y2, x, mods, row_fn, w):
    b, n, d = x.shape
    tm = min(512, n)
    tok = lambda width: pl.BlockSpec((None, tm, width), lambda bi, i: (bi, i, 0))
    return pl.pallas_call(
        _ab_out_kernel,
        grid=(b, n // tm),
        in_specs=[tok(o.shape[-1]), tok(y2.shape[-1]), tok(d), _mod_spec(row_fn, 2, d),
                  _resident(w["wo_a"].shape), _resident(w["wo_h"].shape)],
        out_specs=tok(d),
        out_shape=jax.ShapeDtypeStruct((b, n, d), F32),
        compiler_params=_cparams("parallel", "parallel"),
        name="ab_out",
    )(o, y2, x, mods, w["wo_a"], w["wo_h"])


HALO = 8
FFN_ROWS = 512
FFN_CHUNK_MAX = 1408


def _ffn_kernel(*refs, hidden, hc, final):
    if final:
        (x_ref, xp_ref, xn_ref, sh_ref, sc_ref, g2_ref, ng_ref, wup_ref, cw_ref, cb_ref, wdn_ref,
         fg_ref, o_ref) = refs
    else:
        (x_ref, xp_ref, xn_ref, sh_ref, sc_ref, g2_ref, ng_ref, wup_ref, cw_ref, cb_ref, wdn_ref,
         o_ref) = refs
    i = pl.program_id(1)
    last = pl.num_programs(1) - 1
    x = x_ref[...]
    tm, d = x.shape
    mod = lambda a: _modulate(a, ng_ref[...], sh_ref[...], sc_ref[...])
    hp = mod(xp_ref[...]) * (i > 0).astype(F32)
    hn = mod(xn_ref[...]) * (i < last).astype(F32)
    he = jnp.concatenate([hp, mod(x), hn], axis=0).astype(BF16)
    ne = tm + 2 * HALO

    def conv(a, col):
        w = cw_ref[:, col:col + hc]
        prev = pltpu.roll(a, 1, axis=0)[HALO:HALO + tm]
        nxt = pltpu.roll(a, ne - 1, axis=0)[HALO:HALO + tm]
        return prev * w[0:1] + a[HALO:HALO + tm] * w[1:2] + nxt * w[2:3] + cb_ref[:, col:col + hc]

    acc = jnp.zeros((tm, d), F32)
    for j in range(hidden // hc):
        gate = conv(_dot(he, wup_ref[:, j * hc:(j + 1) * hc]), j * hc)
        val = conv(_dot(he, wup_ref[:, hidden + j * hc:hidden + (j + 1) * hc]), hidden + j * hc)
        mid = (_silu(gate) * val).astype(BF16)
        acc = acc + _dot(mid, wdn_ref[j * hc:(j + 1) * hc, :])
    out = x + g2_ref[...] * acc
    if final:
        out = _rms(out, fg_ref[...])
    o_ref[...] = out


def _ffn_chunk(hidden):
    for hc in range(min(FFN_CHUNK_MAX, hidden) // HEAD_LANES * HEAD_LANES, 0, -HEAD_LANES):
        if hidden % hc == 0:
            return hc
    raise ValueError(f"FFN hidden width {hidden} is not a multiple of {HEAD_LANES}")


def _conv_ffn(x, mods, row_fn, norm_g, w_up, conv_w, conv_b, w_down, final_g=None):
    b, n, d = x.shape
    hidden = w_down.shape[0]
    tm = min(FFN_ROWS, n)
    nh = n // HALO
    tpb = tm // HALO
    final = final_g is not None
    tok = pl.BlockSpec((None, tm, d), lambda bi, i: (bi, i, 0))
    in_specs = [tok,
                pl.BlockSpec((None, HALO, d), lambda bi, i: (bi, jnp.maximum(i * tpb - 1, 0), 0)),
                pl.BlockSpec((None, HALO, d), lambda bi, i: (bi, jnp.minimum((i + 1) * tpb, nh - 1), 0)),
                _mod_spec(row_fn, 3, d), _mod_spec(row_fn, 4, d), _mod_spec(row_fn, 5, d),
                _resident((1, d)), _resident(w_up.shape), _resident(conv_w.shape),
                _resident((1, 2 * hidden)), _resident(w_down.shape)]
    args = [x, x, x, mods, mods, mods, norm_g.reshape(1, d), w_up, conv_w, conv_b.reshape(1, 2 * hidden), w_down]
    if final:
        in_specs.append(_resident((1, d)))
        args.append(final_g.reshape(1, d))
    return pl.pallas_call(
        functools.partial(_ffn_kernel, hidden=hidden, hc=_ffn_chunk(hidden), final=final),
        grid=(b, n // tm),
        in_specs=in_specs,
        out_specs=tok,
        out_shape=jax.ShapeDtypeStruct((b, n, d), F32),
        compiler_params=_cparams("parallel", "parallel"),
        name="conv_ffn",
    )(*args)


def _hg_proj_kernel(x_ref, sh_ref, sc_ref, g_ref, w_ref, o_ref, *, width):
    h = _modulate(x_ref[...], g_ref[...], sh_ref[...], sc_ref[...]).astype(BF16)
    z = _dot(h, w_ref[...])
    o_ref[:, :width] = _silu(z[:, :width])
    o_ref[:, width:] = z[:, width:]


def _hg_project(x, mods, row_fn, norm_g, w_in):
    b, n, d = x.shape
    cols = w_in.shape[1]
    tm = min(512, n)
    tok = lambda width: pl.BlockSpec((None, tm, width), lambda bi, i: (bi, i, 0))
    return pl.pallas_call(
        functools.partial(_hg_proj_kernel, width=cols // 5),
        grid=(b, n // tm),
        in_specs=[tok(d), _mod_spec(row_fn, 0, d), _mod_spec(row_fn, 1, d), _resident((1, d)),
                  _resident(w_in.shape)],
        out_specs=tok(cols),
        out_shape=jax.ShapeDtypeStruct((b, n, cols), F32),
        compiler_params=_cparams("parallel", "parallel"),
        name="hg_project",
    )(x, mods, mods, norm_g.reshape(1, d), w_in)


def _scan_tables(direction):
    c = HG_CHUNK
    idx = np.arange(c)
    ops = []
    level = np.full((c, c), -1, np.int32)
    for li, m in enumerate(HG_LEVELS):
        op = np.zeros((c, c), np.float32)
        for r in range(c):
            lo = (r // m) * m
            hi = lo + m - 1
            upper = (r % (2 * m)) >= m
            if direction == 0:
                if upper:
                    op[r, lo:r + 1] = 1
                else:
                    op[r, r + 1:hi + 1] = 1
            else:
                if not upper:
                    op[r, r:hi + 1] = 1
                else:
                    op[r, lo:r] = 1
        ops.append(op)
        same = (idx[:, None] // (2 * m)) == (idx[None, :] // (2 * m))
        up = (idx % (2 * m)) >= m
        pair = same & (up[:, None] & ~up[None, :] if direction == 0 else ~up[:, None] & up[None, :])
        level[pair] = li
    level[idx, idx] = len(HG_LEVELS)
    if direction == 0:
        to_q = (idx[None, :] <= idx[:, None])
    else:
        to_q = (idx[None, :] >= idx[:, None])
    ops += [to_q.astype(np.float32), np.ones((8, c), np.float32)]
    em = np.concatenate(ops, axis=0)
    return jnp.asarray(np.concatenate([em, em], axis=1), BF16), jnp.asarray(level)


def _gla_group(chunks, lb, em2, lv, st):
    c = HG_CHUNK
    e_ = HG_EXPAND
    nl = len(HG_LEVELS)
    log_lb = jnp.log(lb)
    log_1m = jnp.log1p(-lb)
    log_f = []
    keys = []
    for _, fr, _ in chunks:
        t = jnp.exp(-jnp.abs(fr))
        d = 1.0 + t
        keys.append((1.0 - lb) * (jnp.where(fr > 0, t, 1.0) / d))
        b = log_1m + (jnp.minimum(fr, 0.0) - jnp.log(d))
        log_f.append(jnp.maximum(log_lb, b) + jnp.log(1.0 + jnp.exp(-jnp.abs(log_lb - b))))
    g = jnp.concatenate(log_f, axis=1) * math.log2(math.e)
    hi = g.astype(BF16)
    lo = (g - hi.astype(F32)).astype(BF16)
    e_all = _dot(em2, jnp.concatenate([hi, lo], axis=0))
    outs = []
    for gi, (q, _, val) in enumerate(chunks):
        e = e_all[:, gi * e_:(gi + 1) * e_]
        key = keys[gi]
        att = jnp.zeros((c, c), F32)
        for li in range(nl):
            dec = jnp.exp2(e[li * c:(li + 1) * c])
            att = jnp.where(lv == li, _dot((q * dec).astype(BF16), (key * dec).astype(BF16), _NT), att)
        att = jnp.where(lv == nl, jnp.sum(q * key, axis=1, keepdims=True), att)
        vb = val.astype(BF16)
        to_q = e[nl * c:(nl + 1) * c]
        total = e[(nl + 1) * c:(nl + 1) * c + 1]
        q_in = (q * jnp.exp2(to_q)).astype(BF16)
        outs.append(_dot(att.astype(BF16), vb) + _dot(q_in, st.astype(BF16), _NT))
        k_out = (key * jnp.exp2(total - to_q)).astype(BF16)
        st = st * jnp.exp2(total) + _dot(vb, k_out, _TN)
    return outs, st


def _hg_scan_kernel(*refs, layer, need_ctx):
    (lg_ref, emf_ref, emb_ref, lvf_ref, lvb_ref,
     qc_ref, ffc_ref, fbc_ref, ic_ref, q_ref, ff_ref, fb_ref, i_ref) = refs[:13]
    if need_ctx:
        o_ref, oc_ref, st_ref = refs[13:]
    else:
        o_ref, st_ref = refs[13:]
        oc_ref = None
    c = HG_CHUNK
    lg = lg_ref[...]
    ex = jnp.exp(lg - jnp.max(lg, axis=0, keepdims=True))
    probs = ex / jnp.sum(ex, axis=0, keepdims=True)
    acc = probs[0]
    for l in range(1, layer + 1):
        acc = acc + probs[l]
    lb_all = acc - probs[0]
    lb_f = lb_all[0:1]
    lb_b = lb_all[1:2]

    st_ref[...] = jnp.zeros_like(st_ref)
    o_ref[...] = jnp.zeros_like(o_ref)
    if need_ctx:
        oc_ref[...] = jnp.zeros_like(oc_ref)

    def run(qr, ffr, fbr, ir, outr):
        nc = qr.shape[0] // c
        ng = nc // HG_GROUP
        assert ng * HG_GROUP == nc

        def body(gi, carry):
            rows_f = [pl.multiple_of((gi * HG_GROUP + k) * c, c) for k in range(HG_GROUP)]
            rows_b = [pl.multiple_of((nc - 1 - gi * HG_GROUP - k) * c, c) for k in range(HG_GROUP)]
            load = lambda gate_ref, r: (qr[pl.ds(r, c), :], gate_ref[pl.ds(r, c), :], ir[pl.ds(r, c), :])
            of, stf = _gla_group([load(ffr, r) for r in rows_f], lb_f, emf_ref[...], lvf_ref[...], st_ref[0])
            st_ref[0] = stf
            ob, stb = _gla_group([load(fbr, r) for r in rows_b], lb_b, emb_ref[...], lvb_ref[...], st_ref[1])
            st_ref[1] = stb
            if outr is not None:
                for r, o in zip(rows_f + rows_b, of + ob):
                    outr[pl.ds(r, c), :] += o
            return carry

        lax.fori_loop(0, ng, body, 0, unroll=min(HG_UNROLL, ng))

    run(qc_ref, ffc_ref, fbc_ref, ic_ref, oc_ref)
    run(q_ref, ff_ref, fb_ref, i_ref, o_ref)


def _hg_scan(z, zc, lb_logits, layer, need_ctx):
    b, n, cols = z.shape
    nc_ = zc.shape[1]
    width = cols // 5
    heads = width // HG_EXPAND
    e = HG_EXPAND
    depth = lb_logits.shape[0]
    emf, lvf = _scan_tables(0)
    emb, lvb = _scan_tables(1)
    col = lambda rows, group: pl.BlockSpec((None, rows, e), lambda bi, h: (bi, 0, group * heads + h))
    in_specs = [pl.BlockSpec((depth, 2, e), lambda bi, h: (0, 0, h)),
                _resident(emf.shape), _resident(emb.shape), _resident(lvf.shape), _resident(lvb.shape),
                col(nc_, 0), col(nc_, 1), col(nc_, 2), col(nc_, 3),
                col(n, 0), col(n, 1), col(n, 2), col(n, 3)]
    out_specs = [pl.BlockSpec((None, n, e), lambda bi, h: (bi, 0, h))]
    out_shape = [jax.ShapeDtypeStruct((b, n, width), F32)]
    if need_ctx:
        out_specs.append(pl.BlockSpec((None, nc_, e), lambda bi, h: (bi, 0, h)))
        out_shape.append(jax.ShapeDtypeStruct((b, nc_, width), F32))
    res = pl.pallas_call(
        functools.partial(_hg_scan_kernel, layer=layer, need_ctx=need_ctx),
        grid=(b, heads),
        in_specs=in_specs,
        out_specs=out_specs,
        out_shape=out_shape,
        scratch_shapes=[pltpu.VMEM((2, e, e), F32)],
        compiler_params=_cparams("parallel", "parallel"),
        name="hg_scan",
    )(lb_logits, emf, emb, lvf, lvb, zc, zc, zc, zc, z, z, z, z)
    return (res[0], res[1]) if need_ctx else (res[0], None)


def _hg_out_kernel(o_ref, g_ref, x_ref, g1_ref, ng_ref, w_ref, out_ref, *, heads):
    ng = ng_ref[...]
    parts = [_rms(o_ref[:, h * HG_EXPAND:(h + 1) * HG_EXPAND], ng) for h in range(heads)]
    o = jnp.concatenate(parts, axis=1) * _silu(g_ref[...])
    out_ref[...] = x_ref[...] + g1_ref[...] * _dot(o.astype(BF16), w_ref[...])


def _hg_out(o, z, x, mods, row_fn, norm_g, w_out):
    b, n, d = x.shape
    width = o.shape[-1]
    tm = min(512, n)
    tok = lambda wd: pl.BlockSpec((None, tm, wd), lambda bi, i: (bi, i, 0))
    return pl.pallas_call(
        functools.partial(_hg_out_kernel, heads=width // HG_EXPAND),
        grid=(b, n // tm),
        in_specs=[tok(width), pl.BlockSpec((None, tm, width), lambda bi, i: (bi, i, 4)), tok(d),
                  _mod_spec(row_fn, 2, d), _resident((1, HG_EXPAND)), _resident(w_out.shape)],
        out_specs=tok(d),
        out_shape=jax.ShapeDtypeStruct((b, n, d), F32),
        compiler_params=_cparams("parallel", "parallel"),
        name="hg_out",
    )(o, z, x, mods, norm_g.reshape(1, HG_EXPAND), w_out)


def kernel(x, c, ctx, c_ctx, mod_w, mod_b, norm1_g, norm2_g, ffn_w_up, ffn_conv_w, ffn_conv_b, ffn_w_down, ab_w_in, mla_q_norm_g, mla_w_q_b, mla_kv_norm_g, mla_w_kv_b, hy_conv_w, hy_conv_b, hy_w1, hy_b1, hy_w2, hy_b2, hy_w3, hy_bias, ab_w_out, hg_w_in, hg_lb_logits, hg_norm_g, hg_w_out, final_norm_g):
    batch, n, d = x.shape
    depth = mod_w.shape[0]
    assert batch < MOD_ROWS
    ctx_row = batch
    cc = jnp.zeros((MOD_ROWS, d), F32).at[:batch].set(c).at[ctx_row].set(c_ctx)
    mods_all = _mod_vectors(cc, mod_w, mod_b)
    lat_row = lambda bi, *_: bi
    ctx_row_fn = lambda bi, *_: ctx_row
    rope_tabs = _rope_tables(n)

    xc = ctx
    for layer in range(depth):
        need_ctx = layer < depth - 1
        j = layer // 2
        mods = mods_all[layer]
        if layer % 2 == 0:
            q_lora = mla_q_norm_g.shape[-1]
            kv_lora = mla_kv_norm_g.shape[-1]
            hy_ch = hy_bias.shape[-1]
            w = _ab_weights(ab_w_in[j], mla_w_q_b[j], mla_w_kv_b[j], ab_w_out[j], q_lora, kv_lora, hy_ch)
            hy_cols = (HY_ORDER + 1) * hy_ch
            hy_args = (hy_conv_w[j], hy_conv_b[j], hy_w1[j], hy_b1[j], hy_w2[j], hy_b2[j], hy_w3[j], hy_bias[j])
            q_c, k_c, v_c, hyin_c = _ab_project(xc, mods, ctx_row_fn, norm1_g[layer], w, mla_q_norm_g[j],
                                                mla_kv_norm_g[j], None, hy_cols)
            q, k, v, hyin = _ab_project(x, mods, lat_row, norm1_g[layer], w, mla_q_norm_g[j],
                                        mla_kv_norm_g[j], rope_tabs, hy_cols)
            o = _attention(q, [jnp.concatenate([k_c, k], axis=1)], [jnp.concatenate([v_c, v], axis=1)], w["heads"])
            x_new = _ab_out(o, _hyena(hyin, *hy_args), x, mods, lat_row, w)
            if need_ctx:
                o_c = _attention(q_c, [k_c], [v_c], w["heads"])
                xc = _ab_out(o_c, _hyena(hyin_c, *hy_args), xc, mods, ctx_row_fn, w)
            x = x_new
        else:
            w_in = hg_w_in[j].astype(BF16)
            z_c = _hg_project(xc, mods, ctx_row_fn, norm1_g[layer], w_in)
            z = _hg_project(x, mods, lat_row, norm1_g[layer], w_in)
            o, o_c = _hg_scan(z, z_c, hg_lb_logits, layer, need_ctx)
            w_out = hg_w_out[j].astype(BF16)
            x = _hg_out(o, z, x, mods, lat_row, hg_norm_g[j], w_out)
            if need_ctx:
                xc = _hg_out(o_c, z_c, xc, mods, ctx_row_fn, hg_norm_g[j], w_out)
        w_up = ffn_w_up[layer].astype(BF16)
        w_dn = ffn_w_down[layer].astype(BF16)
        last = layer == depth - 1
        x = _conv_ffn(x, mods, lat_row, norm2_g[layer], w_up, ffn_conv_w[layer], ffn_conv_b[layer], w_dn,
                      final_norm_g if last else None)
        if need_ctx:
            xc = _conv_ffn(xc, mods, ctx_row_fn, norm2_g[layer], w_up, ffn_conv_w[layer], ffn_conv_b[layer], w_dn)
    return x
```

```python
import functools
import math

import numpy as np
import jax
import jax.numpy as jnp
from jax import lax
from jax.experimental import pallas as pl
from jax.experimental.pallas import tpu as pltpu

F32 = jnp.float32
BF16 = jnp.bfloat16
HIGHEST = lax.Precision.HIGHEST

EPS = 1e-6
GRID_W = 64

MLA_NOPE = 64
MLA_ROPE = 32
MLA_V = 64
ROPE_NF = MLA_ROPE // 4
ROPE_BASE = 10000.0
HEAD_LANES = 128
ATTN_TQ = 1024
ATTN_SUB = 256

HY_ORDER = 2
HY_BANDS = 16
HY_SIN_FREQ = 1.0
HY_DECAY_TARGET = 1e-2
HY_FAST_DECAY = 0.3
HY_SLOW_DECAY = 1.5
HY_MIN_DECAY = math.log(HY_DECAY_TARGET) / HY_SLOW_DECAY
HY_MAX_DECAY = math.log(HY_DECAY_TARGET) / HY_FAST_DECAY

FFT_N1 = 64
FFT_KB = 8
FFT_COLS = 8192
FFT_MIN_LEN = 1024

HG_EXPAND = 128
HG_CHUNK = 64
HG_LEVELS = (32, 16, 8, 4, 2, 1)
HG_GROUP = 2
HG_UNROLL = 4

MOD_ROWS = 16
VMEM_LIMIT = 56 * 1024 * 1024

_NT = (((1,), (1,)), ((), ()))
_TN = (((0,), (0,)), ((), ()))


def _cparams(*sem):
    return pltpu.CompilerParams(dimension_semantics=sem, vmem_limit_bytes=VMEM_LIMIT)


def _dot(a, b, dims=(((1,), (0,)), ((), ()))):
    return lax.dot_general(a, b, dims, precision=lax.Precision.DEFAULT, preferred_element_type=F32)


def _rms(x, g):
    return x * lax.rsqrt(jnp.mean(x * x, axis=-1, keepdims=True) + EPS) * g


def _modulate(x, g, shift, scale):
    return _rms(x, g) * (1.0 + scale) + shift


def _silu(x):
    return x * jax.nn.sigmoid(x)


def _mod_spec(row_fn, k, d):
    return pl.BlockSpec((None, None, 1, d), lambda *idx: (row_fn(*idx), k, 0, 0))


def _resident(shape):
    nd = len(shape)
    return pl.BlockSpec(shape, lambda *idx: (0,) * nd, pipeline_mode=pl.Buffered(1))


def _mod_kernel(c_ref, w_ref, b_ref, o_ref):
    a = _silu(c_ref[...])
    o_ref[...] = jnp.dot(a, w_ref[...], precision=HIGHEST, preferred_element_type=F32) + b_ref[...]


def _mod_vectors(cc, mod_w, mod_b):
    depth, d, n6 = mod_w.shape
    tn = 512
    out = pl.pallas_call(
        _mod_kernel,
        grid=(depth, n6 // tn),
        in_specs=[pl.BlockSpec((MOD_ROWS, d), lambda l, j: (0, 0)),
                  pl.BlockSpec((None, d, tn), lambda l, j: (l, 0, j)),
                  pl.BlockSpec((None, 1, tn), lambda l, j: (l, 0, j))],
        out_specs=pl.BlockSpec((None, MOD_ROWS, tn), lambda l, j: (l, 0, j)),
        out_shape=jax.ShapeDtypeStruct((depth, MOD_ROWS, n6), F32),
        compiler_params=_cparams("parallel", "parallel"),
        name="mod_vectors",
    )(cc, mod_w, mod_b.reshape(depth, 1, n6))
    return out.reshape(depth, MOD_ROWS, 6, 1, d)


def _ab_proj_kernel(*refs, rope, q_lora, kv_lora, heads, scale):
    if rope:
        (x_ref, sh_ref, sc_ref, g_ref, win_ref, qg_ref, wqa_ref, wqb_ref, kvg_ref, wk_ref, wv_ref,
         cos_ref, sin_ref, q_ref, k_ref, v_ref, hy_ref) = refs
    else:
        (x_ref, sh_ref, sc_ref, g_ref, win_ref, qg_ref, wqa_ref, kvg_ref, wk_ref, wv_ref,
         q_ref, k_ref, v_ref, hy_ref) = refs
    h = _modulate(x_ref[...], g_ref[...], sh_ref[...], sc_ref[...]).astype(BF16)
    z = _dot(h, win_ref[...])
    o = q_lora + kv_lora
    hy_ref[...] = z[:, o + 2 * HEAD_LANES:]
    qn = _rms(z[:, :q_lora], qg_ref[...]).astype(BF16)
    kvn = _rms(z[:, q_lora:o], kvg_ref[...]).astype(BF16)
    v_ref[...] = _dot(kvn, wv_ref[...]).astype(BF16)
    qa = _dot(qn, wqa_ref[...])
    kk = _dot(kvn, wk_ref[...])
    kr = z[:, o:o + HEAD_LANES]
    if rope:
        cos = cos_ref[...]
        sin = sin_ref[...]
        qb = _dot(qn, wqb_ref[...])
        kr = kr * cos + z[:, o + HEAD_LANES:o + 2 * HEAD_LANES] * sin
    for hh in range(heads):
        sl = slice(hh * HEAD_LANES, (hh + 1) * HEAD_LANES)
        qh = qa[:, sl]
        if rope:
            qh = qh * cos + qb[:, sl] * sin
        q_ref[:, sl] = (qh * scale).astype(BF16)
        k_ref[:, sl] = (kk[:, sl] + kr).astype(BF16)


def _rope_swap(w):
    nf = ROPE_NF
    return jnp.concatenate([-w[:, nf:2 * nf], w[:, :nf], -w[:, 3 * nf:4 * nf], w[:, 2 * nf:3 * nf]], axis=1)


def _ab_weights(w_in, w_q_b, w_kv_b, w_out, q_lora, kv_lora, hy_ch):
    d = w_in.shape[0]
    heads = w_q_b.shape[1] // (MLA_NOPE + MLA_ROPE)
    o = q_lora + kv_lora
    w_kr = w_in[:, o:o + MLA_ROPE]
    zpad = lambda n: jnp.zeros((d, n), F32)
    tail = HEAD_LANES - MLA_NOPE - MLA_ROPE
    win = jnp.concatenate(
        [w_in[:, :o],
         zpad(MLA_NOPE), w_kr, zpad(tail),
         zpad(MLA_NOPE), _rope_swap(w_kr), zpad(tail),
         w_in[:, o + MLA_ROPE:]], axis=1).astype(BF16)
    wq = w_q_b.reshape(q_lora, heads, MLA_NOPE + MLA_ROPE)
    zq = jnp.zeros((q_lora, heads, tail), F32)
    wqa = jnp.concatenate([wq, zq], axis=2).reshape(q_lora, heads * HEAD_LANES).astype(BF16)
    wq_rope = wq[:, :, MLA_NOPE:].reshape(q_lora * heads, MLA_ROPE)
    wqb = jnp.concatenate([jnp.zeros((q_lora, heads, MLA_NOPE), F32),
                           _rope_swap(wq_rope).reshape(q_lora, heads, MLA_ROPE), zq],
                          axis=2).reshape(q_lora, heads * HEAD_LANES).astype(BF16)
    wkv = w_kv_b.reshape(kv_lora, heads, MLA_NOPE + MLA_V)
    zk = jnp.zeros((kv_lora, heads, HEAD_LANES - MLA_NOPE), F32)
    wk = jnp.concatenate([wkv[:, :, :MLA_NOPE], zk], axis=2).reshape(kv_lora, heads * HEAD_LANES).astype(BF16)
    zv = jnp.zeros((kv_lora, heads, HEAD_LANES - MLA_V), F32)
    wv = jnp.concatenate([wkv[:, :, MLA_NOPE:], zv], axis=2).reshape(kv_lora, heads * HEAD_LANES).astype(BF16)
    wo = w_out[:heads * MLA_V].reshape(heads, MLA_V, d)
    wo_a = jnp.concatenate([wo, jnp.zeros((heads, HEAD_LANES - MLA_V, d), F32)],
                           axis=1).reshape(heads * HEAD_LANES, d).astype(BF16)
    wo_h = w_out[heads * MLA_V:].astype(BF16)
    return dict(win=win, wqa=wqa, wqb=wqb, wk=wk, wv=wv, wo_a=wo_a, wo_h=wo_h, heads=heads)


def _rope_tables(n):
    rows = n // GRID_W
    row = jnp.repeat(jnp.arange(rows), GRID_W).astype(F32)
    col = jnp.tile(jnp.arange(GRID_W), rows).astype(F32)
    inv = ROPE_BASE ** (-jnp.arange(ROPE_NF, dtype=F32) / ROPE_NF)
    ar = row[:, None] * inv
    ac = col[:, None] * inv
    ones = jnp.ones((n, MLA_NOPE), F32)
    tail = HEAD_LANES - MLA_NOPE - MLA_ROPE
    cos = jnp.concatenate([ones, jnp.cos(ar), jnp.cos(ar), jnp.cos(ac), jnp.cos(ac),
                           jnp.ones((n, tail), F32)], axis=1)
    sin = jnp.concatenate([0 * ones, jnp.sin(ar), jnp.sin(ar), jnp.sin(ac), jnp.sin(ac),
                           jnp.zeros((n, tail), F32)], axis=1)
    return cos, sin


def _ab_project(x, mods, row_fn, norm_g, w, q_norm_g, kv_norm_g, rope_tabs, hy_cols):
    b, n, d = x.shape
    tm = min(512, n)
    heads = w["heads"]
    q_lora = q_norm_g.shape[-1]
    kv_lora = kv_norm_g.shape[-1]
    hw = heads * HEAD_LANES
    rope = rope_tabs is not None
    tok = lambda width: pl.BlockSpec((None, tm, width), lambda bi, i: (bi, i, 0))
    in_specs = [tok(d), _mod_spec(row_fn, 0, d), _mod_spec(row_fn, 1, d), _resident((1, d)),
                _resident(w["win"].shape), _resident((1, q_lora)), _resident(w["wqa"].shape)]
    args = [x, mods, mods, norm_g.reshape(1, d), w["win"], q_norm_g.reshape(1, q_lora), w["wqa"]]
    if rope:
        in_specs.append(_resident(w["wqb"].shape))
        args.append(w["wqb"])
    in_specs += [_resident((1, kv_lora)), _resident(w["wk"].shape), _resident(w["wv"].shape)]
    args += [kv_norm_g.reshape(1, kv_lora), w["wk"], w["wv"]]
    if rope:
        in_specs += [pl.BlockSpec((tm, HEAD_LANES), lambda bi, i: (i, 0))] * 2
        args += list(rope_tabs)
    kern = functools.partial(_ab_proj_kernel, rope=rope, q_lora=q_lora, kv_lora=kv_lora, heads=heads,
                             scale=math.log2(math.e) / math.sqrt(MLA_NOPE + MLA_ROPE))
    return pl.pallas_call(
        kern,
        grid=(b, n // tm),
        in_specs=in_specs,
        out_specs=[tok(hw), tok(hw), tok(hw), tok(hy_cols)],
        out_shape=[jax.ShapeDtypeStruct((b, n, hw), BF16)] * 3 + [jax.ShapeDtypeStruct((b, n, hy_cols), F32)],
        compiler_params=_cparams("parallel", "parallel"),
        name="ab_project",
    )(*args)


def _attn_kernel(*refs, sub, nseg):
    q_ref, o_ref = refs[0], refs[-1]
    ks = [r[...] for r in refs[1:1 + nseg]]
    vs = [r[...] for r in refs[1 + nseg:1 + 2 * nseg]]
    for r in range(0, q_ref.shape[0], sub):
        q = q_ref[r:r + sub, :]
        ss = [_dot(q, k, _NT) for k in ks]
        m = functools.reduce(jnp.maximum, [jnp.max(s, axis=-1, keepdims=True) for s in ss])
        ps = [jnp.exp2(s - m) for s in ss]
        l = sum(jnp.sum(p, axis=-1, keepdims=True) for p in ps)
        o = sum(_dot(p.astype(BF16), v) for p, v in zip(ps, vs))
        o_ref[r:r + sub, :] = (o / l).astype(o_ref.dtype)


def _attention(q, ks, vs, heads):
    b, n, hw = q.shape
    tq = min(ATTN_TQ, n)
    tile = pl.BlockSpec((None, tq, HEAD_LANES), lambda bi, h, i: (bi, i, h))
    whole = lambda a: pl.BlockSpec((None, a.shape[1], HEAD_LANES), lambda bi, h, i: (bi, 0, h))
    return pl.pallas_call(
        functools.partial(_attn_kernel, sub=min(ATTN_SUB, tq), nseg=len(ks)),
        grid=(b, heads, n // tq),
        in_specs=[tile] + [whole(a) for a in ks] + [whole(a) for a in vs],
        out_specs=tile,
        out_shape=jax.ShapeDtypeStruct((b, n, hw), BF16),
        compiler_params=_cparams("parallel", "parallel", "parallel"),
        name="attention",
    )(q, *ks, *vs)


def _shift_rows(a, direction):
    n = a.shape[0]
    row = lax.broadcasted_iota(jnp.int32, a.shape, 0)
    if direction > 0:
        return jnp.where(row == 0, 0.0, pltpu.roll(a, 1, axis=0))
    return jnp.where(row == n - 1, 0.0, pltpu.roll(a, n - 1, axis=0))


def _dwconv_kernel(x_ref, w_ref, b_ref, o_ref, ob_ref):
    x = x_ref[...]
    w = w_ref[...]
    y = _shift_rows(x, 1) * w[0:1] + x * w[1:2] + _shift_rows(x, -1) * w[2:3] + b_ref[...]
    o_ref[...] = y
    ob_ref[...] = y.astype(BF16)


def _hy_dwconv(hy, conv_w, conv_b, ch, part_major=False):
    b, n, c3 = hy.shape
    tc = 256 if ch % 256 == 0 else 128
    assert ch % tc == 0
    blk = pl.BlockSpec((None, n, tc), lambda bi, j: (bi, 0, j))
    if part_major:
        per = ch // tc
        oblk = pl.BlockSpec((None, None, n, tc), lambda bi, j: (bi, j // per, 0, j % per))
        oshape = (b, c3 // ch, n, ch)
    else:
        oblk, oshape = blk, (b, n, c3)
    return pl.pallas_call(
        _dwconv_kernel,
        grid=(b, c3 // tc),
        in_specs=[blk, pl.BlockSpec((3, tc), lambda bi, j: (0, j)), pl.BlockSpec((1, tc), lambda bi, j: (0, j))],
        out_specs=[oblk, oblk],
        out_shape=[jax.ShapeDtypeStruct(oshape, F32), jax.ShapeDtypeStruct(oshape, BF16)],
        compiler_params=_cparams("parallel", "parallel"),
        name="hy_dwconv",
    )(hy, conv_w, conv_b.reshape(1, c3))


def _hy_filter_kernel(z_ref, t_ref, w1_ref, b1_ref, w2_ref, b2_ref, w3_ref, dl_ref, o_ref, *, ch, split):
    hdot = lambda a, w: jnp.dot(a, w, precision=HIGHEST, preferred_element_type=F32)
    a = jnp.sin(HY_SIN_FREQ * (hdot(z_ref[...], w1_ref[...]) + b1_ref[...]))
    a = jnp.sin(HY_SIN_FREQ * (hdot(a, w2_ref[...]) + b2_ref[...]))
    f = hdot(a, w3_ref[...])
    dec = jnp.exp(-t_ref[...] * dl_ref[...])
    tl = z_ref.shape[0]
    row = pl.program_id(0) * tl + lax.broadcasted_iota(jnp.int32, (tl, ch), 0)
    for o in range(HY_ORDER):
        hf = f[:, (2 * o) * ch:(2 * o + 1) * ch] * dec
        hb = jnp.where(row > 0, f[:, (2 * o + 1) * ch:(2 * o + 2) * ch] * dec, 0.0)
        if split:
            o_ref[2 * o] = hf
            o_ref[2 * o + 1] = hb
        else:
            o_ref[:, (2 * o) * ch:(2 * o + 1) * ch] = (hf + hb).astype(BF16)
            o_ref[:, (2 * o + 1) * ch:(2 * o + 2) * ch] = (hf - hb).astype(BF16)


def _hy_filter_inputs(length, w1, b1, w2, b2, w3, ch, split=False):
    t = jnp.linspace(0.0, 1.0, length, dtype=F32)[:, None]
    wv = 2.0 * math.pi * jnp.arange(length, dtype=F32)[:, None] / length
    f = jnp.linspace(1e-4, HY_BANDS - 1, HY_BANDS, dtype=F32)
    z = jnp.concatenate([t, jnp.cos(f * wv), -jnp.sin(f * wv)], axis=-1)
    emb = z.shape[1]
    z = jnp.pad(z, ((0, 0), (0, HEAD_LANES - emb)))
    w1p = jnp.pad(w1, ((0, HEAD_LANES - emb), (0, 0)))
    deltas = jnp.abs(jnp.linspace(HY_MIN_DECAY, HY_MAX_DECAY, ch, dtype=F32))[None, :]
    fw = w1.shape[1]
    tl = min(512, length)
    if split:
        out_spec = pl.BlockSpec((2 * HY_ORDER, tl, ch), lambda i: (0, i, 0))
        out_shape = jax.ShapeDtypeStruct((2 * HY_ORDER, length, ch), F32)
    else:
        out_spec = pl.BlockSpec((tl, 2 * HY_ORDER * ch), lambda i: (i, 0))
        out_shape = jax.ShapeDtypeStruct((length, 2 * HY_ORDER * ch), BF16)
    return pl.pallas_call(
        functools.partial(_hy_filter_kernel, ch=ch, split=split),
        grid=(length // tl,),
        in_specs=[pl.BlockSpec((tl, HEAD_LANES), lambda i: (i, 0)), pl.BlockSpec((tl, 1), lambda i: (i, 0)),
                  _resident((HEAD_LANES, fw)), _resident((1, fw)), _resident((fw, fw)), _resident((1, fw)),
                  _resident(w3.shape), _resident((1, ch))],
        out_specs=out_spec,
        out_shape=out_shape,
        compiler_params=_cparams("parallel"),
        name="hy_filter",
    )(z, t, w1p, b1.reshape(1, fw), w2, b2.reshape(1, fw), w3, deltas)


def _dft_tables(length):
    n2 = 2 * length
    theta = 2.0 * math.pi / n2
    n = jnp.arange(length, dtype=jnp.int32)[None, :]
    a = 8 * jnp.arange(length // 8, dtype=jnp.int32)[:, None]
    b = jnp.arange(8, dtype=jnp.int32)[:, None]
    ang_a = ((a * n) % n2).astype(F32) * theta
    ang_b = ((b * n) % n2).astype(F32) * theta
    ca, sa = jnp.cos(ang_a)[:, None, :], jnp.sin(ang_a)[:, None, :]
    cb, sb = jnp.cos(ang_b)[None, :, :], jnp.sin(ang_b)[None, :, :]
    fc = (ca * cb - sa * sb).reshape(length, length)
    fs = -(sa * cb + ca * sb).reshape(length, length)
    return fc.astype(BF16), fs.astype(BF16)


def _alternating_sum(x):
    col = lax.broadcasted_iota(jnp.int32, (8, x.shape[0]), 1)
    alt = (1 - 2 * (col & 1)).astype(F32).astype(BF16)
    return _dot(alt, x)[0:1]


def _filter_dft_kernel(hs_ref, hd_ref, fc_ref, fs_ref, o_ref):
    hs = hs_ref[...]
    o_ref[0] = _dot(fc_ref[...], hs)
    ki = _dot(fs_ref[...], hd_ref[...])
    nyq = _alternating_sum(hs)
    tf = ki.shape[0]
    row = pl.program_id(1) * tf + lax.broadcasted_iota(jnp.int32, (tf, 1), 0)
    o_ref[1] = jnp.where(row == 0, nyq, ki)


def _dft_spec_kernel(x_ref, fc_ref, fs_ref, k_ref, o_ref, *, inv_n):
    x = x_ref[...]
    ur = _dot(fc_ref[...], x)
    kr = k_ref[0]
    ki = k_ref[1]
    tf = ur.shape[0]
    row = pl.program_id(1) * tf + lax.broadcasted_iota(jnp.int32, (tf, 1), 0)
    first = row == 0
    ui = jnp.where(first, _alternating_sum(x), _dot(fs_ref[...], x))
    uiki = ui * ki
    yr = ur * kr - jnp.where(first, 0.0, uiki)
    yi = jnp.where(first, uiki, ur * ki + ui * kr)
    scale = jnp.where(first, inv_n, 2.0 * inv_n)
    o_ref[0] = (yr * scale).astype(BF16)
    o_ref[1] = (yi * scale).astype(BF16)


def _filter_spectrum(p, fc, fs, ch):
    length, cols = p.shape
    tf = min(512, length)
    return pl.pallas_call(
        _filter_dft_kernel,
        grid=(HY_ORDER, length // tf),
        in_specs=[pl.BlockSpec((length, ch), lambda o, i: (0, 2 * o)),
                  pl.BlockSpec((length, ch), lambda o, i: (0, 2 * o + 1)),
                  pl.BlockSpec((tf, length), lambda o, i: (i, 0)),
                  pl.BlockSpec((tf, length), lambda o, i: (i, 0))],
        out_specs=pl.BlockSpec((None, 2, tf, ch), lambda o, i: (o, 0, i, 0)),
        out_shape=jax.ShapeDtypeStruct((HY_ORDER, 2, length, ch), F32),
        compiler_params=_cparams("parallel", "arbitrary"),
        name="hy_filter_dft",
    )(p, p, fc, fs)


def _hy_spectral(xb, col, ch, fc, fs, kspec):
    b, length, _ = xb.shape
    tf = min(512, length)
    return pl.pallas_call(
        functools.partial(_dft_spec_kernel, inv_n=1.0 / (2 * length)),
        grid=(b, length // tf),
        in_specs=[pl.BlockSpec((None, length, ch), lambda bi, i: (bi, 0, col)),
                  pl.BlockSpec((tf, length), lambda bi, i: (i, 0)),
                  pl.BlockSpec((tf, length), lambda bi, i: (i, 0)),
                  pl.BlockSpec((2, tf, ch), lambda bi, i: (0, i, 0))],
        out_specs=pl.BlockSpec((None, 2, tf, ch), lambda bi, i: (bi, 0, i, 0)),
        out_shape=jax.ShapeDtypeStruct((b, 2, length, ch), BF16),
        compiler_params=_cparams("parallel", "arbitrary"),
        name="hy_dft_spectral",
    )(xb, fc, fs, kspec)


def _idft_gate_kernel(y_ref, fc_ref, fs_ref, u_ref, xg_ref, d_ref, o_ref, ob_ref):
    tt = fc_ref.shape[0]
    t = pl.program_id(1) * tt + lax.broadcasted_iota(jnp.int32, (tt, 1), 0)
    sign = (1 - 2 * (t & 1)).astype(F32)
    conv = (_dot(fc_ref[...], y_ref[0]) + _dot(fs_ref[...], y_ref[1])
            + sign * y_ref[1, 0:1, :].astype(F32))
    out = xg_ref[...] * (conv + u_ref[...] * d_ref[...])
    o_ref[...] = out
    ob_ref[...] = out.astype(BF16)


def _hy_inverse_gate(y, fc, fs, u, u_col, xg, xg_col, dvec, ch):
    b, _, length, _ = y.shape
    tt = min(512, length)
    out = pl.BlockSpec((None, tt, ch), lambda bi, i: (bi, i, 0))
    return pl.pallas_call(
        _idft_gate_kernel,
        grid=(b, length // tt),
        in_specs=[pl.BlockSpec((None, 2, length, ch), lambda bi, i: (bi, 0, 0, 0)),
                  pl.BlockSpec((tt, length), lambda bi, i: (i, 0)),
                  pl.BlockSpec((tt, length), lambda bi, i: (i, 0)),
                  pl.BlockSpec((None, tt, ch), lambda bi, i: (bi, i, u_col)),
                  pl.BlockSpec((None, tt, ch), lambda bi, i: (bi, i, xg_col)),
                  pl.BlockSpec((1, ch), lambda bi, i: (0, 0))],
        out_specs=[out, out],
        out_shape=[jax.ShapeDtypeStruct((b, length, ch), F32), jax.ShapeDtypeStruct((b, length, ch), BF16)],
        compiler_params=_cparams("parallel", "arbitrary"),
        name="hy_idft_gate",
    )(y, fc, fs, u, xg, dvec.reshape(1, ch))


def _fft_tables(length):
    n = 2 * length
    n1 = FFT_N1
    n2 = n // n1
    h1 = n1 // 2
    k1 = jnp.arange(n1, dtype=jnp.int32)
    m1 = jnp.arange(h1, dtype=jnp.int32)
    a_f = ((k1[:, None] * m1[None, :]) % n1).astype(F32) * (2.0 * math.pi / n1)
    a_g = ((m1[:, None] * k1[None, :]) % n1).astype(F32) * (2.0 * math.pi / n1)
    f1 = jnp.concatenate([jnp.cos(a_f), -jnp.sin(a_f)], axis=0).astype(BF16)
    g1 = jnp.concatenate([jnp.cos(a_g), -jnp.sin(a_g)], axis=1).astype(BF16)
    k2 = jnp.arange(n2, dtype=jnp.int32)
    a2 = ((k2[:, None] * k2[None, :]) % n2).astype(F32) * (2.0 * math.pi / n2)
    c2, s2 = jnp.cos(a2), jnp.sin(a2)
    f2 = jnp.concatenate([jnp.concatenate([c2, s2], axis=1), jnp.concatenate([-s2, c2], axis=1)], axis=0)
    g2 = jnp.concatenate([jnp.concatenate([c2, -s2], axis=1), jnp.concatenate([s2, c2], axis=1)], axis=0)
    at = (k1[:, None] * k2[None, :]).astype(F32) * (2.0 * math.pi / n)
    twc = jnp.broadcast_to(jnp.cos(at)[:, :, None], (n1, n2, HEAD_LANES))
    tws = jnp.broadcast_to(jnp.sin(at)[:, :, None], (n1, n2, HEAD_LANES))
    return dict(f1=f1, g1=g1, f2=f2.astype(BF16), g2=g2.astype(BF16), twc=twc, tws=tws, n1=n1, n2=n2)


def _fft_s1_kernel(x_ref, f1_ref, o_ref):
    o_ref[...] = _dot(f1_ref[...], x_ref[...].astype(BF16)).astype(BF16)


def _fft_stage1(x4, part, tb):
    g, _, h1, cols = x4.shape
    tn = min(FFT_COLS, cols)
    return pl.pallas_call(
        _fft_s1_kernel,
        grid=(g, cols // tn),
        in_specs=[pl.BlockSpec((None, None, h1, tn), lambda gi, j: (gi, part, 0, j)), _resident(tb["f1"].shape)],
        out_specs=pl.BlockSpec((None, 2 * tb["n1"], tn), lambda gi, j: (gi, 0, j)),
        out_shape=jax.ShapeDtypeStruct((g, 2 * tb["n1"], cols), BF16),
        compiler_params=_cparams("parallel", "parallel"),
        name="hy_fft_stage1",
    )(x4, tb["f1"])


def _twiddled(a_ref, twc_ref, tws_ref, j):
    ar = a_ref[0, j].astype(F32)
    ai = a_ref[1, j].astype(F32)
    rep = ar.shape[-1] // HEAD_LANES
    c = jnp.tile(twc_ref[j], (1, rep))
    s = jnp.tile(tws_ref[j], (1, rep))
    return jnp.concatenate([ar * c + ai * s, ai * c - ar * s], axis=0).astype(BF16), c, s


def _fft_filter_kernel(a_ref, twc_ref, tws_ref, f2_ref, o_ref):
    n2 = a_ref.shape[2]
    for j in range(a_ref.shape[1]):
        z, _, _ = _twiddled(a_ref, twc_ref, tws_ref, j)
        x = _dot(f2_ref[...], z)
        o_ref[0, j] = x[:n2]
        o_ref[1, j] = x[n2:]


def _fft_filter_spectrum(a5, tb):
    g, _, n1, n2, ch = a5.shape
    kb = FFT_KB
    blk = pl.BlockSpec((None, 2, kb, n2, ch), lambda gi, k: (gi, 0, k, 0, 0))
    tw = pl.BlockSpec((kb, n2, HEAD_LANES), lambda gi, k: (k, 0, 0))
    return pl.pallas_call(
        _fft_filter_kernel,
        grid=(g, n1 // kb),
        in_specs=[blk, tw, tw, _resident(tb["f2"].shape)],
        out_specs=blk,
        out_shape=jax.ShapeDtypeStruct(a5.shape, F32),
        compiler_params=_cparams("parallel", "parallel"),
        name="hy_fft_filter",
    )(a5, tb["twc"], tb["tws"], tb["f2"])


def _fft_middle_kernel(a_ref, hf_ref, hb_ref, twc_ref, tws_ref, f2_ref, g2_ref, o_ref):
    n2 = a_ref.shape[2]
    for j in range(a_ref.shape[1]):
        z, c, s = _twiddled(a_ref, twc_ref, tws_ref, j)
        x = _dot(f2_ref[...], z)
        xr, xi = x[:n2], x[n2:]
        kr = hf_ref[0, j] + hb_ref[0, j]
        ki = hf_ref[1, j] - hb_ref[1, j]
        y = jnp.concatenate([xr * kr - xi * ki, xr * ki + xi * kr], axis=0).astype(BF16)
        w = _dot(g2_ref[...], y)
        wr, wi = w[:n2], w[n2:]
        o_ref[0, j] = (wr * c - wi * s).astype(BF16)
        o_ref[1, j] = (wi * c + wr * s).astype(BF16)


def _fft_middle(a5, spec, order, tb):
    b, _, n1, n2, ch = a5.shape
    kb = FFT_KB
    blk = pl.BlockSpec((None, 2, kb, n2, ch), lambda k, bi: (bi, 0, k, 0, 0))
    sp = lambda which: pl.BlockSpec((None, 2, kb, n2, ch), lambda k, bi: (2 * order + which, 0, k, 0, 0))
    tw = pl.BlockSpec((kb, n2, HEAD_LANES), lambda k, bi: (k, 0, 0))
    return pl.pallas_call(
        _fft_middle_kernel,
        grid=(n1 // kb, b),
        in_specs=[blk, sp(0), sp(1), tw, tw, _resident(tb["f2"].shape), _resident(tb["g2"].shape)],
        out_specs=blk,
        out_shape=jax.ShapeDtypeStruct(a5.shape, BF16),
        compiler_params=_cparams("parallel", "arbitrary"),
        name="hy_fft_middle",
    )(a5, spec, spec, tb["twc"], tb["tws"], tb["f2"], tb["g2"])


def _fft_s3_gate_kernel(w_ref, g1_ref, u_ref, xg_ref, d_ref, o_ref, ob_ref, *, inv_n):
    conv = _dot(g1_ref[...], w_ref[...]) * inv_n
    out = xg_ref[...] * (conv + u_ref[...] * d_ref[...])
    o_ref[...] = out
    ob_ref[...] = out.astype(BF16)


def _fft_stage3_gate(w3, tb, u4, u_part, xg4, xg_part, dvec, length):
    b, rows, cols = w3.shape
    h1 = rows // 4
    tn = min(FFT_COLS, cols)
    ch = dvec.shape[-1]
    part = lambda p: pl.BlockSpec((None, None, h1, tn), lambda bi, j: (bi, p, 0, j))
    out = pl.BlockSpec((None, h1, tn), lambda bi, j: (bi, 0, j))
    return pl.pallas_call(
        functools.partial(_fft_s3_gate_kernel, inv_n=1.0 / (2 * length)),
        grid=(b, cols // tn),
        in_specs=[pl.BlockSpec((None, rows, tn), lambda bi, j: (bi, 0, j)), _resident(tb["g1"].shape),
                  part(u_part), part(xg_part), _resident((1, tn))],
        out_specs=[out, out],
        out_shape=[jax.ShapeDtypeStruct((b, h1, cols), F32), jax.ShapeDtypeStruct((b, h1, cols), BF16)],
        compiler_params=_cparams("parallel", "parallel"),
        name="hy_fft_stage3_gate",
    )(w3, tb["g1"], u4, xg4, jnp.tile(dvec.reshape(1, ch), (1, tn // ch)))


def _hyena_fft(hy, conv_w, conv_b, w1, b1, w2, b2, w3, hy_bias):
    ch = hy_bias.shape[-1]
    b, length, _ = hy.shape
    tb = _fft_tables(length)
    n1, n2 = tb["n1"], tb["n2"]
    h1, cols = n1 // 2, n2 * ch
    hyc, hyb = _hy_dwconv(hy, conv_w, conv_b, ch, part_major=True)
    hyc = hyc.reshape(b, HY_ORDER + 1, h1, cols)
    hyb = hyb.reshape(b, HY_ORDER + 1, h1, cols)
    filt = _hy_filter_inputs(length, w1, b1, w2, b2, w3, ch, split=True)
    a = _fft_stage1(filt.reshape(2 * HY_ORDER, 1, h1, cols), 0, tb)
    spec = _fft_filter_spectrum(a.reshape(2 * HY_ORDER, 2, n1, n2, ch), tb)

    def conv(x4, part, order, u4, u_part, xg_part):
        a = _fft_stage1(x4, part, tb).reshape(b, 2, n1, n2, ch)
        w = _fft_middle(a, spec, order, tb).reshape(b, 2 * n1, cols)
        return _fft_stage3_gate(w, tb, u4, u_part, hyc, xg_part, hy_bias[order], length)

    y1, y1b = conv(hyb, 0, 0, hyc, 0, 1)
    y1 = y1.reshape(b, 1, h1, cols)
    y2, _ = conv(y1b.reshape(b, 1, h1, cols), 0, 1, y1, 0, 2)
    return y2.reshape(b, length, ch)


def _hyena(hy, conv_w, conv_b, w1, b1, w2, b2, w3, hy_bias):
    ch = hy_bias.shape[-1]
    length = hy.shape[1]
    if length >= FFT_MIN_LEN and (2 * length) % FFT_N1 == 0 and ((2 * length) // FFT_N1) % 8 == 0:
        return _hyena_fft(hy, conv_w, conv_b, w1, b1, w2, b2, w3, hy_bias)
    hyc, hyb = _hy_dwconv(hy, conv_w, conv_b, ch)
    fc, fs = _dft_tables(length)
    kspec = _filter_spectrum(_hy_filter_inputs(length, w1, b1, w2, b2, w3, ch), fc, fs, ch)
    y = _hy_spectral(hyb, 0, ch, fc, fs, kspec[0])
    y1, y1b = _hy_inverse_gate(y, fc, fs, hyc, 0, hyc, 1, hy_bias[0], ch)
    y = _hy_spectral(y1b, 0, ch, fc, fs, kspec[1])
    y2, _ = _hy_inverse_gate(y, fc, fs, y1, 0, hyc, 2, hy_bias[1], ch)
    return y2


def _ab_out_kernel(o_ref, y_ref, x_ref, g1_ref, woa_ref, woh_ref, out_ref):
    y = _dot(o_ref[...], woa_ref[...]) + _dot(y_ref[...].astype(BF16), woh_ref[...])
    out_ref[...] = x_ref[...] + g1_ref[...] * y


def _ab_out(o, y2, x, mods, row_fn, w):
    b, n, d = x.shape
    tm = min(512, n)
    tok = lambda width: pl.BlockSpec((None, tm, width), lambda bi, i: (bi, i, 0))
    return pl.pallas_call(
        _ab_out_kernel,
        grid=(b, n // tm),
        in_specs=[tok(o.shape[-1]), tok(y2.shape[-1]), tok(d), _mod_spec(row_fn, 2, d),
                  _resident(w["wo_a"].shape), _resident(w["wo_h"].shape)],
        out_specs=tok(d),
        out_shape=jax.ShapeDtypeStruct((b, n, d), F32),
        compiler_params=_cparams("parallel", "parallel"),
        name="ab_out",
    )(o, y2, x, mods, w["wo_a"], w["wo_h"])


HALO = 8
FFN_ROWS = 512
FFN_CHUNK_MAX = 1408


def _ffn_kernel(*refs, hidden, hc, final):
    if final:
        (x_ref, xp_ref, xn_ref, sh_ref, sc_ref, g2_ref, ng_ref, wup_ref, cw_ref, cb_ref, wdn_ref,
         fg_ref, o_ref) = refs
    else:
        (x_ref, xp_ref, xn_ref, sh_ref, sc_ref, g2_ref, ng_ref, wup_ref, cw_ref, cb_ref, wdn_ref,
         o_ref) = refs
    i = pl.program_id(1)
    last = pl.num_programs(1) - 1
    x = x_ref[...]
    tm, d = x.shape
    mod = lambda a: _modulate(a, ng_ref[...], sh_ref[...], sc_ref[...])
    hp = mod(xp_ref[...]) * (i > 0).astype(F32)
    hn = mod(xn_ref[...]) * (i < last).astype(F32)
    he = jnp.concatenate([hp, mod(x), hn], axis=0).astype(BF16)
    ne = tm + 2 * HALO

    def conv(a, col):
        w = cw_ref[:, col:col + hc]
        prev = pltpu.roll(a, 1, axis=0)[HALO:HALO + tm]
        nxt = pltpu.roll(a, ne - 1, axis=0)[HALO:HALO + tm]
        return prev * w[0:1] + a[HALO:HALO + tm] * w[1:2] + nxt * w[2:3] + cb_ref[:, col:col + hc]

    acc = jnp.zeros((tm, d), F32)
    for j in range(hidden // hc):
        gate = conv(_dot(he, wup_ref[:, j * hc:(j + 1) * hc]), j * hc)
        val = conv(_dot(he, wup_ref[:, hidden + j * hc:hidden + (j + 1) * hc]), hidden + j * hc)
        mid = (_silu(gate) * val).astype(BF16)
        acc = acc + _dot(mid, wdn_ref[j * hc:(j + 1) * hc, :])
    out = x + g2_ref[...] * acc
    if final:
        out = _rms(out, fg_ref[...])
    o_ref[...] = out


def _ffn_chunk(hidden):
    for hc in range(min(FFN_CHUNK_MAX, hidden) // HEAD_LANES * HEAD_LANES, 0, -HEAD_LANES):
        if hidden % hc == 0:
            return hc
    raise ValueError(f"FFN hidden width {hidden} is not a multiple of {HEAD_LANES}")


def _conv_ffn(x, mods, row_fn, norm_g, w_up, conv_w, conv_b, w_down, final_g=None):
    b, n, d = x.shape
    hidden = w_down.shape[0]
    tm = min(FFN_ROWS, n)
    nh = n // HALO
    tpb = tm // HALO
    final = final_g is not None
    tok = pl.BlockSpec((None, tm, d), lambda bi, i: (bi, i, 0))
    in_specs = [tok,
                pl.BlockSpec((None, HALO, d), lambda bi, i: (bi, jnp.maximum(i * tpb - 1, 0), 0)),
                pl.BlockSpec((None, HALO, d), lambda bi, i: (bi, jnp.minimum((i + 1) * tpb, nh - 1), 0)),
                _mod_spec(row_fn, 3, d), _mod_spec(row_fn, 4, d), _mod_spec(row_fn, 5, d),
                _resident((1, d)), _resident(w_up.shape), _resident(conv_w.shape),
                _resident((1, 2 * hidden)), _resident(w_down.shape)]
    args = [x, x, x, mods, mods, mods, norm_g.reshape(1, d), w_up, conv_w, conv_b.reshape(1, 2 * hidden), w_down]
    if final:
        in_specs.append(_resident((1, d)))
        args.append(final_g.reshape(1, d))
    return pl.pallas_call(
        functools.partial(_ffn_kernel, hidden=hidden, hc=_ffn_chunk(hidden), final=final),
        grid=(b, n // tm),
        in_specs=in_specs,
        out_specs=tok,
        out_shape=jax.ShapeDtypeStruct((b, n, d), F32),
        compiler_params=_cparams("parallel", "parallel"),
        name="conv_ffn",
    )(*args)


def _hg_proj_kernel(x_ref, sh_ref, sc_ref, g_ref, w_ref, o_ref, *, width):
    h = _modulate(x_ref[...], g_ref[...], sh_ref[...], sc_ref[...]).astype(BF16)
    z = _dot(h, w_ref[...])
    o_ref[:, :width] = _silu(z[:, :width])
    o_ref[:, width:] = z[:, width:]


def _hg_project(x, mods, row_fn, norm_g, w_in):
    b, n, d = x.shape
    cols = w_in.shape[1]
    tm = min(512, n)
    tok = lambda width: pl.BlockSpec((None, tm, width), lambda bi, i: (bi, i, 0))
    return pl.pallas_call(
        functools.partial(_hg_proj_kernel, width=cols // 5),
        grid=(b, n // tm),
        in_specs=[tok(d), _mod_spec(row_fn, 0, d), _mod_spec(row_fn, 1, d), _resident((1, d)),
                  _resident(w_in.shape)],
        out_specs=tok(cols),
        out_shape=jax.ShapeDtypeStruct((b, n, cols), F32),
        compiler_params=_cparams("parallel", "parallel"),
        name="hg_project",
    )(x, mods, mods, norm_g.reshape(1, d), w_in)


def _scan_tables(direction):
    c = HG_CHUNK
    idx = np.arange(c)
    ops = []
    level = np.full((c, c), -1, np.int32)
    for li, m in enumerate(HG_LEVELS):
        op = np.zeros((c, c), np.float32)
        for r in range(c):
            lo = (r // m) * m
            hi = lo + m - 1
            upper = (r % (2 * m)) >= m
            if direction == 0:
                if upper:
                    op[r, lo:r + 1] = 1
                else:
                    op[r, r + 1:hi + 1] = 1
            else:
                if not upper:
                    op[r, r:hi + 1] = 1
                else:
                    op[r, lo:r] = 1
        ops.append(op)
        same = (idx[:, None] // (2 * m)) == (idx[None, :] // (2 * m))
        up = (idx % (2 * m)) >= m
        pair = same & (up[:, None] & ~up[None, :] if direction == 0 else ~up[:, None] & up[None, :])
        level[pair] = li
    level[idx, idx] = len(HG_LEVELS)
    if direction == 0:
        to_q = (idx[None, :] <= idx[:, None])
    else:
        to_q = (idx[None, :] >= idx[:, None])
    ops += [to_q.astype(np.float32), np.ones((8, c), np.float32)]
    em = np.concatenate(ops, axis=0)
    return jnp.asarray(np.concatenate([em, em], axis=1), BF16), jnp.asarray(level)


def _gla_group(chunks, lb, em2, lv, st):
    c = HG_CHUNK
    e_ = HG_EXPAND
    nl = len(HG_LEVELS)
    log_lb = jnp.log(lb)
    log_1m = jnp.log1p(-lb)
    log_f = []
    keys = []
    for _, fr, _ in chunks:
        t = jnp.exp(-jnp.abs(fr))
        d = 1.0 + t
        keys.append((1.0 - lb) * (jnp.where(fr > 0, t, 1.0) / d))
        b = log_1m + (jnp.minimum(fr, 0.0) - jnp.log(d))
        log_f.append(jnp.maximum(log_lb, b) + jnp.log(1.0 + jnp.exp(-jnp.abs(log_lb - b))))
    g = jnp.concatenate(log_f, axis=1) * math.log2(math.e)
    hi = g.astype(BF16)
    lo = (g - hi.astype(F32)).astype(BF16)
    e_all = _dot(em2, jnp.concatenate([hi, lo], axis=0))
    outs = []
    for gi, (q, _, val) in enumerate(chunks):
        e = e_all[:, gi * e_:(gi + 1) * e_]
        key = keys[gi]
        att = jnp.zeros((c, c), F32)
        for li in range(nl):
            dec = jnp.exp2(e[li * c:(li + 1) * c])
            att = jnp.where(lv == li, _dot((q * dec).astype(BF16), (key * dec).astype(BF16), _NT), att)
        att = jnp.where(lv == nl, jnp.sum(q * key, axis=1, keepdims=True), att)
        vb = val.astype(BF16)
        to_q = e[nl * c:(nl + 1) * c]
        total = e[(nl + 1) * c:(nl + 1) * c + 1]
        q_in = (q * jnp.exp2(to_q)).astype(BF16)
        outs.append(_dot(att.astype(BF16), vb) + _dot(q_in, st.astype(BF16), _NT))
        k_out = (key * jnp.exp2(total - to_q)).astype(BF16)
        st = st * jnp.exp2(total) + _dot(vb, k_out, _TN)
    return outs, st


def _hg_scan_kernel(*refs, layer, need_ctx):
    (lg_ref, emf_ref, emb_ref, lvf_ref, lvb_ref,
     qc_ref, ffc_ref, fbc_ref, ic_ref, q_ref, ff_ref, fb_ref, i_ref) = refs[:13]
    if need_ctx:
        o_ref, oc_ref, st_ref = refs[13:]
    else:
        o_ref, st_ref = refs[13:]
        oc_ref = None
    c = HG_CHUNK
    lg = lg_ref[...]
    ex = jnp.exp(lg - jnp.max(lg, axis=0, keepdims=True))
    probs = ex / jnp.sum(ex, axis=0, keepdims=True)
    acc = probs[0]
    for l in range(1, layer + 1):
        acc = acc + probs[l]
    lb_all = acc - probs[0]
    lb_f = lb_all[0:1]
    lb_b = lb_all[1:2]

    st_ref[...] = jnp.zeros_like(st_ref)
    o_ref[...] = jnp.zeros_like(o_ref)
    if need_ctx:
        oc_ref[...] = jnp.zeros_like(oc_ref)

    def run(qr, ffr, fbr, ir, outr):
        nc = qr.shape[0] // c
        ng = nc // HG_GROUP
        assert ng * HG_GROUP == nc

        def body(gi, carry):
            rows_f = [pl.multiple_of((gi * HG_GROUP + k) * c, c) for k in range(HG_GROUP)]
            rows_b = [pl.multiple_of((nc - 1 - gi * HG_GROUP - k) * c, c) for k in range(HG_GROUP)]
            load = lambda gate_ref, r: (qr[pl.ds(r, c), :], gate_ref[pl.ds(r, c), :], ir[pl.ds(r, c), :])
            of, stf = _gla_group([load(ffr, r) for r in rows_f], lb_f, emf_ref[...], lvf_ref[...], st_ref[0])
            st_ref[0] = stf
            ob, stb = _gla_group([load(fbr, r) for r in rows_b], lb_b, emb_ref[...], lvb_ref[...], st_ref[1])
            st_ref[1] = stb
            if outr is not None:
                for r, o in zip(rows_f + rows_b, of + ob):
                    outr[pl.ds(r, c), :] += o
            return carry

        lax.fori_loop(0, ng, body, 0, unroll=min(HG_UNROLL, ng))

    run(qc_ref, ffc_ref, fbc_ref, ic_ref, oc_ref)
    run(q_ref, ff_ref, fb_ref, i_ref, o_ref)


def _hg_scan(z, zc, lb_logits, layer, need_ctx):
    b, n, cols = z.shape
    nc_ = zc.shape[1]
    width = cols // 5
    heads = width // HG_EXPAND
    e = HG_EXPAND
    depth = lb_logits.shape[0]
    emf, lvf = _scan_tables(0)
    emb, lvb = _scan_tables(1)
    col = lambda rows, group: pl.BlockSpec((None, rows, e), lambda bi, h: (bi, 0, group * heads + h))
    in_specs = [pl.BlockSpec((depth, 2, e), lambda bi, h: (0, 0, h)),
                _resident(emf.shape), _resident(emb.shape), _resident(lvf.shape), _resident(lvb.shape),
                col(nc_, 0), col(nc_, 1), col(nc_, 2), col(nc_, 3),
                col(n, 0), col(n, 1), col(n, 2), col(n, 3)]
    out_specs = [pl.BlockSpec((None, n, e), lambda bi, h: (bi, 0, h))]
    out_shape = [jax.ShapeDtypeStruct((b, n, width), F32)]
    if need_ctx:
        out_specs.append(pl.BlockSpec((None, nc_, e), lambda bi, h: (bi, 0, h)))
        out_shape.append(jax.ShapeDtypeStruct((b, nc_, width), F32))
    res = pl.pallas_call(
        functools.partial(_hg_scan_kernel, layer=layer, need_ctx=need_ctx),
        grid=(b, heads),
        in_specs=in_specs,
        out_specs=out_specs,
        out_shape=out_shape,
        scratch_shapes=[pltpu.VMEM((2, e, e), F32)],
        compiler_params=_cparams("parallel", "parallel"),
        name="hg_scan",
    )(lb_logits, emf, emb, lvf, lvb, zc, zc, zc, zc, z, z, z, z)
    return (res[0], res[1]) if need_ctx else (res[0], None)


def _hg_out_kernel(o_ref, g_ref, x_ref, g1_ref, ng_ref, w_ref, out_ref, *, heads):
    ng = ng_ref[...]
    parts = [_rms(o_ref[:, h * HG_EXPAND:(h + 1) * HG_EXPAND], ng) for h in range(heads)]
    o = jnp.concatenate(parts, axis=1) * _silu(g_ref[...])
    out_ref[...] = x_ref[...] + g1_ref[...] * _dot(o.astype(BF16), w_ref[...])


def _hg_out(o, z, x, mods, row_fn, norm_g, w_out):
    b, n, d = x.shape
    width = o.shape[-1]
    tm = min(512, n)
    tok = lambda wd: pl.BlockSpec((None, tm, wd), lambda bi, i: (bi, i, 0))
    return pl.pallas_call(
        functools.partial(_hg_out_kernel, heads=width // HG_EXPAND),
        grid=(b, n // tm),
        in_specs=[tok(width), pl.BlockSpec((None, tm, width), lambda bi, i: (bi, i, 4)), tok(d),
                  _mod_spec(row_fn, 2, d), _resident((1, HG_EXPAND)), _resident(w_out.shape)],
        out_specs=tok(d),
        out_shape=jax.ShapeDtypeStruct((b, n, d), F32),
        compiler_params=_cparams("parallel", "parallel"),
        name="hg_out",
    )(o, z, x, mods, norm_g.reshape(1, HG_EXPAND), w_out)


def kernel(x, c, ctx, c_ctx, mod_w, mod_b, norm1_g, norm2_g, ffn_w_up, ffn_conv_w, ffn_conv_b, ffn_w_down, ab_w_in, mla_q_norm_g, mla_w_q_b, mla_kv_norm_g, mla_w_kv_b, hy_conv_w, hy_conv_b, hy_w1, hy_b1, hy_w2, hy_b2, hy_w3, hy_bias, ab_w_out, hg_w_in, hg_lb_logits, hg_norm_g, hg_w_out, final_norm_g):
    batch, n, d = x.shape
    depth = mod_w.shape[0]
    assert batch < MOD_ROWS
    ctx_row = batch
    cc = jnp.zeros((MOD_ROWS, d), F32).at[:batch].set(c).at[ctx_row].set(c_ctx)
    mods_all = _mod_vectors(cc, mod_w, mod_b)
    lat_row = lambda bi, *_: bi
    ctx_row_fn = lambda bi, *_: ctx_row
    rope_tabs = _rope_tables(n)

    xc = ctx
    for layer in range(depth):
        need_ctx = layer < depth - 1
        j = layer // 2
        mods = mods_all[layer]
        if layer % 2 == 0:
            q_lora = mla_q_norm_g.shape[-1]
            kv_lora = mla_kv_norm_g.shape[-1]
            hy_ch = hy_bias.shape[-1]
            w = _ab_weights(ab_w_in[j], mla_w_q_b[j], mla_w_kv_b[j], ab_w_out[j], q_lora, kv_lora, hy_ch)
            hy_cols = (HY_ORDER + 1) * hy_ch
            hy_args = (hy_conv_w[j], hy_conv_b[j], hy_w1[j], hy_b1[j], hy_w2[j], hy_b2[j], hy_w3[j], hy_bias[j])
            q_c, k_c, v_c, hyin_c = _ab_project(xc, mods, ctx_row_fn, norm1_g[layer], w, mla_q_norm_g[j],
                                                mla_kv_norm_g[j], None, hy_cols)
            q, k, v, hyin = _ab_project(x, mods, lat_row, norm1_g[layer], w, mla_q_norm_g[j],
                                        mla_kv_norm_g[j], rope_tabs, hy_cols)
            o = _attention(q, [jnp.concatenate([k_c, k], axis=1)], [jnp.concatenate([v_c, v], axis=1)], w["heads"])
            x_new = _ab_out(o, _hyena(hyin, *hy_args), x, mods, lat_row, w)
            if need_ctx:
                o_c = _attention(q_c, [k_c], [v_c], w["heads"])
                xc = _ab_out(o_c, _hyena(hyin_c, *hy_args), xc, mods, ctx_row_fn, w)
            x = x_new
        else:
            w_in = hg_w_in[j].astype(BF16)
            z_c = _hg_project(xc, mods, ctx_row_fn, norm1_g[layer], w_in)
            z = _hg_project(x, mods, lat_row, norm1_g[layer], w_in)
            o, o_c = _hg_scan(z, z_c, hg_lb_logits, layer, need_ctx)
            w_out = hg_w_out[j].astype(BF16)
            x = _hg_out(o, z, x, mods, lat_row, hg_norm_g[j], w_out)
            if need_ctx:
                xc = _hg_out(o_c, z_c, xc, mods, ctx_row_fn, hg_norm_g[j], w_out)
        w_up = ffn_w_up[layer].astype(BF16)
        w_dn = ffn_w_down[layer].astype(BF16)
        last = layer == depth - 1
        x = _conv_ffn(x, mods, lat_row, norm2_g[layer], w_up, ffn_conv_w[layer], ffn_conv_b[layer], w_dn,
                      final_norm_g if last else None)
        if need_ctx:
            xc = _conv_ffn(xc, mods, ctx_row_fn, norm2_g[layer], w_up, ffn_conv_w[layer], ffn_conv_b[layer], w_dn)
    return x
```

```python
import functools
import math

import numpy as np
import jax
import jax.numpy as jnp
from jax import lax
from jax.experimental import pallas as pl
from jax.experimental.pallas import tpu as pltpu

F32 = jnp.float32
BF16 = jnp.bfloat16
HIGHEST = lax.Precision.HIGHEST

EPS = 1e-6
GRID_W = 64

MLA_NOPE = 64
MLA_ROPE = 32
MLA_V = 64
ROPE_NF = MLA_ROPE // 4
ROPE_BASE = 10000.0
HEAD_LANES = 128
ATTN_TQ = 1024
ATTN_SUB = 256

HY_ORDER = 2
HY_BANDS = 16
HY_SIN_FREQ = 1.0
HY_DECAY_TARGET = 1e-2
HY_FAST_DECAY = 0.3
HY_SLOW_DECAY = 1.5
HY_MIN_DECAY = math.log(HY_DECAY_TARGET) / HY_SLOW_DECAY
HY_MAX_DECAY = math.log(HY_DECAY_TARGET) / HY_FAST_DECAY

HG_EXPAND = 128
HG_CHUNK = 64
HG_LEVELS = (32, 16, 8, 4, 2, 1)
HG_GROUP = 2
HG_UNROLL = 4

ROW_TILE = 512
MOD_ROWS = 16
HALO = 8
VMEM_LIMIT = 56 * 1024 * 1024

_NT = (((1,), (1,)), ((), ()))
_TN = (((0,), (0,)), ((), ()))


def _cparams(*sem):
    return pltpu.CompilerParams(dimension_semantics=sem, vmem_limit_bytes=VMEM_LIMIT)


def _dot(a, b, dims=(((1,), (0,)), ((), ()))):
    return lax.dot_general(a, b, dims, precision=lax.Precision.DEFAULT, preferred_element_type=F32)


def _rms(x, g):
    return x * lax.rsqrt(jnp.mean(x * x, axis=-1, keepdims=True) + EPS) * g


def _modulate(x, g, shift, scale):
    return _rms(x, g) * (1.0 + scale) + shift


def _silu(x):
    return x * jax.nn.sigmoid(x)


def _mod_spec(row_fn, k, d):
    return pl.BlockSpec((None, None, 1, d), lambda *idx: (row_fn(*idx), k, 0, 0))


def _resident(shape):
    nd = len(shape)
    return pl.BlockSpec(shape, lambda *idx: (0,) * nd, pipeline_mode=pl.Buffered(1))


def _mod_kernel(c_ref, w_ref, b_ref, o_ref):
    a = _silu(c_ref[...])
    o_ref[...] = jnp.dot(a, w_ref[...], precision=HIGHEST, preferred_element_type=F32) + b_ref[...]


def _mod_vectors(cc, mod_w, mod_b):
    depth, d, n6 = mod_w.shape
    tn = 512
    out = pl.pallas_call(
        _mod_kernel,
        grid=(depth, n6 // tn),
        in_specs=[pl.BlockSpec((MOD_ROWS, d), lambda l, j: (0, 0)),
                  pl.BlockSpec((None, d, tn), lambda l, j: (l, 0, j)),
                  pl.BlockSpec((None, 1, tn), lambda l, j: (l, 0, j))],
        out_specs=pl.BlockSpec((None, MOD_ROWS, tn), lambda l, j: (l, 0, j)),
        out_shape=jax.ShapeDtypeStruct((depth, MOD_ROWS, n6), F32),
        compiler_params=_cparams("parallel", "parallel"),
        name="mod_vectors",
    )(cc, mod_w, mod_b.reshape(depth, 1, n6))
    return out.reshape(depth, MOD_ROWS, 6, 1, d)


def _halo_rows(x_ref, xp_ref, xn_ref, mod):
    i = pl.program_id(1)
    last = pl.num_programs(1) - 1
    hp = mod(xp_ref[...]) * (i > 0).astype(F32)
    hn = mod(xn_ref[...]) * (i < last).astype(F32)
    return jnp.concatenate([hp, mod(x_ref[...]), hn], axis=0).astype(BF16)


def _conv3_rows(a, w, bias, tm):
    ne = tm + 2 * HALO
    prev = pltpu.roll(a, 1, axis=0)[HALO:HALO + tm]
    nxt = pltpu.roll(a, ne - 1, axis=0)[HALO:HALO + tm]
    return prev * w[0:1] + a[HALO:HALO + tm] * w[1:2] + nxt * w[2:3] + bias


def _ab_proj_kernel(*refs, rope, q_lora, kv_lora, heads, scale, ch):
    if rope:
        (x_ref, xp_ref, xn_ref, sh_ref, sc_ref, g_ref, win_ref, qg_ref, wqa_ref, wqb_ref, kvg_ref, wk_ref,
         wv_ref, cw_ref, cb_ref, cos_ref, sin_ref, q_ref, k_ref, v_ref, hy_ref, hyb_ref) = refs
    else:
        (x_ref, xp_ref, xn_ref, sh_ref, sc_ref, g_ref, win_ref, qg_ref, wqa_ref, kvg_ref, wk_ref,
         wv_ref, cw_ref, cb_ref, q_ref, k_ref, v_ref, hy_ref, hyb_ref) = refs
    tm = x_ref.shape[0]
    he = _halo_rows(x_ref, xp_ref, xn_ref, lambda a: _modulate(a, g_ref[...], sh_ref[...], sc_ref[...]))
    ze = _dot(he, win_ref[...])
    o = q_lora + kv_lora
    hy = _conv3_rows(ze[:, o + 2 * HEAD_LANES:], cw_ref[...], cb_ref[...], tm)
    hy_ref[...] = hy
    hyb_ref[...] = hy[:, :ch].astype(BF16)
    z = ze[HALO:HALO + tm]
    qn = _rms(z[:, :q_lora], qg_ref[...]).astype(BF16)
    kvn = _rms(z[:, q_lora:o], kvg_ref[...]).astype(BF16)
    v_ref[...] = _dot(kvn, wv_ref[...]).astype(BF16)
    qa = _dot(qn, wqa_ref[...])
    kk = _dot(kvn, wk_ref[...])
    kr = z[:, o:o + HEAD_LANES]
    if rope:
        cos = cos_ref[...]
        sin = sin_ref[...]
        qb = _dot(qn, wqb_ref[...])
        kr = kr * cos + z[:, o + HEAD_LANES:o + 2 * HEAD_LANES] * sin
    for hh in range(heads):
        sl = slice(hh * HEAD_LANES, (hh + 1) * HEAD_LANES)
        qh = qa[:, sl]
        if rope:
            qh = qh * cos + qb[:, sl] * sin
        q_ref[:, sl] = (qh * scale).astype(BF16)
        k_ref[:, sl] = (kk[:, sl] + kr).astype(BF16)


def _rope_swap(w):
    nf = ROPE_NF
    return jnp.concatenate([-w[:, nf:2 * nf], w[:, :nf], -w[:, 3 * nf:4 * nf], w[:, 2 * nf:3 * nf]], axis=1)


def _ab_weights(w_in, w_q_b, w_kv_b, w_out, q_lora, kv_lora, hy_ch):
    d = w_in.shape[0]
    heads = w_q_b.shape[1] // (MLA_NOPE + MLA_ROPE)
    o = q_lora + kv_lora
    w_kr = w_in[:, o:o + MLA_ROPE]
    zpad = lambda n: jnp.zeros((d, n), F32)
    tail = HEAD_LANES - MLA_NOPE - MLA_ROPE
    win = jnp.concatenate(
        [w_in[:, :o],
         zpad(MLA_NOPE), w_kr, zpad(tail),
         zpad(MLA_NOPE), _rope_swap(w_kr), zpad(tail),
         w_in[:, o + MLA_ROPE:]], axis=1).astype(BF16)
    wq = w_q_b.reshape(q_lora, heads, MLA_NOPE + MLA_ROPE)
    zq = jnp.zeros((q_lora, heads, tail), F32)
    wqa = jnp.concatenate([wq, zq], axis=2).reshape(q_lora, heads * HEAD_LANES).astype(BF16)
    wq_rope = wq[:, :, MLA_NOPE:].reshape(q_lora * heads, MLA_ROPE)
    wqb = jnp.concatenate([jnp.zeros((q_lora, heads, MLA_NOPE), F32),
                           _rope_swap(wq_rope).reshape(q_lora, heads, MLA_ROPE), zq],
                          axis=2).reshape(q_lora, heads * HEAD_LANES).astype(BF16)
    wkv = w_kv_b.reshape(kv_lora, heads, MLA_NOPE + MLA_V)
    zk = jnp.zeros((kv_lora, heads, HEAD_LANES - MLA_NOPE), F32)
    wk = jnp.concatenate([wkv[:, :, :MLA_NOPE], zk], axis=2).reshape(kv_lora, heads * HEAD_LANES).astype(BF16)
    zv = jnp.zeros((kv_lora, heads, HEAD_LANES - MLA_V), F32)
    wv = jnp.concatenate([wkv[:, :, MLA_NOPE:], zv], axis=2).reshape(kv_lora, heads * HEAD_LANES).astype(BF16)
    wo = w_out[:heads * MLA_V].reshape(heads, MLA_V, d)
    wo_a = jnp.concatenate([wo, jnp.zeros((heads, HEAD_LANES - MLA_V, d), F32)],
                           axis=1).reshape(heads * HEAD_LANES, d).astype(BF16)
    wo_h = w_out[heads * MLA_V:].astype(BF16)
    return dict(win=win, wqa=wqa, wqb=wqb, wk=wk, wv=wv, wo_a=wo_a, wo_h=wo_h, heads=heads)


def _rope_tables(n):
    rows = n // GRID_W
    row = jnp.repeat(jnp.arange(rows), GRID_W).astype(F32)
    col = jnp.tile(jnp.arange(GRID_W), rows).astype(F32)
    inv = ROPE_BASE ** (-jnp.arange(ROPE_NF, dtype=F32) / ROPE_NF)
    ar = row[:, None] * inv
    ac = col[:, None] * inv
    ones = jnp.ones((n, MLA_NOPE), F32)
    tail = HEAD_LANES - MLA_NOPE - MLA_ROPE
    cos = jnp.concatenate([ones, jnp.cos(ar), jnp.cos(ar), jnp.cos(ac), jnp.cos(ac),
                           jnp.ones((n, tail), F32)], axis=1)
    sin = jnp.concatenate([0 * ones, jnp.sin(ar), jnp.sin(ar), jnp.sin(ac), jnp.sin(ac),
                           jnp.zeros((n, tail), F32)], axis=1)
    return cos, sin


def _halo_specs(tm, n, d):
    nh = n // HALO
    tpb = tm // HALO
    return [pl.BlockSpec((None, tm, d), lambda bi, i: (bi, i, 0)),
            pl.BlockSpec((None, HALO, d), lambda bi, i: (bi, jnp.maximum(i * tpb - 1, 0), 0)),
            pl.BlockSpec((None, HALO, d), lambda bi, i: (bi, jnp.minimum((i + 1) * tpb, nh - 1), 0))]


def _ab_project(x, mods, row_fn, norm_g, w, q_norm_g, kv_norm_g, rope_tabs, conv_w, conv_b, ch):
    b, n, d = x.shape
    tm = min(ROW_TILE, n)
    heads = w["heads"]
    q_lora = q_norm_g.shape[-1]
    kv_lora = kv_norm_g.shape[-1]
    hw = heads * HEAD_LANES
    hy_cols = conv_w.shape[-1]
    rope = rope_tabs is not None
    tok = lambda width: pl.BlockSpec((None, tm, width), lambda bi, i: (bi, i, 0))
    in_specs = _halo_specs(tm, n, d) + [
        _mod_spec(row_fn, 0, d), _mod_spec(row_fn, 1, d), _resident((1, d)),
        _resident(w["win"].shape), _resident((1, q_lora)), _resident(w["wqa"].shape)]
    args = [x, x, x, mods, mods, norm_g.reshape(1, d), w["win"], q_norm_g.reshape(1, q_lora), w["wqa"]]
    if rope:
        in_specs.append(_resident(w["wqb"].shape))
        args.append(w["wqb"])
    in_specs += [_resident((1, kv_lora)), _resident(w["wk"].shape), _resident(w["wv"].shape),
                 _resident(conv_w.shape), _resident((1, hy_cols))]
    args += [kv_norm_g.reshape(1, kv_lora), w["wk"], w["wv"], conv_w, conv_b.reshape(1, hy_cols)]
    if rope:
        in_specs += [pl.BlockSpec((tm, HEAD_LANES), lambda bi, i: (i, 0))] * 2
        args += list(rope_tabs)
    kern = functools.partial(_ab_proj_kernel, rope=rope, q_lora=q_lora, kv_lora=kv_lora, heads=heads, ch=ch,
                             scale=math.log2(math.e) / math.sqrt(MLA_NOPE + MLA_ROPE))
    return pl.pallas_call(
        kern,
        grid=(b, n // tm),
        in_specs=in_specs,
        out_specs=[tok(hw), tok(hw), tok(hw), tok(hy_cols), tok(ch)],
        out_shape=[jax.ShapeDtypeStruct((b, n, hw), BF16)] * 3 + [jax.ShapeDtypeStruct((b, n, hy_cols), F32),
                                                                   jax.ShapeDtypeStruct((b, n, ch), BF16)],
        compiler_params=_cparams("parallel", "parallel"),
        name="ab_project",
    )(*args)


def _attn_kernel(*refs, sub, nseg):
    q_ref, o_ref = refs[0], refs[-1]
    ks = [r[...] for r in refs[1:1 + nseg]]
    vs = [r[...] for r in refs[1 + nseg:1 + 2 * nseg]]
    for r in range(0, q_ref.shape[0], sub):
        q = q_ref[r:r + sub, :]
        ss = [_dot(q, k, _NT) for k in ks]
        m = functools.reduce(jnp.maximum, [jnp.max(s, axis=-1, keepdims=True) for s in ss])
        ps = [jnp.exp2(s - m) for s in ss]
        l = sum(jnp.sum(p, axis=-1, keepdims=True) for p in ps)
        o = sum(_dot(p.astype(BF16), v) for p, v in zip(ps, vs))
        o_ref[r:r + sub, :] = (o / l).astype(o_ref.dtype)


def _attention(q, ks, vs, heads):
    b, n, hw = q.shape
    tq = min(ATTN_TQ, n)
    tile = pl.BlockSpec((None, tq, HEAD_LANES), lambda bi, h, i: (bi, i, h))
    whole = lambda a: pl.BlockSpec((None, a.shape[1], HEAD_LANES), lambda bi, h, i: (bi, 0, h))
    return pl.pallas_call(
        functools.partial(_attn_kernel, sub=min(ATTN_SUB, tq), nseg=len(ks)),
        grid=(b, heads, n // tq),
        in_specs=[tile] + [whole(a) for a in ks] + [whole(a) for a in vs],
        out_specs=tile,
        out_shape=jax.ShapeDtypeStruct((b, n, hw), BF16),
        compiler_params=_cparams("parallel", "parallel", "parallel"),
        name="attention",
    )(q, *ks, *vs)


def _hy_filter_kernel(z_ref, t_ref, w1_ref, b1_ref, w2_ref, b2_ref, w3_ref, dl_ref, o_ref, *, ch):
    hdot = lambda a, w: jnp.dot(a, w, precision=HIGHEST, preferred_element_type=F32)
    a = jnp.sin(HY_SIN_FREQ * (hdot(z_ref[...], w1_ref[...]) + b1_ref[...]))
    a = jnp.sin(HY_SIN_FREQ * (hdot(a, w2_ref[...]) + b2_ref[...]))
    f = hdot(a, w3_ref[...])
    dec = jnp.exp(-t_ref[...] * dl_ref[...])
    tl = z_ref.shape[0]
    row = pl.program_id(0) * tl + lax.broadcasted_iota(jnp.int32, (tl, ch), 0)
    for o in range(HY_ORDER):
        hf = f[:, (2 * o) * ch:(2 * o + 1) * ch] * dec
        hb = jnp.where(row > 0, f[:, (2 * o + 1) * ch:(2 * o + 2) * ch] * dec, 0.0)
        o_ref[:, (2 * o) * ch:(2 * o + 1) * ch] = (hf + hb).astype(BF16)
        o_ref[:, (2 * o + 1) * ch:(2 * o + 2) * ch] = (hf - hb).astype(BF16)


def _hy_filter_inputs(length, w1, b1, w2, b2, w3, ch):
    t = jnp.linspace(0.0, 1.0, length, dtype=F32)[:, None]
    wv = 2.0 * math.pi * jnp.arange(length, dtype=F32)[:, None] / length
    f = jnp.linspace(1e-4, HY_BANDS - 1, HY_BANDS, dtype=F32)
    z = jnp.concatenate([t, jnp.cos(f * wv), -jnp.sin(f * wv)], axis=-1)
    emb = z.shape[1]
    z = jnp.pad(z, ((0, 0), (0, HEAD_LANES - emb)))
    w1p = jnp.pad(w1, ((0, HEAD_LANES - emb), (0, 0)))
    deltas = jnp.abs(jnp.linspace(HY_MIN_DECAY, HY_MAX_DECAY, ch, dtype=F32))[None, :]
    fw = w1.shape[1]
    tl = min(512, length)
    return pl.pallas_call(
        functools.partial(_hy_filter_kernel, ch=ch),
        grid=(length // tl,),
        in_specs=[pl.BlockSpec((tl, HEAD_LANES), lambda i: (i, 0)), pl.BlockSpec((tl, 1), lambda i: (i, 0)),
                  _resident((HEAD_LANES, fw)), _resident((1, fw)), _resident((fw, fw)), _resident((1, fw)),
                  _resident(w3.shape), _resident((1, ch))],
        out_specs=pl.BlockSpec((tl, 2 * HY_ORDER * ch), lambda i: (i, 0)),
        out_shape=jax.ShapeDtypeStruct((length, 2 * HY_ORDER * ch), BF16),
        compiler_params=_cparams("parallel"),
        name="hy_filter",
    )(z, t, w1p, b1.reshape(1, fw), w2, b2.reshape(1, fw), w3, deltas)


def _dft_tables(length):
    n2 = 2 * length
    theta = 2.0 * math.pi / n2
    n = jnp.arange(length, dtype=jnp.int32)[None, :]
    a = 8 * jnp.arange(length // 8, dtype=jnp.int32)[:, None]
    b = jnp.arange(8, dtype=jnp.int32)[:, None]
    ang_a = ((a * n) % n2).astype(F32) * theta
    ang_b = ((b * n) % n2).astype(F32) * theta
    ca, sa = jnp.cos(ang_a)[:, None, :], jnp.sin(ang_a)[:, None, :]
    cb, sb = jnp.cos(ang_b)[None, :, :], jnp.sin(ang_b)[None, :, :]
    fc = (ca * cb - sa * sb).reshape(length, length)
    fs = -(sa * cb + ca * sb).reshape(length, length)
    return fc.astype(BF16), fs.astype(BF16)


def _alternating_sum(x):
    col = lax.broadcasted_iota(jnp.int32, (8, x.shape[0]), 1)
    alt = (1 - 2 * (col & 1)).astype(F32).astype(BF16)
    return _dot(alt, x)[0:1]


def _filter_dft_kernel(hs_ref, hd_ref, fc_ref, fs_ref, o_ref):
    hs = hs_ref[...]
    o_ref[0] = _dot(fc_ref[...], hs)
    ki = _dot(fs_ref[...], hd_ref[...])
    nyq = _alternating_sum(hs)
    tf = ki.shape[0]
    row = pl.program_id(1) * tf + lax.broadcasted_iota(jnp.int32, (tf, 1), 0)
    o_ref[1] = jnp.where(row == 0, nyq, ki)


def _dft_spec_kernel(x_ref, fc_ref, fs_ref, k_ref, o_ref, *, inv_n):
    x = x_ref[...]
    ur = _dot(fc_ref[...], x)
    kr = k_ref[0]
    ki = k_ref[1]
    tf = ur.shape[0]
    row = pl.program_id(1) * tf + lax.broadcasted_iota(jnp.int32, (tf, 1), 0)
    first = row == 0
    ui = jnp.where(first, _alternating_sum(x), _dot(fs_ref[...], x))
    uiki = ui * ki
    yr = ur * kr - jnp.where(first, 0.0, uiki)
    yi = jnp.where(first, uiki, ur * ki + ui * kr)
    scale = jnp.where(first, inv_n, 2.0 * inv_n)
    o_ref[0] = (yr * scale).astype(BF16)
    o_ref[1] = (yi * scale).astype(BF16)


def _filter_spectrum(p, fc, fs, ch):
    length, cols = p.shape
    tf = min(512, length)
    return pl.pallas_call(
        _filter_dft_kernel,
        grid=(HY_ORDER, length // tf),
        in_specs=[pl.BlockSpec((length, ch), lambda o, i: (0, 2 * o)),
                  pl.BlockSpec((length, ch), lambda o, i: (0, 2 * o + 1)),
                  pl.BlockSpec((tf, length), lambda o, i: (i, 0)),
                  pl.BlockSpec((tf, length), lambda o, i: (i, 0))],
        out_specs=pl.BlockSpec((None, 2, tf, ch), lambda o, i: (o, 0, i, 0)),
        out_shape=jax.ShapeDtypeStruct((HY_ORDER, 2, length, ch), F32),
        compiler_params=_cparams("parallel", "arbitrary"),
        name="hy_filter_dft",
    )(p, p, fc, fs)


def _hy_spectral(xb, col, ch, fc, fs, kspec):
    b, length, _ = xb.shape
    tf = min(512, length)
    return pl.pallas_call(
        functools.partial(_dft_spec_kernel, inv_n=1.0 / (2 * length)),
        grid=(b, length // tf),
        in_specs=[pl.BlockSpec((None, length, ch), lambda bi, i: (bi, 0, col)),
                  pl.BlockSpec((tf, length), lambda bi, i: (i, 0)),
                  pl.BlockSpec((tf, length), lambda bi, i: (i, 0)),
                  pl.BlockSpec((2, tf, ch), lambda bi, i: (0, i, 0))],
        out_specs=pl.BlockSpec((None, 2, tf, ch), lambda bi, i: (bi, 0, i, 0)),
        out_shape=jax.ShapeDtypeStruct((b, 2, length, ch), BF16),
        compiler_params=_cparams("parallel", "arbitrary"),
        name="hy_dft_spectral",
    )(xb, fc, fs, kspec)


def _idft_gate_kernel(y_ref, fc_ref, fs_ref, u_ref, xg_ref, d_ref, o_ref, ob_ref):
    tt = fc_ref.shape[0]
    t = pl.program_id(1) * tt + lax.broadcasted_iota(jnp.int32, (tt, 1), 0)
    sign = (1 - 2 * (t & 1)).astype(F32)
    conv = (_dot(fc_ref[...], y_ref[0]) + _dot(fs_ref[...], y_ref[1])
            + sign * y_ref[1, 0:1, :].astype(F32))
    out = xg_ref[...] * (conv + u_ref[...] * d_ref[...])
    o_ref[...] = out
    ob_ref[...] = out.astype(BF16)


def _hy_inverse_gate(y, fc, fs, u, u_col, xg, xg_col, dvec, ch):
    b, _, length, _ = y.shape
    tt = min(512, length)
    out = pl.BlockSpec((None, tt, ch), lambda bi, i: (bi, i, 0))
    return pl.pallas_call(
        _idft_gate_kernel,
        grid=(b, length // tt),
        in_specs=[pl.BlockSpec((None, 2, length, ch), lambda bi, i: (bi, 0, 0, 0)),
                  pl.BlockSpec((tt, length), lambda bi, i: (i, 0)),
                  pl.BlockSpec((tt, length), lambda bi, i: (i, 0)),
                  pl.BlockSpec((None, tt, ch), lambda bi, i: (bi, i, u_col)),
                  pl.BlockSpec((None, tt, ch), lambda bi, i: (bi, i, xg_col)),
                  pl.BlockSpec((1, ch), lambda bi, i: (0, 0))],
        out_specs=[out, out],
        out_shape=[jax.ShapeDtypeStruct((b, length, ch), F32), jax.ShapeDtypeStruct((b, length, ch), BF16)],
        compiler_params=_cparams("parallel", "arbitrary"),
        name="hy_idft_gate",
    )(y, fc, fs, u, xg, dvec.reshape(1, ch))


def _hyena(hyc, hyb, w1, b1, w2, b2, w3, hy_bias):
    ch = hy_bias.shape[-1]
    length = hyc.shape[1]
    fc, fs = _dft_tables(length)
    kspec = _filter_spectrum(_hy_filter_inputs(length, w1, b1, w2, b2, w3, ch), fc, fs, ch)
    y = _hy_spectral(hyb, 0, ch, fc, fs, kspec[0])
    y1, y1b = _hy_inverse_gate(y, fc, fs, hyc, 0, hyc, 1, hy_bias[0], ch)
    y = _hy_spectral(y1b, 0, ch, fc, fs, kspec[1])
    y2, _ = _hy_inverse_gate(y, fc, fs, y1, 0, hyc, 2, hy_bias[1], ch)
    return y2


def _ab_out_kernel(o_ref, y_ref, x_ref, g1_ref, woa_ref, woh_ref, out_ref):
    y = _dot(o_ref[...], woa_ref[...]) + _dot(y_ref[...].astype(BF16), woh_ref[...])
    out_ref[...] = x_ref[...] + g1_ref[...] * y


def _ab_out(o, y2, x, mods, row_fn, w):
    b, n, d = x.shape
    tm = min(ROW_TILE, n)
    tok = lambda width: pl.BlockSpec((None, tm, width), lambda bi, i: (bi, i, 0))
    return pl.pallas_call(
        _ab_out_kernel,
        grid=(b, n // tm),
        in_specs=[tok(o.shape[-1]), tok(y2.shape[-1]), tok(d), _mod_spec(row_fn, 2, d),
                  _resident(w["wo_a"].shape), _resident(w["wo_h"].shape)],
        out_specs=tok(d),
        out_shape=jax.ShapeDtypeStruct((b, n, d), F32),
        compiler_params=_cparams("parallel", "parallel"),
        name="ab_out",
    )(o, y2, x, mods, w["wo_a"], w["wo_h"])


FFN_ROWS = 512
FFN_CHUNK_MAX = 1408


def _ffn_kernel(*refs, hidden, hc, final):
    if final:
        (x_ref, xp_ref, xn_ref, sh_ref, sc_ref, g2_ref, ng_ref, wup_ref, cw_ref, cb_ref, wdn_ref,
         fg_ref, o_ref) = refs
    else:
        (x_ref, xp_ref, xn_ref, sh_ref, sc_ref, g2_ref, ng_ref, wup_ref, cw_ref, cb_ref, wdn_ref,
         o_ref) = refs
    x = x_ref[...]
    tm, d = x.shape
    he = _halo_rows(x_ref, xp_ref, xn_ref, lambda a: _modulate(a, ng_ref[...], sh_ref[...], sc_ref[...]))

    def conv(a, col):
        return _conv3_rows(a, cw_ref[:, col:col + hc], cb_ref[:, col:col + hc], tm)

    acc = jnp.zeros((tm, d), F32)
    for j in range(hidden // hc):
        gate = conv(_dot(he, wup_ref[:, j * hc:(j + 1) * hc]), j * hc)
        val = conv(_dot(he, wup_ref[:, hidden + j * hc:hidden + (j + 1) * hc]), hidden + j * hc)
        mid = (_silu(gate) * val).astype(BF16)
        acc = acc + _dot(mid, wdn_ref[j * hc:(j + 1) * hc, :])
    out = x + g2_ref[...] * acc
    if final:
        out = _rms(out, fg_ref[...])
    o_ref[...] = out


def _ffn_chunk(hidden):
    for hc in range(min(FFN_CHUNK_MAX, hidden) // HEAD_LANES * HEAD_LANES, 0, -HEAD_LANES):
        if hidden % hc == 0:
            return hc
    raise ValueError(f"FFN hidden width {hidden} is not a multiple of {HEAD_LANES}")


def _conv_ffn(x, mods, row_fn, norm_g, w_up, conv_w, conv_b, w_down, final_g=None):
    b, n, d = x.shape
    hidden = w_down.shape[0]
    tm = min(FFN_ROWS, n)
    final = final_g is not None
    tok = pl.BlockSpec((None, tm, d), lambda bi, i: (bi, i, 0))
    in_specs = _halo_specs(tm, n, d) + [
        _mod_spec(row_fn, 3, d), _mod_spec(row_fn, 4, d), _mod_spec(row_fn, 5, d),
        _resident((1, d)), _resident(w_up.shape), _resident(conv_w.shape),
        _resident((1, 2 * hidden)), _resident(w_down.shape)]
    args = [x, x, x, mods, mods, mods, norm_g.reshape(1, d), w_up, conv_w, conv_b.reshape(1, 2 * hidden), w_down]
    if final:
        in_specs.append(_resident((1, d)))
        args.append(final_g.reshape(1, d))
    return pl.pallas_call(
        functools.partial(_ffn_kernel, hidden=hidden, hc=_ffn_chunk(hidden), final=final),
        grid=(b, n // tm),
        in_specs=in_specs,
        out_specs=tok,
        out_shape=jax.ShapeDtypeStruct((b, n, d), F32),
        compiler_params=_cparams("parallel", "parallel"),
        name="conv_ffn",
    )(*args)


def _hg_proj_kernel(x_ref, sh_ref, sc_ref, g_ref, w_ref, o_ref, *, width):
    h = _modulate(x_ref[...], g_ref[...], sh_ref[...], sc_ref[...]).astype(BF16)
    z = _dot(h, w_ref[...])
    o_ref[:, :width] = _silu(z[:, :width])
    o_ref[:, width:] = z[:, width:]


def _hg_project(x, mods, row_fn, norm_g, w_in):
    b, n, d = x.shape
    cols = w_in.shape[1]
    tm = min(ROW_TILE, n)
    tok = lambda width: pl.BlockSpec((None, tm, width), lambda bi, i: (bi, i, 0))
    return pl.pallas_call(
        functools.partial(_hg_proj_kernel, width=cols // 5),
        grid=(b, n // tm),
        in_specs=[tok(d), _mod_spec(row_fn, 0, d), _mod_spec(row_fn, 1, d), _resident((1, d)),
                  _resident(w_in.shape)],
        out_specs=tok(cols),
        out_shape=jax.ShapeDtypeStruct((b, n, cols), F32),
        compiler_params=_cparams("parallel", "parallel"),
        name="hg_project",
    )(x, mods, mods, norm_g.reshape(1, d), w_in)


def _scan_tables(direction):
    c = HG_CHUNK
    idx = np.arange(c)
    ops = []
    level = np.full((c, c), -1, np.int32)
    for li, m in enumerate(HG_LEVELS):
        op = np.zeros((c, c), np.float32)
        for r in range(c):
            lo = (r // m) * m
            hi = lo + m - 1
            upper = (r % (2 * m)) >= m
            if direction == 0:
                if upper:
                    op[r, lo:r + 1] = 1
                else:
                    op[r, r + 1:hi + 1] = 1
            else:
                if not upper:
                    op[r, r:hi + 1] = 1
                else:
                    op[r, lo:r] = 1
        ops.append(op)
        same = (idx[:, None] // (2 * m)) == (idx[None, :] // (2 * m))
        up = (idx % (2 * m)) >= m
        pair = same & (up[:, None] & ~up[None, :] if direction == 0 else ~up[:, None] & up[None, :])
        level[pair] = li
    level[idx, idx] = len(HG_LEVELS)
    if direction == 0:
        to_q = (idx[None, :] <= idx[:, None])
    else:
        to_q = (idx[None, :] >= idx[:, None])
    ops += [to_q.astype(np.float32), np.ones((8, c), np.float32)]
    em = np.concatenate(ops, axis=0)
    return jnp.asarray(np.concatenate([em, em], axis=1), BF16), jnp.asarray(level)


def _gla_group(chunks, lb, em2, lv, st):
    c = HG_CHUNK
    e_ = HG_EXPAND
    nl = len(HG_LEVELS)
    log_lb = jnp.log(lb)
    log_1m = jnp.log1p(-lb)
    log_f = []
    keys = []
    for _, fr, _ in chunks:
        t = jnp.exp(-jnp.abs(fr))
        d = 1.0 + t
        keys.append((1.0 - lb) * (jnp.where(fr > 0, t, 1.0) / d))
        b = log_1m + (jnp.minimum(fr, 0.0) - jnp.log(d))
        log_f.append(jnp.maximum(log_lb, b) + jnp.log(1.0 + jnp.exp(-jnp.abs(log_lb - b))))
    g = jnp.concatenate(log_f, axis=1) * math.log2(math.e)
    hi = g.astype(BF16)
    lo = (g - hi.astype(F32)).astype(BF16)
    e_all = _dot(em2, jnp.concatenate([hi, lo], axis=0))
    outs = []
    for gi, (q, _, val) in enumerate(chunks):
        e = e_all[:, gi * e_:(gi + 1) * e_]
        key = keys[gi]
        att = jnp.zeros((c, c), F32)
        for li in range(nl):
            dec = jnp.exp2(e[li * c:(li + 1) * c])
            att = jnp.where(lv == li, _dot((q * dec).astype(BF16), (key * dec).astype(BF16), _NT), att)
        att = jnp.where(lv == nl, jnp.sum(q * key, axis=1, keepdims=True), att)
        vb = val.astype(BF16)
        to_q = e[nl * c:(nl + 1) * c]
        total = e[(nl + 1) * c:(nl + 1) * c + 1]
        q_in = (q * jnp.exp2(to_q)).astype(BF16)
        outs.append(_dot(att.astype(BF16), vb) + _dot(q_in, st.astype(BF16), _NT))
        k_out = (key * jnp.exp2(total - to_q)).astype(BF16)
        st = st * jnp.exp2(total) + _dot(vb, k_out, _TN)
    return outs, st


def _hg_scan_kernel(*refs, layer, need_ctx):
    (lg_ref, emf_ref, emb_ref, lvf_ref, lvb_ref,
     qc_ref, ffc_ref, fbc_ref, ic_ref, q_ref, ff_ref, fb_ref, i_ref) = refs[:13]
    if need_ctx:
        o_ref, oc_ref, st_ref = refs[13:]
    else:
        o_ref, st_ref = refs[13:]
        oc_ref = None
    c = HG_CHUNK
    lg = lg_ref[...]
    ex = jnp.exp(lg - jnp.max(lg, axis=0, keepdims=True))
    probs = ex / jnp.sum(ex, axis=0, keepdims=True)
    acc = probs[0]
    for l in range(1, layer + 1):
        acc = acc + probs[l]
    lb_all = acc - probs[0]
    lb_f = lb_all[0:1]
    lb_b = lb_all[1:2]

    st_ref[...] = jnp.zeros_like(st_ref)
    o_ref[...] = jnp.zeros_like(o_ref)
    if need_ctx:
        oc_ref[...] = jnp.zeros_like(oc_ref)

    def run(qr, ffr, fbr, ir, outr):
        nc = qr.shape[0] // c
        ng = nc // HG_GROUP
        assert ng * HG_GROUP == nc

        def body(gi, carry):
            rows_f = [pl.multiple_of((gi * HG_GROUP + k) * c, c) for k in range(HG_GROUP)]
            rows_b = [pl.multiple_of((nc - 1 - gi * HG_GROUP - k) * c, c) for k in range(HG_GROUP)]
            load = lambda gate_ref, r: (qr[pl.ds(r, c), :], gate_ref[pl.ds(r, c), :], ir[pl.ds(r, c), :])
            of, stf = _gla_group([load(ffr, r) for r in rows_f], lb_f, emf_ref[...], lvf_ref[...], st_ref[0])
            st_ref[0] = stf
            ob, stb = _gla_group([load(fbr, r) for r in rows_b], lb_b, emb_ref[...], lvb_ref[...], st_ref[1])
            st_ref[1] = stb
            if outr is not None:
                for r, o in zip(rows_f + rows_b, of + ob):
                    outr[pl.ds(r, c), :] += o
            return carry

        lax.fori_loop(0, ng, body, 0, unroll=min(HG_UNROLL, ng))

    run(qc_ref, ffc_ref, fbc_ref, ic_ref, oc_ref)
    run(q_ref, ff_ref, fb_ref, i_ref, o_ref)


def _hg_scan(z, zc, lb_logits, layer, need_ctx):
    b, n, cols = z.shape
    nc_ = zc.shape[1]
    width = cols // 5
    heads = width // HG_EXPAND
    e = HG_EXPAND
    depth = lb_logits.shape[0]
    emf, lvf = _scan_tables(0)
    emb, lvb = _scan_tables(1)
    col = lambda rows, group: pl.BlockSpec((None, rows, e), lambda bi, h: (bi, 0, group * heads + h))
    in_specs = [pl.BlockSpec((depth, 2, e), lambda bi, h: (0, 0, h)),
                _resident(emf.shape), _resident(emb.shape), _resident(lvf.shape), _resident(lvb.shape),
                col(nc_, 0), col(nc_, 1), col(nc_, 2), col(nc_, 3),
                col(n, 0), col(n, 1), col(n, 2), col(n, 3)]
    out_specs = [pl.BlockSpec((None, n, e), lambda bi, h: (bi, 0, h))]
    out_shape = [jax.ShapeDtypeStruct((b, n, width), F32)]
    if need_ctx:
        out_specs.append(pl.BlockSpec((None, nc_, e), lambda bi, h: (bi, 0, h)))
        out_shape.append(jax.ShapeDtypeStruct((b, nc_, width), F32))
    res = pl.pallas_call(
        functools.partial(_hg_scan_kernel, layer=layer, need_ctx=need_ctx),
        grid=(b, heads),
        in_specs=in_specs,
        out_specs=out_specs,
        out_shape=out_shape,
        scratch_shapes=[pltpu.VMEM((2, e, e), F32)],
        compiler_params=_cparams("parallel", "parallel"),
        name="hg_scan",
    )(lb_logits, emf, emb, lvf, lvb, zc, zc, zc, zc, z, z, z, z)
    return (res[0], res[1]) if need_ctx else (res[0], None)


def _hg_out_kernel(o_ref, g_ref, x_ref, g1_ref, ng_ref, w_ref, out_ref, *, heads):
    ng = ng_ref[...]
    parts = [_rms(o_ref[:, h * HG_EXPAND:(h + 1) * HG_EXPAND], ng) for h in range(heads)]
    o = jnp.concatenate(parts, axis=1) * _silu(g_ref[...])
    out_ref[...] = x_ref[...] + g1_ref[...] * _dot(o.astype(BF16), w_ref[...])


def _hg_out(o, z, x, mods, row_fn, norm_g, w_out):
    b, n, d = x.shape
    width = o.shape[-1]
    tm = min(ROW_TILE, n)
    tok = lambda wd: pl.BlockSpec((None, tm, wd), lambda bi, i: (bi, i, 0))
    return pl.pallas_call(
        functools.partial(_hg_out_kernel, heads=width // HG_EXPAND),
        grid=(b, n // tm),
        in_specs=[tok(width), pl.BlockSpec((None, tm, width), lambda bi, i: (bi, i, 4)), tok(d),
                  _mod_spec(row_fn, 2, d), _resident((1, HG_EXPAND)), _resident(w_out.shape)],
        out_specs=tok(d),
        out_shape=jax.ShapeDtypeStruct((b, n, d), F32),
        compiler_params=_cparams("parallel", "parallel"),
        name="hg_out",
    )(o, z, x, mods, norm_g.reshape(1, HG_EXPAND), w_out)


def kernel(x, c, ctx, c_ctx, mod_w, mod_b, norm1_g, norm2_g, ffn_w_up, ffn_conv_w, ffn_conv_b, ffn_w_down, ab_w_in, mla_q_norm_g, mla_w_q_b, mla_kv_norm_g, mla_w_kv_b, hy_conv_w, hy_conv_b, hy_w1, hy_b1, hy_w2, hy_b2, hy_w3, hy_bias, ab_w_out, hg_w_in, hg_lb_logits, hg_norm_g, hg_w_out, final_norm_g):
    batch, n, d = x.shape
    depth = mod_w.shape[0]
    assert batch < MOD_ROWS
    ctx_row = batch
    cc = jnp.zeros((MOD_ROWS, d), F32).at[:batch].set(c).at[ctx_row].set(c_ctx)
    mods_all = _mod_vectors(cc, mod_w, mod_b)
    lat_row = lambda bi, *_: bi
    ctx_row_fn = lambda bi, *_: ctx_row
    rope_tabs = _rope_tables(n)

    xc = ctx
    for layer in range(depth):
        need_ctx = layer < depth - 1
        j = layer // 2
        mods = mods_all[layer]
        if layer % 2 == 0:
            q_lora = mla_q_norm_g.shape[-1]
            kv_lora = mla_kv_norm_g.shape[-1]
            hy_ch = hy_bias.shape[-1]
            w = _ab_weights(ab_w_in[j], mla_w_q_b[j], mla_w_kv_b[j], ab_w_out[j], q_lora, kv_lora, hy_ch)
            hy_args = (hy_w1[j], hy_b1[j], hy_w2[j], hy_b2[j], hy_w3[j], hy_bias[j])
            q_c, k_c, v_c, hy_c, hyb_c = _ab_project(xc, mods, ctx_row_fn, norm1_g[layer], w, mla_q_norm_g[j],
                                                     mla_kv_norm_g[j], None, hy_conv_w[j], hy_conv_b[j], hy_ch)
            q, k, v, hy, hyb = _ab_project(x, mods, lat_row, norm1_g[layer], w, mla_q_norm_g[j],
                                           mla_kv_norm_g[j], rope_tabs, hy_conv_w[j], hy_conv_b[j], hy_ch)
            o = _attention(q, [jnp.concatenate([k_c, k], axis=1)], [jnp.concatenate([v_c, v], axis=1)], w["heads"])
            x_new = _ab_out(o, _hyena(hy, hyb, *hy_args), x, mods, lat_row, w)
            if need_ctx:
                o_c = _attention(q_c, [k_c], [v_c], w["heads"])
                xc = _ab_out(o_c, _hyena(hy_c, hyb_c, *hy_args), xc, mods, ctx_row_fn, w)
            x = x_new
        else:
            w_in = hg_w_in[j].astype(BF16)
            z_c = _hg_project(xc, mods, ctx_row_fn, norm1_g[layer], w_in)
            z = _hg_project(x, mods, lat_row, norm1_g[layer], w_in)
            o, o_c = _hg_scan(z, z_c, hg_lb_logits, layer, need_ctx)
            w_out = hg_w_out[j].astype(BF16)
            x = _hg_out(o, z, x, mods, lat_row, hg_norm_g[j], w_out)
            if need_ctx:
                xc = _hg_out(o_c, z_c, xc, mods, ctx_row_fn, hg_norm_g[j], w_out)
        w_up = ffn_w_up[layer].astype(BF16)
        w_dn = ffn_w_down[layer].astype(BF16)
        last = layer == depth - 1
        x = _conv_ffn(x, mods, lat_row, norm2_g[layer], w_up, ffn_conv_w[layer], ffn_conv_b[layer], w_dn,
                      final_norm_g if last else None)
        if need_ctx:
            xc = _conv_ffn(xc, mods, ctx_row_fn, norm2_g[layer], w_up, ffn_conv_w[layer], ffn_conv_b[layer], w_dn)
    return x
```

```python
import functools
import math

import numpy as np
import jax
import jax.numpy as jnp
from jax import lax
from jax.experimental import pallas as pl
from jax.experimental.pallas import tpu as pltpu

F32 = jnp.float32
BF16 = jnp.bfloat16
HIGHEST = lax.Precision.HIGHEST

EPS = 1e-6
GRID_W = 64

MLA_NOPE = 64
MLA_ROPE = 32
MLA_V = 64
ROPE_NF = MLA_ROPE // 4
ROPE_BASE = 10000.0
HEAD_LANES = 128
ATTN_TQ = 2048
ATTN_SUB = 256

HY_ORDER = 2
HY_BANDS = 16
HY_SIN_FREQ = 1.0
HY_DECAY_TARGET = 1e-2
HY_FAST_DECAY = 0.3
HY_SLOW_DECAY = 1.5
HY_MIN_DECAY = math.log(HY_DECAY_TARGET) / HY_SLOW_DECAY
HY_MAX_DECAY = math.log(HY_DECAY_TARGET) / HY_FAST_DECAY

HG_EXPAND = 128
HG_CHUNK = 64
HG_LEVELS = (32, 16, 8, 4, 2, 1)
HG_GROUP = 4
HG_UNROLL = 2

ROW_TILE = 512
MOD_ROWS = 16
HALO = 8
VMEM_LIMIT = 56 * 1024 * 1024

_NT = (((1,), (1,)), ((), ()))
_TN = (((0,), (0,)), ((), ()))


def _cparams(*sem):
    return pltpu.CompilerParams(dimension_semantics=sem, vmem_limit_bytes=VMEM_LIMIT)


def _dot(a, b, dims=(((1,), (0,)), ((), ()))):
    return lax.dot_general(a, b, dims, precision=lax.Precision.DEFAULT, preferred_element_type=F32)


def _rms(x, g):
    return x * lax.rsqrt(jnp.mean(x * x, axis=-1, keepdims=True) + EPS) * g


def _modulate(x, g, shift, scale):
    return _rms(x, g) * (1.0 + scale) + shift


def _silu(x):
    return x * jax.nn.sigmoid(x)


def _mod_spec(row_fn, k, d):
    return pl.BlockSpec((None, None, 1, d), lambda *idx: (row_fn(*idx), k, 0, 0))


def _resident(shape):
    nd = len(shape)
    return pl.BlockSpec(shape, lambda *idx: (0,) * nd, pipeline_mode=pl.Buffered(1))


def _mod_kernel(c_ref, w_ref, b_ref, o_ref):
    a = _silu(c_ref[...])
    o_ref[...] = jnp.dot(a, w_ref[...], precision=HIGHEST, preferred_element_type=F32) + b_ref[...]


def _mod_vectors(cc, mod_w, mod_b):
    depth, d, n6 = mod_w.shape
    tn = 512
    out = pl.pallas_call(
        _mod_kernel,
        grid=(depth, n6 // tn),
        in_specs=[pl.BlockSpec((MOD_ROWS, d), lambda l, j: (0, 0)),
                  pl.BlockSpec((None, d, tn), lambda l, j: (l, 0, j)),
                  pl.BlockSpec((None, 1, tn), lambda l, j: (l, 0, j))],
        out_specs=pl.BlockSpec((None, MOD_ROWS, tn), lambda l, j: (l, 0, j)),
        out_shape=jax.ShapeDtypeStruct((depth, MOD_ROWS, n6), F32),
        compiler_params=_cparams("parallel", "parallel"),
        name="mod_vectors",
    )(cc, mod_w, mod_b.reshape(depth, 1, n6))
    return out.reshape(depth, MOD_ROWS, 6, 1, d)


def _halo_rows(x_ref, xp_ref, xn_ref, mod):
    i = pl.program_id(1)
    last = pl.num_programs(1) - 1
    hp = mod(xp_ref[...]) * (i > 0).astype(F32)
    hn = mod(xn_ref[...]) * (i < last).astype(F32)
    return jnp.concatenate([hp, mod(x_ref[...]), hn], axis=0).astype(BF16)


def _conv3_rows(a, w, bias, tm):
    ne = tm + 2 * HALO
    prev = pltpu.roll(a, 1, axis=0)[HALO:HALO + tm]
    nxt = pltpu.roll(a, ne - 1, axis=0)[HALO:HALO + tm]
    return prev * w[0:1] + a[HALO:HALO + tm] * w[1:2] + nxt * w[2:3] + bias


def _ab_proj_kernel(*refs, rope, q_lora, kv_lora, heads, scale, ch):
    if rope:
        (x_ref, xp_ref, xn_ref, sh_ref, sc_ref, g_ref, win_ref, qg_ref, wqa_ref, wqb_ref, kvg_ref, wk_ref,
         wv_ref, cw_ref, cb_ref, cos_ref, sin_ref, q_ref, k_ref, v_ref, hy_ref, hyb_ref) = refs
    else:
        (x_ref, xp_ref, xn_ref, sh_ref, sc_ref, g_ref, win_ref, qg_ref, wqa_ref, kvg_ref, wk_ref,
         wv_ref, cw_ref, cb_ref, q_ref, k_ref, v_ref, hy_ref, hyb_ref) = refs
    tm = x_ref.shape[0]
    he = _halo_rows(x_ref, xp_ref, xn_ref, lambda a: _modulate(a, g_ref[...], sh_ref[...], sc_ref[...]))
    ze = _dot(he, win_ref[...])
    o = q_lora + kv_lora
    hy = _conv3_rows(ze[:, o + 2 * HEAD_LANES:], cw_ref[...], cb_ref[...], tm)
    hy_ref[...] = hy
    hyb_ref[...] = hy[:, :ch].astype(BF16)
    z = ze[HALO:HALO + tm]
    qn = _rms(z[:, :q_lora], qg_ref[...]).astype(BF16)
    kvn = _rms(z[:, q_lora:o], kvg_ref[...]).astype(BF16)
    v_ref[...] = _dot(kvn, wv_ref[...]).astype(BF16)
    qa = _dot(qn, wqa_ref[...])
    kk = _dot(kvn, wk_ref[...])
    kr = z[:, o:o + HEAD_LANES]
    if rope:
        cos = cos_ref[...]
        sin = sin_ref[...]
        qb = _dot(qn, wqb_ref[...])
        kr = kr * cos + z[:, o + HEAD_LANES:o + 2 * HEAD_LANES] * sin
    for hh in range(heads):
        sl = slice(hh * HEAD_LANES, (hh + 1) * HEAD_LANES)
        qh = qa[:, sl]
        if rope:
            qh = qh * cos + qb[:, sl] * sin
        q_ref[:, sl] = (qh * scale).astype(BF16)
        k_ref[:, sl] = (kk[:, sl] + kr).astype(BF16)


def _rope_swap(w):
    nf = ROPE_NF
    return jnp.concatenate([-w[:, nf:2 * nf], w[:, :nf], -w[:, 3 * nf:4 * nf], w[:, 2 * nf:3 * nf]], axis=1)


def _ab_weights(w_in, w_q_b, w_kv_b, w_out, q_lora, kv_lora, hy_ch):
    d = w_in.shape[0]
    heads = w_q_b.shape[1] // (MLA_NOPE + MLA_ROPE)
    o = q_lora + kv_lora
    w_kr = w_in[:, o:o + MLA_ROPE]
    zpad = lambda n: jnp.zeros((d, n), F32)
    tail = HEAD_LANES - MLA_NOPE - MLA_ROPE
    win = jnp.concatenate(
        [w_in[:, :o],
         zpad(MLA_NOPE), w_kr, zpad(tail),
         zpad(MLA_NOPE), _rope_swap(w_kr), zpad(tail),
         w_in[:, o + MLA_ROPE:]], axis=1).astype(BF16)
    wq = w_q_b.reshape(q_lora, heads, MLA_NOPE + MLA_ROPE)
    zq = jnp.zeros((q_lora, heads, tail), F32)
    wqa = jnp.concatenate([wq, zq], axis=2).reshape(q_lora, heads * HEAD_LANES).astype(BF16)
    wq_rope = wq[:, :, MLA_NOPE:].reshape(q_lora * heads, MLA_ROPE)
    wqb = jnp.concatenate([jnp.zeros((q_lora, heads, MLA_NOPE), F32),
                           _rope_swap(wq_rope).reshape(q_lora, heads, MLA_ROPE), zq],
                          axis=2).reshape(q_lora, heads * HEAD_LANES).astype(BF16)
    wkv = w_kv_b.reshape(kv_lora, heads, MLA_NOPE + MLA_V)
    zk = jnp.zeros((kv_lora, heads, HEAD_LANES - MLA_NOPE), F32)
    wk = jnp.concatenate([wkv[:, :, :MLA_NOPE], zk], axis=2).reshape(kv_lora, heads * HEAD_LANES).astype(BF16)
    zv = jnp.zeros((kv_lora, heads, HEAD_LANES - MLA_V), F32)
    wv = jnp.concatenate([wkv[:, :, MLA_NOPE:], zv], axis=2).reshape(kv_lora, heads * HEAD_LANES).astype(BF16)
    wo = w_out[:heads * MLA_V].reshape(heads, MLA_V, d)
    wo_a = jnp.concatenate([wo, jnp.zeros((heads, HEAD_LANES - MLA_V, d), F32)],
                           axis=1).reshape(heads * HEAD_LANES, d).astype(BF16)
    wo_h = w_out[heads * MLA_V:].astype(BF16)
    return dict(win=win, wqa=wqa, wqb=wqb, wk=wk, wv=wv, wo_a=wo_a, wo_h=wo_h, heads=heads)


def _rope_tables(n):
    rows = n // GRID_W
    row = jnp.repeat(jnp.arange(rows), GRID_W).astype(F32)
    col = jnp.tile(jnp.arange(GRID_W), rows).astype(F32)
    inv = ROPE_BASE ** (-jnp.arange(ROPE_NF, dtype=F32) / ROPE_NF)
    ar = row[:, None] * inv
    ac = col[:, None] * inv
    ones = jnp.ones((n, MLA_NOPE), F32)
    tail = HEAD_LANES - MLA_NOPE - MLA_ROPE
    cos = jnp.concatenate([ones, jnp.cos(ar), jnp.cos(ar), jnp.cos(ac), jnp.cos(ac),
                           jnp.ones((n, tail), F32)], axis=1)
    sin = jnp.concatenate([0 * ones, jnp.sin(ar), jnp.sin(ar), jnp.sin(ac), jnp.sin(ac),
                           jnp.zeros((n, tail), F32)], axis=1)
    return cos, sin


def _halo_specs(tm, n, d):
    nh = n // HALO
    tpb = tm // HALO
    return [pl.BlockSpec((None, tm, d), lambda bi, i: (bi, i, 0)),
            pl.BlockSpec((None, HALO, d), lambda bi, i: (bi, jnp.maximum(i * tpb - 1, 0), 0)),
            pl.BlockSpec((None, HALO, d), lambda bi, i: (bi, jnp.minimum((i + 1) * tpb, nh - 1), 0))]


def _ab_project(x, mods, row_fn, norm_g, w, q_norm_g, kv_norm_g, rope_tabs, conv_w, conv_b, ch):
    b, n, d = x.shape
    tm = min(ROW_TILE, n)
    heads = w["heads"]
    q_lora = q_norm_g.shape[-1]
    kv_lora = kv_norm_g.shape[-1]
    hw = heads * HEAD_LANES
    hy_cols = conv_w.shape[-1]
    rope = rope_tabs is not None
    tok = lambda width: pl.BlockSpec((None, tm, width), lambda bi, i: (bi, i, 0))
    in_specs = _halo_specs(tm, n, d) + [
        _mod_spec(row_fn, 0, d), _mod_spec(row_fn, 1, d), _resident((1, d)),
        _resident(w["win"].shape), _resident((1, q_lora)), _resident(w["wqa"].shape)]
    args = [x, x, x, mods, mods, norm_g.reshape(1, d), w["win"], q_norm_g.reshape(1, q_lora), w["wqa"]]
    if rope:
        in_specs.append(_resident(w["wqb"].shape))
        args.append(w["wqb"])
    in_specs += [_resident((1, kv_lora)), _resident(w["wk"].shape), _resident(w["wv"].shape),
                 _resident(conv_w.shape), _resident((1, hy_cols))]
    args += [kv_norm_g.reshape(1, kv_lora), w["wk"], w["wv"], conv_w, conv_b.reshape(1, hy_cols)]
    if rope:
        in_specs += [pl.BlockSpec((tm, HEAD_LANES), lambda bi, i: (i, 0))] * 2
        args += list(rope_tabs)
    kern = functools.partial(_ab_proj_kernel, rope=rope, q_lora=q_lora, kv_lora=kv_lora, heads=heads, ch=ch,
                             scale=math.log2(math.e) / math.sqrt(MLA_NOPE + MLA_ROPE))
    return pl.pallas_call(
        kern,
        grid=(b, n // tm),
        in_specs=in_specs,
        out_specs=[tok(hw), tok(hw), tok(hw), tok(hy_cols), tok(ch)],
        out_shape=[jax.ShapeDtypeStruct((b, n, hw), BF16)] * 3 + [jax.ShapeDtypeStruct((b, n, hy_cols), F32),
                                                                   jax.ShapeDtypeStruct((b, n, ch), BF16)],
        compiler_params=_cparams("parallel", "parallel"),
        name="ab_project",
    )(*args)


def _attn_kernel(*refs, sub, nseg):
    q_ref, o_ref = refs[0], refs[-1]
    ks = [r[...] for r in refs[1:1 + nseg]]
    vs = [r[...] for r in refs[1 + nseg:1 + 2 * nseg]]
    for r in range(0, q_ref.shape[0], sub):
        q = q_ref[r:r + sub, :]
        ss = [_dot(q, k, _NT) for k in ks]
        m = functools.reduce(jnp.maximum, [jnp.max(s, axis=-1, keepdims=True) for s in ss])
        ps = [jnp.exp2(s - m) for s in ss]
        l = sum(jnp.sum(p, axis=-1, keepdims=True) for p in ps)
        o = sum(_dot(p.astype(BF16), v) for p, v in zip(ps, vs))
        o_ref[r:r + sub, :] = (o / l).astype(o_ref.dtype)


def _attention(q, ks, vs, heads):
    b, n, hw = q.shape
    tq = min(ATTN_TQ, n)
    tile = pl.BlockSpec((None, tq, HEAD_LANES), lambda bi, h, i: (bi, i, h))
    whole = lambda a: pl.BlockSpec((None, a.shape[1], HEAD_LANES), lambda bi, h, i: (bi, 0, h))
    return pl.pallas_call(
        functools.partial(_attn_kernel, sub=min(ATTN_SUB, tq), nseg=len(ks)),
        grid=(b, heads, n // tq),
        in_specs=[tile] + [whole(a) for a in ks] + [whole(a) for a in vs],
        out_specs=tile,
        out_shape=jax.ShapeDtypeStruct((b, n, hw), BF16),
        compiler_params=_cparams("parallel", "parallel", "parallel"),
        name="attention",
    )(q, *ks, *vs)


def _hy_filter_kernel(z_ref, t_ref, w1_ref, b1_ref, w2_ref, b2_ref, w3_ref, dl_ref, o_ref, *, ch):
    hdot = lambda a, w: jnp.dot(a, w, precision=HIGHEST, preferred_element_type=F32)
    a = jnp.sin(HY_SIN_FREQ * (hdot(z_ref[...], w1_ref[...]) + b1_ref[...]))
    a = jnp.sin(HY_SIN_FREQ * (hdot(a, w2_ref[...]) + b2_ref[...]))
    f = hdot(a, w3_ref[...])
    dec = jnp.exp(-t_ref[...] * dl_ref[...])
    tl = z_ref.shape[0]
    row = pl.program_id(0) * tl + lax.broadcasted_iota(jnp.int32, (tl, ch), 0)
    for o in range(HY_ORDER):
        hf = f[:, (2 * o) * ch:(2 * o + 1) * ch] * dec
        hb = jnp.where(row > 0, f[:, (2 * o + 1) * ch:(2 * o + 2) * ch] * dec, 0.0)
        o_ref[:, (2 * o) * ch:(2 * o + 1) * ch] = (hf + hb).astype(BF16)
        o_ref[:, (2 * o + 1) * ch:(2 * o + 2) * ch] = (hf - hb).astype(BF16)


def _hy_filter_inputs(length, w1, b1, w2, b2, w3, ch):
    t = jnp.linspace(0.0, 1.0, length, dtype=F32)[:, None]
    wv = 2.0 * math.pi * jnp.arange(length, dtype=F32)[:, None] / length
    f = jnp.linspace(1e-4, HY_BANDS - 1, HY_BANDS, dtype=F32)
    z = jnp.concatenate([t, jnp.cos(f * wv), -jnp.sin(f * wv)], axis=-1)
    emb = z.shape[1]
    z = jnp.pad(z, ((0, 0), (0, HEAD_LANES - emb)))
    w1p = jnp.pad(w1, ((0, HEAD_LANES - emb), (0, 0)))
    deltas = jnp.abs(jnp.linspace(HY_MIN_DECAY, HY_MAX_DECAY, ch, dtype=F32))[None, :]
    fw = w1.shape[1]
    tl = min(512, length)
    return pl.pallas_call(
        functools.partial(_hy_filter_kernel, ch=ch),
        grid=(length // tl,),
        in_specs=[pl.BlockSpec((tl, HEAD_LANES), lambda i: (i, 0)), pl.BlockSpec((tl, 1), lambda i: (i, 0)),
                  _resident((HEAD_LANES, fw)), _resident((1, fw)), _resident((fw, fw)), _resident((1, fw)),
                  _resident(w3.shape), _resident((1, ch))],
        out_specs=pl.BlockSpec((tl, 2 * HY_ORDER * ch), lambda i: (i, 0)),
        out_shape=jax.ShapeDtypeStruct((length, 2 * HY_ORDER * ch), BF16),
        compiler_params=_cparams("parallel"),
        name="hy_filter",
    )(z, t, w1p, b1.reshape(1, fw), w2, b2.reshape(1, fw), w3, deltas)


def _dft_tables(length):
    n2 = 2 * length
    theta = 2.0 * math.pi / n2
    n = jnp.arange(length, dtype=jnp.int32)[None, :]
    a = 8 * jnp.arange(length // 8, dtype=jnp.int32)[:, None]
    b = jnp.arange(8, dtype=jnp.int32)[:, None]
    ang_a = ((a * n) % n2).astype(F32) * theta
    ang_b = ((b * n) % n2).astype(F32) * theta
    ca, sa = jnp.cos(ang_a)[:, None, :], jnp.sin(ang_a)[:, None, :]
    cb, sb = jnp.cos(ang_b)[None, :, :], jnp.sin(ang_b)[None, :, :]
    fc = (ca * cb - sa * sb).reshape(length, length)
    fs = -(sa * cb + ca * sb).reshape(length, length)
    return fc.astype(BF16), fs.astype(BF16)


def _alternating_sum(x):
    col = lax.broadcasted_iota(jnp.int32, (8, x.shape[0]), 1)
    alt = (1 - 2 * (col & 1)).astype(F32).astype(BF16)
    return _dot(alt, x)[0:1]


def _filter_dft_kernel(hs_ref, hd_ref, fc_ref, fs_ref, o_ref):
    hs = hs_ref[...]
    o_ref[0] = _dot(fc_ref[...], hs)
    ki = _dot(fs_ref[...], hd_ref[...])
    nyq = _alternating_sum(hs)
    tf = ki.shape[0]
    row = pl.program_id(1) * tf + lax.broadcasted_iota(jnp.int32, (tf, 1), 0)
    o_ref[1] = jnp.where(row == 0, nyq, ki)


def _dft_spec_kernel(x_ref, fc_ref, fs_ref, k_ref, o_ref, *, inv_n):
    x = x_ref[...]
    ur = _dot(fc_ref[...], x)
    kr = k_ref[0]
    ki = k_ref[1]
    tf = ur.shape[0]
    row = pl.program_id(1) * tf + lax.broadcasted_iota(jnp.int32, (tf, 1), 0)
    first = row == 0
    ui = jnp.where(first, _alternating_sum(x), _dot(fs_ref[...], x))
    uiki = ui * ki
    yr = ur * kr - jnp.where(first, 0.0, uiki)
    yi = jnp.where(first, uiki, ur * ki + ui * kr)
    scale = jnp.where(first, inv_n, 2.0 * inv_n)
    o_ref[0] = (yr * scale).astype(BF16)
    o_ref[1] = (yi * scale).astype(BF16)


def _filter_spectrum(p, fc, fs, ch):
    length, cols = p.shape
    tf = min(512, length)
    return pl.pallas_call(
        _filter_dft_kernel,
        grid=(HY_ORDER, length // tf),
        in_specs=[pl.BlockSpec((length, ch), lambda o, i: (0, 2 * o)),
                  pl.BlockSpec((length, ch), lambda o, i: (0, 2 * o + 1)),
                  pl.BlockSpec((tf, length), lambda o, i: (i, 0)),
                  pl.BlockSpec((tf, length), lambda o, i: (i, 0))],
        out_specs=pl.BlockSpec((None, 2, tf, ch), lambda o, i: (o, 0, i, 0)),
        out_shape=jax.ShapeDtypeStruct((HY_ORDER, 2, length, ch), F32),
        compiler_params=_cparams("parallel", "arbitrary"),
        name="hy_filter_dft",
    )(p, p, fc, fs)


def _hy_spectral(xb, col, ch, fc, fs, kspec):
    b, length, _ = xb.shape
    tf = min(512, length)
    return pl.pallas_call(
        functools.partial(_dft_spec_kernel, inv_n=1.0 / (2 * length)),
        grid=(b, length // tf),
        in_specs=[pl.BlockSpec((None, length, ch), lambda bi, i: (bi, 0, col)),
                  pl.BlockSpec((tf, length), lambda bi, i: (i, 0)),
                  pl.BlockSpec((tf, length), lambda bi, i: (i, 0)),
                  pl.BlockSpec((2, tf, ch), lambda bi, i: (0, i, 0))],
        out_specs=pl.BlockSpec((None, 2, tf, ch), lambda bi, i: (bi, 0, i, 0)),
        out_shape=jax.ShapeDtypeStruct((b, 2, length, ch), BF16),
        compiler_params=_cparams("parallel", "arbitrary"),
        name="hy_dft_spectral",
    )(xb, fc, fs, kspec)


def _idft_gate_kernel(y_ref, fc_ref, fs_ref, u_ref, xg_ref, d_ref, o_ref, ob_ref):
    tt = fc_ref.shape[0]
    t = pl.program_id(1) * tt + lax.broadcasted_iota(jnp.int32, (tt, 1), 0)
    sign = (1 - 2 * (t & 1)).astype(F32)
    conv = (_dot(fc_ref[...], y_ref[0]) + _dot(fs_ref[...], y_ref[1])
            + sign * y_ref[1, 0:1, :].astype(F32))
    out = xg_ref[...] * (conv + u_ref[...] * d_ref[...])
    o_ref[...] = out
    ob_ref[...] = out.astype(BF16)


def _hy_inverse_gate(y, fc, fs, u, u_col, xg, xg_col, dvec, ch):
    b, _, length, _ = y.shape
    tt = min(512, length)
    out = pl.BlockSpec((None, tt, ch), lambda bi, i: (bi, i, 0))
    return pl.pallas_call(
        _idft_gate_kernel,
        grid=(b, length // tt),
        in_specs=[pl.BlockSpec((None, 2, length, ch), lambda bi, i: (bi, 0, 0, 0)),
                  pl.BlockSpec((tt, length), lambda bi, i: (i, 0)),
                  pl.BlockSpec((tt, length), lambda bi, i: (i, 0)),
                  pl.BlockSpec((None, tt, ch), lambda bi, i: (bi, i, u_col)),
                  pl.BlockSpec((None, tt, ch), lambda bi, i: (bi, i, xg_col)),
                  pl.BlockSpec((1, ch), lambda bi, i: (0, 0))],
        out_specs=[out, out],
        out_shape=[jax.ShapeDtypeStruct((b, length, ch), F32), jax.ShapeDtypeStruct((b, length, ch), BF16)],
        compiler_params=_cparams("parallel", "arbitrary"),
        name="hy_idft_gate",
    )(y, fc, fs, u, xg, dvec.reshape(1, ch))


def _hyena(hyc, hyb, w1, b1, w2, b2, w3, hy_bias):
    ch = hy_bias.shape[-1]
    length = hyc.shape[1]
    fc, fs = _dft_tables(length)
    kspec = _filter_spectrum(_hy_filter_inputs(length, w1, b1, w2, b2, w3, ch), fc, fs, ch)
    y = _hy_spectral(hyb, 0, ch, fc, fs, kspec[0])
    y1, y1b = _hy_inverse_gate(y, fc, fs, hyc, 0, hyc, 1, hy_bias[0], ch)
    y = _hy_spectral(y1b, 0, ch, fc, fs, kspec[1])
    y2, _ = _hy_inverse_gate(y, fc, fs, y1, 0, hyc, 2, hy_bias[1], ch)
    return y2


def _ab_out_kernel(o_ref, y_ref, x_ref, g1_ref, woa_ref, woh_ref, out_ref):
    y = _dot(o_ref[...], woa_ref[...]) + _dot(y_ref[...].astype(BF16), woh_ref[...])
    out_ref[...] = x_ref[...] + g1_ref[...] * y


def _ab_out(o, y2, x, mods, row_fn, w):
    b, n, d = x.shape
    tm = min(ROW_TILE, n)
    tok = lambda width: pl.BlockSpec((None, tm, width), lambda bi, i: (bi, i, 0))
    return pl.pallas_call(
        _ab_out_kernel,
        grid=(b, n // tm),
        in_specs=[tok(o.shape[-1]), tok(y2.shape[-1]), tok(d), _mod_spec(row_fn, 2, d),
                  _resident(w["wo_a"].shape), _resident(w["wo_h"].shape)],
        out_specs=tok(d),
        out_shape=jax.ShapeDtypeStruct((b, n, d), F32),
        compiler_params=_cparams("parallel", "parallel"),
        name="ab_out",
    )(o, y2, x, mods, w["wo_a"], w["wo_h"])


FFN_ROWS = 512
FFN_CHUNK_MAX = 1408


def _ffn_kernel(*refs, hidden, hc, final):
    if final:
        (x_ref, xp_ref, xn_ref, sh_ref, sc_ref, g2_ref, ng_ref, wup_ref, cw_ref, cb_ref, wdn_ref,
         fg_ref, o_ref) = refs
    else:
        (x_ref, xp_ref, xn_ref, sh_ref, sc_ref, g2_ref, ng_ref, wup_ref, cw_ref, cb_ref, wdn_ref,
         o_ref) = refs
    x = x_ref[...]
    tm, d = x.shape
    he = _halo_rows(x_ref, xp_ref, xn_ref, lambda a: _modulate(a, ng_ref[...], sh_ref[...], sc_ref[...]))

    def conv(a, col):
        return _conv3_rows(a, cw_ref[:, col:col + hc], cb_ref[:, col:col + hc], tm)

    acc = jnp.zeros((tm, d), F32)
    for j in range(hidden // hc):
        gate = conv(_dot(he, wup_ref[:, j * hc:(j + 1) * hc]), j * hc)
        val = conv(_dot(he, wup_ref[:, hidden + j * hc:hidden + (j + 1) * hc]), hidden + j * hc)
        mid = (_silu(gate) * val).astype(BF16)
        acc = acc + _dot(mid, wdn_ref[j * hc:(j + 1) * hc, :])
    out = x + g2_ref[...] * acc
    if final:
        out = _rms(out, fg_ref[...])
    o_ref[...] = out


def _ffn_chunk(hidden):
    for hc in range(min(FFN_CHUNK_MAX, hidden) // HEAD_LANES * HEAD_LANES, 0, -HEAD_LANES):
        if hidden % hc == 0:
            return hc
    raise ValueError(f"FFN hidden width {hidden} is not a multiple of {HEAD_LANES}")


def _conv_ffn(x, mods, row_fn, norm_g, w_up, conv_w, conv_b, w_down, final_g=None):
    b, n, d = x.shape
    hidden = w_down.shape[0]
    tm = min(FFN_ROWS, n)
    final = final_g is not None
    tok = pl.BlockSpec((None, tm, d), lambda bi, i: (bi, i, 0))
    in_specs = _halo_specs(tm, n, d) + [
        _mod_spec(row_fn, 3, d), _mod_spec(row_fn, 4, d), _mod_spec(row_fn, 5, d),
        _resident((1, d)), _resident(w_up.shape), _resident(conv_w.shape),
        _resident((1, 2 * hidden)), _resident(w_down.shape)]
    args = [x, x, x, mods, mods, mods, norm_g.reshape(1, d), w_up, conv_w, conv_b.reshape(1, 2 * hidden), w_down]
    if final:
        in_specs.append(_resident((1, d)))
        args.append(final_g.reshape(1, d))
    return pl.pallas_call(
        functools.partial(_ffn_kernel, hidden=hidden, hc=_ffn_chunk(hidden), final=final),
        grid=(b, n // tm),
        in_specs=in_specs,
        out_specs=tok,
        out_shape=jax.ShapeDtypeStruct((b, n, d), F32),
        compiler_params=_cparams("parallel", "parallel"),
        name="conv_ffn",
    )(*args)


def _hg_proj_kernel(x_ref, sh_ref, sc_ref, g_ref, w_ref, o_ref, *, width):
    h = _modulate(x_ref[...], g_ref[...], sh_ref[...], sc_ref[...]).astype(BF16)
    z = _dot(h, w_ref[...])
    o_ref[:, :width] = _silu(z[:, :width])
    o_ref[:, width:] = z[:, width:]


def _hg_project(x, mods, row_fn, norm_g, w_in):
    b, n, d = x.shape
    cols = w_in.shape[1]
    tm = min(ROW_TILE, n)
    tok = lambda width: pl.BlockSpec((None, tm, width), lambda bi, i: (bi, i, 0))
    return pl.pallas_call(
        functools.partial(_hg_proj_kernel, width=cols // 5),
        grid=(b, n // tm),
        in_specs=[tok(d), _mod_spec(row_fn, 0, d), _mod_spec(row_fn, 1, d), _resident((1, d)),
                  _resident(w_in.shape)],
        out_specs=tok(cols),
        out_shape=jax.ShapeDtypeStruct((b, n, cols), F32),
        compiler_params=_cparams("parallel", "parallel"),
        name="hg_project",
    )(x, mods, mods, norm_g.reshape(1, d), w_in)


def _scan_tables(direction):
    c = HG_CHUNK
    idx = np.arange(c)
    level = np.full((c, c), -1, np.int32)
    for li, m in enumerate(HG_LEVELS):
        same = (idx[:, None] // (2 * m)) == (idx[None, :] // (2 * m))
        up = (idx % (2 * m)) >= m
        pair = same & (up[:, None] & ~up[None, :] if direction == 0 else ~up[:, None] & up[None, :])
        level[pair] = li
    level[idx, idx] = len(HG_LEVELS)
    if direction == 0:
        to_q = (idx[None, :] <= idx[:, None])
    else:
        to_q = (idx[None, :] >= idx[:, None])
    em = np.concatenate([to_q.astype(np.float32), np.ones((8, c), np.float32)], axis=0)
    return jnp.asarray(np.concatenate([em, em], axis=1), BF16), jnp.asarray(level)


def _level_exponent(run, m, rev):
    c, w = run.shape
    off = m if rev else m - 1
    if 2 * m >= 8:
        parts = [jnp.broadcast_to(run[a + off:a + off + 1, :], (2 * m, w)) for a in range(0, c, 2 * m)]
    elif m == 2:
        low = lax.broadcasted_iota(jnp.int32, (8, w), 0) < 4
        parts = [jnp.where(low, jnp.broadcast_to(run[a + off:a + off + 1, :], (8, w)),
                           jnp.broadcast_to(run[a + 4 + off:a + 5 + off, :], (8, w))) for a in range(0, c, 8)]
    else:
        odd = (lax.broadcasted_iota(jnp.int32, (c, w), 0) & 1) == 1
        if rev:
            parts = [jnp.where(odd, run, pltpu.roll(run, c - 1, axis=0))]
        else:
            parts = [jnp.where(odd, pltpu.roll(run, 1, axis=0), run)]
    ref = parts[0] if len(parts) == 1 else jnp.concatenate(parts, axis=0)
    return -jnp.abs(run - ref)


def _gla_group(chunks, lb, em2, lv, st, rev):
    c = HG_CHUNK
    e_ = HG_EXPAND
    nl = len(HG_LEVELS)
    log_lb = jnp.log(lb)
    log_1m = jnp.log1p(-lb)
    log_f = []
    keys = []
    for _, fr, _ in chunks:
        t = jnp.exp(-jnp.abs(fr))
        d = 1.0 + t
        keys.append((1.0 - lb) * (jnp.where(fr > 0, t, 1.0) / d))
        b = log_1m + (jnp.minimum(fr, 0.0) - jnp.log(d))
        log_f.append(jnp.maximum(log_lb, b) + jnp.log(1.0 + jnp.exp(-jnp.abs(log_lb - b))))
    g = jnp.concatenate(log_f, axis=1) * math.log2(math.e)
    hi = g.astype(BF16)
    lo = (g - hi.astype(F32)).astype(BF16)
    e_all = _dot(em2, jnp.concatenate([hi, lo], axis=0))
    outs = []
    for gi, (q, _, val) in enumerate(chunks):
        to_q = e_all[:c, gi * e_:(gi + 1) * e_]
        total = e_all[c:c + 1, gi * e_:(gi + 1) * e_]
        key = keys[gi]
        att = jnp.zeros((c, c), F32)
        for li, m in enumerate(HG_LEVELS):
            dec = jnp.exp2(_level_exponent(to_q, m, rev))
            att = jnp.where(lv == li, _dot((q * dec).astype(BF16), (key * dec).astype(BF16), _NT), att)
        att = jnp.where(lv == nl, jnp.sum(q * key, axis=1, keepdims=True), att)
        vb = val.astype(BF16)
        q_in = (q * jnp.exp2(to_q)).astype(BF16)
        outs.append(_dot(att.astype(BF16), vb) + _dot(q_in, st.astype(BF16), _NT))
        k_out = (key * jnp.exp2(total - to_q)).astype(BF16)
        st = st * jnp.exp2(total) + _dot(vb, k_out, _TN)
    return outs, st


def _hg_scan_kernel(*refs, layer, need_ctx):
    (lg_ref, emf_ref, emb_ref, lvf_ref, lvb_ref,
     qc_ref, ffc_ref, fbc_ref, ic_ref, q_ref, ff_ref, fb_ref, i_ref) = refs[:13]
    if need_ctx:
        o_ref, oc_ref, st_ref = refs[13:]
    else:
        o_ref, st_ref = refs[13:]
        oc_ref = None
    c = HG_CHUNK
    lg = lg_ref[...]
    ex = jnp.exp(lg - jnp.max(lg, axis=0, keepdims=True))
    probs = ex / jnp.sum(ex, axis=0, keepdims=True)
    acc = probs[0]
    for l in range(1, layer + 1):
        acc = acc + probs[l]
    lb_all = acc - probs[0]
    lb_f = lb_all[0:1]
    lb_b = lb_all[1:2]

    st_ref[...] = jnp.zeros_like(st_ref)
    o_ref[...] = jnp.zeros_like(o_ref)
    if need_ctx:
        oc_ref[...] = jnp.zeros_like(oc_ref)

    def run(qr, ffr, fbr, ir, outr):
        nc = qr.shape[0] // c
        grp = math.gcd(nc, HG_GROUP)
        ng = nc // grp

        def body(gi, carry):
            rows_f = [pl.multiple_of((gi * grp + k) * c, c) for k in range(grp)]
            rows_b = [pl.multiple_of((nc - 1 - gi * grp - k) * c, c) for k in range(grp)]
            load = lambda gate_ref, r: (qr[pl.ds(r, c), :], gate_ref[pl.ds(r, c), :], ir[pl.ds(r, c), :])
            of, stf = _gla_group([load(ffr, r) for r in rows_f], lb_f, emf_ref[...], lvf_ref[...], st_ref[0], False)
            st_ref[0] = stf
            ob, stb = _gla_group([load(fbr, r) for r in rows_b], lb_b, emb_ref[...], lvb_ref[...], st_ref[1], True)
            st_ref[1] = stb
            if outr is not None:
                for r, o in zip(rows_f + rows_b, of + ob):
                    outr[pl.ds(r, c), :] += o
            return carry

        lax.fori_loop(0, ng, body, 0, unroll=min(HG_UNROLL, ng))

    run(qc_ref, ffc_ref, fbc_ref, ic_ref, oc_ref)
    run(q_ref, ff_ref, fb_ref, i_ref, o_ref)


def _hg_scan(z, zc, lb_logits, layer, need_ctx):
    b, n, cols = z.shape
    nc_ = zc.shape[1]
    width = cols // 5
    heads = width // HG_EXPAND
    e = HG_EXPAND
    depth = lb_logits.shape[0]
    emf, lvf = _scan_tables(0)
    emb, lvb = _scan_tables(1)
    col = lambda rows, group: pl.BlockSpec((None, rows, e), lambda bi, h: (bi, 0, group * heads + h))
    in_specs = [pl.BlockSpec((depth, 2, e), lambda bi, h: (0, 0, h)),
                _resident(emf.shape), _resident(emb.shape), _resident(lvf.shape), _resident(lvb.shape),
                col(nc_, 0), col(nc_, 1), col(nc_, 2), col(nc_, 3),
                col(n, 0), col(n, 1), col(n, 2), col(n, 3)]
    out_specs = [pl.BlockSpec((None, n, e), lambda bi, h: (bi, 0, h))]
    out_shape = [jax.ShapeDtypeStruct((b, n, width), F32)]
    if need_ctx:
        out_specs.append(pl.BlockSpec((None, nc_, e), lambda bi, h: (bi, 0, h)))
        out_shape.append(jax.ShapeDtypeStruct((b, nc_, width), F32))
    res = pl.pallas_call(
        functools.partial(_hg_scan_kernel, layer=layer, need_ctx=need_ctx),
        grid=(b, heads),
        in_specs=in_specs,
        out_specs=out_specs,
        out_shape=out_shape,
        scratch_shapes=[pltpu.VMEM((2, e, e), F32)],
        compiler_params=_cparams("parallel", "parallel"),
        name="hg_scan",
    )(lb_logits, emf, emb, lvf, lvb, zc, zc, zc, zc, z, z, z, z)
    return (res[0], res[1]) if need_ctx else (res[0], None)


def _hg_out_kernel(o_ref, g_ref, x_ref, g1_ref, ng_ref, w_ref, out_ref, *, heads):
    ng = ng_ref[...]
    parts = [_rms(o_ref[:, h * HG_EXPAND:(h + 1) * HG_EXPAND], ng) for h in range(heads)]
    o = jnp.concatenate(parts, axis=1) * _silu(g_ref[...])
    out_ref[...] = x_ref[...] + g1_ref[...] * _dot(o.astype(BF16), w_ref[...])


def _hg_out(o, z, x, mods, row_fn, norm_g, w_out):
    b, n, d = x.shape
    width = o.shape[-1]
    tm = min(ROW_TILE, n)
    tok = lambda wd: pl.BlockSpec((None, tm, wd), lambda bi, i: (bi, i, 0))
    return pl.pallas_call(
        functools.partial(_hg_out_kernel, heads=width // HG_EXPAND),
        grid=(b, n // tm),
        in_specs=[tok(width), pl.BlockSpec((None, tm, width), lambda bi, i: (bi, i, 4)), tok(d),
                  _mod_spec(row_fn, 2, d), _resident((1, HG_EXPAND)), _resident(w_out.shape)],
        out_specs=tok(d),
        out_shape=jax.ShapeDtypeStruct((b, n, d), F32),
        compiler_params=_cparams("parallel", "parallel"),
        name="hg_out",
    )(o, z, x, mods, norm_g.reshape(1, HG_EXPAND), w_out)


def kernel(x, c, ctx, c_ctx, mod_w, mod_b, norm1_g, norm2_g, ffn_w_up, ffn_conv_w, ffn_conv_b, ffn_w_down, ab_w_in, mla_q_norm_g, mla_w_q_b, mla_kv_norm_g, mla_w_kv_b, hy_conv_w, hy_conv_b, hy_w1, hy_b1, hy_w2, hy_b2, hy_w3, hy_bias, ab_w_out, hg_w_in, hg_lb_logits, hg_norm_g, hg_w_out, final_norm_g):
    batch, n, d = x.shape
    depth = mod_w.shape[0]
    assert batch < MOD_ROWS
    ctx_row = batch
    cc = jnp.zeros((MOD_ROWS, d), F32).at[:batch].set(c).at[ctx_row].set(c_ctx)
    mods_all = _mod_vectors(cc, mod_w, mod_b)
    lat_row = lambda bi, *_: bi
    ctx_row_fn = lambda bi, *_: ctx_row
    rope_tabs = _rope_tables(n)

    xc = ctx
    for layer in range(depth):
        need_ctx = layer < depth - 1
        j = layer // 2
        mods = mods_all[layer]
        if layer % 2 == 0:
            q_lora = mla_q_norm_g.shape[-1]
            kv_lora = mla_kv_norm_g.shape[-1]
            hy_ch = hy_bias.shape[-1]
            w = _ab_weights(ab_w_in[j], mla_w_q_b[j], mla_w_kv_b[j], ab_w_out[j], q_lora, kv_lora, hy_ch)
            hy_args = (hy_w1[j], hy_b1[j], hy_w2[j], hy_b2[j], hy_w3[j], hy_bias[j])
            q_c, k_c, v_c, hy_c, hyb_c = _ab_project(xc, mods, ctx_row_fn, norm1_g[layer], w, mla_q_norm_g[j],
                                                     mla_kv_norm_g[j], None, hy_conv_w[j], hy_conv_b[j], hy_ch)
            q, k, v, hy, hyb = _ab_project(x, mods, lat_row, norm1_g[layer], w, mla_q_norm_g[j],
                                           mla_kv_norm_g[j], rope_tabs, hy_conv_w[j], hy_conv_b[j], hy_ch)
            o = _attention(q, [jnp.concatenate([k_c, k], axis=1)], [jnp.concatenate([v_c, v], axis=1)], w["heads"])
            x_new = _ab_out(o, _hyena(hy, hyb, *hy_args), x, mods, lat_row, w)
            if need_ctx:
                o_c = _attention(q_c, [k_c], [v_c], w["heads"])
                xc = _ab_out(o_c, _hyena(hy_c, hyb_c, *hy_args), xc, mods, ctx_row_fn, w)
            x = x_new
        else:
            w_in = hg_w_in[j].astype(BF16)
            z_c = _hg_project(xc, mods, ctx_row_fn, norm1_g[layer], w_in)
            z = _hg_project(x, mods, lat_row, norm1_g[layer], w_in)
            o, o_c = _hg_scan(z, z_c, hg_lb_logits, layer, need_ctx)
            w_out = hg_w_out[j].astype(BF16)
            x = _hg_out(o, z, x, mods, lat_row, hg_norm_g[j], w_out)
            if need_ctx:
                xc = _hg_out(o_c, z_c, xc, mods, ctx_row_fn, hg_norm_g[j], w_out)
        w_up = ffn_w_up[layer].astype(BF16)
        w_dn = ffn_w_down[layer].astype(BF16)
        last = layer == depth - 1
        x = _conv_ffn(x, mods, lat_row, norm2_g[layer], w_up, ffn_conv_w[layer], ffn_conv_b[layer], w_dn,
                      final_norm_g if last else None)
        if need_ctx:
            xc = _conv_ffn(xc, mods, ctx_row_fn, norm2_g[layer], w_up, ffn_conv_w[layer], ffn_conv_b[layer], w_dn)
    return x
```

```python
import functools
import math

import numpy as np
import jax
import jax.numpy as jnp
from jax import lax
from jax.experimental import pallas as pl
from jax.experimental.pallas import tpu as pltpu

F32 = jnp.float32
BF16 = jnp.bfloat16
HIGHEST = lax.Precision.HIGHEST

EPS = 1e-6
GRID_W = 64

MLA_NOPE = 64
MLA_ROPE = 32
MLA_V = 64
ROPE_NF = MLA_ROPE // 4
ROPE_BASE = 10000.0
HEAD_LANES = 128
ATTN_TQ = 2048
ATTN_SUB = 256

HY_ORDER = 2
HY_BANDS = 16
HY_SIN_FREQ = 1.0
HY_DECAY_TARGET = 1e-2
HY_FAST_DECAY = 0.3
HY_SLOW_DECAY = 1.5
HY_MIN_DECAY = math.log(HY_DECAY_TARGET) / HY_SLOW_DECAY
HY_MAX_DECAY = math.log(HY_DECAY_TARGET) / HY_FAST_DECAY

HG_EXPAND = 128
HG_CHUNK = 64
HG_LEVELS = (32, 16, 8, 4, 2, 1)
HG_GROUP = 4
HG_UNROLL = 2

ROW_TILE = 512
MOD_ROWS = 16
HALO = 8
VMEM_LIMIT = 56 * 1024 * 1024

_NT = (((1,), (1,)), ((), ()))
_TN = (((0,), (0,)), ((), ()))


def _cparams(*sem):
    return pltpu.CompilerParams(dimension_semantics=sem, vmem_limit_bytes=VMEM_LIMIT)


def _dot(a, b, dims=(((1,), (0,)), ((), ()))):
    return lax.dot_general(a, b, dims, precision=lax.Precision.DEFAULT, preferred_element_type=F32)


def _rms(x, g):
    return x * lax.rsqrt(jnp.mean(x * x, axis=-1, keepdims=True) + EPS) * g


def _modulate(x, g, shift, scale):
    return _rms(x, g) * (1.0 + scale) + shift


def _silu(x):
    return x * jax.nn.sigmoid(x)


def _mod_spec(row_fn, k, d):
    return pl.BlockSpec((None, None, 1, d), lambda *idx: (row_fn(*idx), k, 0, 0))


def _resident(shape):
    nd = len(shape)
    return pl.BlockSpec(shape, lambda *idx: (0,) * nd, pipeline_mode=pl.Buffered(1))


def _mod_kernel(c_ref, w_ref, b_ref, o_ref):
    a = _silu(c_ref[...])
    o_ref[...] = jnp.dot(a, w_ref[...], precision=HIGHEST, preferred_element_type=F32) + b_ref[...]


def _mod_vectors(cc, mod_w, mod_b):
    depth, d, n6 = mod_w.shape
    tn = 512
    out = pl.pallas_call(
        _mod_kernel,
        grid=(depth, n6 // tn),
        in_specs=[pl.BlockSpec((MOD_ROWS, d), lambda l, j: (0, 0)),
                  pl.BlockSpec((None, d, tn), lambda l, j: (l, 0, j)),
                  pl.BlockSpec((None, 1, tn), lambda l, j: (l, 0, j))],
        out_specs=pl.BlockSpec((None, MOD_ROWS, tn), lambda l, j: (l, 0, j)),
        out_shape=jax.ShapeDtypeStruct((depth, MOD_ROWS, n6), F32),
        compiler_params=_cparams("parallel", "parallel"),
        name="mod_vectors",
    )(cc, mod_w, mod_b.reshape(depth, 1, n6))
    return out.reshape(depth, MOD_ROWS, 6, 1, d)


def _halo_rows(x_ref, xp_ref, xn_ref, mod):
    i = pl.program_id(1)
    last = pl.num_programs(1) - 1
    hp = mod(xp_ref[...]) * (i > 0).astype(F32)
    hn = mod(xn_ref[...]) * (i < last).astype(F32)
    return jnp.concatenate([hp, mod(x_ref[...]), hn], axis=0).astype(BF16)


def _conv3_rows(a, w, bias, tm):
    ne = tm + 2 * HALO
    prev = pltpu.roll(a, 1, axis=0)[HALO:HALO + tm]
    nxt = pltpu.roll(a, ne - 1, axis=0)[HALO:HALO + tm]
    return prev * w[0:1] + a[HALO:HALO + tm] * w[1:2] + nxt * w[2:3] + bias


def _ab_proj_kernel(*refs, rope, q_lora, kv_lora, heads, scale, ch, kv_aliased):
    if kv_aliased:
        refs = refs[:-7] + refs[-5:]
    if rope:
        (x_ref, xp_ref, xn_ref, sh_ref, sc_ref, g_ref, win_ref, qg_ref, wqa_ref, wqb_ref, kvg_ref, wk_ref,
         wv_ref, cw_ref, cb_ref, cos_ref, sin_ref, q_ref, k_ref, v_ref, hy_ref, hyb_ref) = refs
    else:
        (x_ref, xp_ref, xn_ref, sh_ref, sc_ref, g_ref, win_ref, qg_ref, wqa_ref, kvg_ref, wk_ref,
         wv_ref, cw_ref, cb_ref, q_ref, k_ref, v_ref, hy_ref, hyb_ref) = refs
    tm = x_ref.shape[0]
    he = _halo_rows(x_ref, xp_ref, xn_ref, lambda a: _modulate(a, g_ref[...], sh_ref[...], sc_ref[...]))
    ze = _dot(he, win_ref[...])
    o = q_lora + kv_lora
    hy = _conv3_rows(ze[:, o + 2 * HEAD_LANES:], cw_ref[...], cb_ref[...], tm)
    hy_ref[...] = hy
    hyb_ref[...] = hy[:, :ch].astype(BF16)
    z = ze[HALO:HALO + tm]
    qn = _rms(z[:, :q_lora], qg_ref[...]).astype(BF16)
    kvn = _rms(z[:, q_lora:o], kvg_ref[...]).astype(BF16)
    v_ref[...] = _dot(kvn, wv_ref[...]).astype(BF16)
    qa = _dot(qn, wqa_ref[...])
    kk = _dot(kvn, wk_ref[...])
    kr = z[:, o:o + HEAD_LANES]
    if rope:
        cos = cos_ref[...]
        sin = sin_ref[...]
        qb = _dot(qn, wqb_ref[...])
        kr = kr * cos + z[:, o + HEAD_LANES:o + 2 * HEAD_LANES] * sin
    for hh in range(heads):
        sl = slice(hh * HEAD_LANES, (hh + 1) * HEAD_LANES)
        qh = qa[:, sl]
        if rope:
            qh = qh * cos + qb[:, sl] * sin
        q_ref[:, sl] = (qh * scale).astype(BF16)
        k_ref[:, sl] = (kk[:, sl] + kr).astype(BF16)


def _rope_swap(w):
    nf = ROPE_NF
    return jnp.concatenate([-w[:, nf:2 * nf], w[:, :nf], -w[:, 3 * nf:4 * nf], w[:, 2 * nf:3 * nf]], axis=1)


def _ab_weights(w_in, w_q_b, w_kv_b, w_out, q_lora, kv_lora, hy_ch):
    d = w_in.shape[0]
    heads = w_q_b.shape[1] // (MLA_NOPE + MLA_ROPE)
    o = q_lora + kv_lora
    w_kr = w_in[:, o:o + MLA_ROPE]
    zpad = lambda n: jnp.zeros((d, n), F32)
    tail = HEAD_LANES - MLA_NOPE - MLA_ROPE
    win = jnp.concatenate(
        [w_in[:, :o],
         zpad(MLA_NOPE), w_kr, zpad(tail),
         zpad(MLA_NOPE), _rope_swap(w_kr), zpad(tail),
         w_in[:, o + MLA_ROPE:]], axis=1).astype(BF16)
    wq = w_q_b.reshape(q_lora, heads, MLA_NOPE + MLA_ROPE)
    zq = jnp.zeros((q_lora, heads, tail), F32)
    wqa = jnp.concatenate([wq, zq], axis=2).reshape(q_lora, heads * HEAD_LANES).astype(BF16)
    wq_rope = wq[:, :, MLA_NOPE:].reshape(q_lora * heads, MLA_ROPE)
    wqb = jnp.concatenate([jnp.zeros((q_lora, heads, MLA_NOPE), F32),
                           _rope_swap(wq_rope).reshape(q_lora, heads, MLA_ROPE), zq],
                          axis=2).reshape(q_lora, heads * HEAD_LANES).astype(BF16)
    wkv = w_kv_b.reshape(kv_lora, heads, MLA_NOPE + MLA_V)
    zk = jnp.zeros((kv_lora, heads, HEAD_LANES - MLA_NOPE), F32)
    wk = jnp.concatenate([wkv[:, :, :MLA_NOPE], zk], axis=2).reshape(kv_lora, heads * HEAD_LANES).astype(BF16)
    zv = jnp.zeros((kv_lora, heads, HEAD_LANES - MLA_V), F32)
    wv = jnp.concatenate([wkv[:, :, MLA_NOPE:], zv], axis=2).reshape(kv_lora, heads * HEAD_LANES).astype(BF16)
    wo = w_out[:heads * MLA_V].reshape(heads, MLA_V, d)
    wo_a = jnp.concatenate([wo, jnp.zeros((heads, HEAD_LANES - MLA_V, d), F32)],
                           axis=1).reshape(heads * HEAD_LANES, d).astype(BF16)
    wo_h = w_out[heads * MLA_V:].astype(BF16)
    return dict(win=win, wqa=wqa, wqb=wqb, wk=wk, wv=wv, wo_a=wo_a, wo_h=wo_h, heads=heads)


def _rope_tables(n):
    rows = n // GRID_W
    row = jnp.repeat(jnp.arange(rows), GRID_W).astype(F32)
    col = jnp.tile(jnp.arange(GRID_W), rows).astype(F32)
    inv = ROPE_BASE ** (-jnp.arange(ROPE_NF, dtype=F32) / ROPE_NF)
    ar = row[:, None] * inv
    ac = col[:, None] * inv
    ones = jnp.ones((n, MLA_NOPE), F32)
    tail = HEAD_LANES - MLA_NOPE - MLA_ROPE
    cos = jnp.concatenate([ones, jnp.cos(ar), jnp.cos(ar), jnp.cos(ac), jnp.cos(ac),
                           jnp.ones((n, tail), F32)], axis=1)
    sin = jnp.concatenate([0 * ones, jnp.sin(ar), jnp.sin(ar), jnp.sin(ac), jnp.sin(ac),
                           jnp.zeros((n, tail), F32)], axis=1)
    return cos, sin


def _halo_specs(tm, n, d):
    nh = n // HALO
    tpb = tm // HALO
    return [pl.BlockSpec((None, tm, d), lambda bi, i: (bi, i, 0)),
            pl.BlockSpec((None, HALO, d), lambda bi, i: (bi, jnp.maximum(i * tpb - 1, 0), 0)),
            pl.BlockSpec((None, HALO, d), lambda bi, i: (bi, jnp.minimum((i + 1) * tpb, nh - 1), 0))]


def _ab_project(x, mods, row_fn, norm_g, w, q_norm_g, kv_norm_g, rope_tabs, conv_w, conv_b, ch,
                kv_rows=None, kv_row0=0, kv_into=None):
    b, n, d = x.shape
    tm = min(ROW_TILE, n)
    kv_rows = kv_rows or n
    assert kv_row0 % tm == 0
    heads = w["heads"]
    q_lora = q_norm_g.shape[-1]
    kv_lora = kv_norm_g.shape[-1]
    hw = heads * HEAD_LANES
    hy_cols = conv_w.shape[-1]
    rope = rope_tabs is not None
    tok = lambda width: pl.BlockSpec((None, tm, width), lambda bi, i: (bi, i, 0))
    in_specs = _halo_specs(tm, n, d) + [
        _mod_spec(row_fn, 0, d), _mod_spec(row_fn, 1, d), _resident((1, d)),
        _resident(w["win"].shape), _resident((1, q_lora)), _resident(w["wqa"].shape)]
    args = [x, x, x, mods, mods, norm_g.reshape(1, d), w["win"], q_norm_g.reshape(1, q_lora), w["wqa"]]
    if rope:
        in_specs.append(_resident(w["wqb"].shape))
        args.append(w["wqb"])
    in_specs += [_resident((1, kv_lora)), _resident(w["wk"].shape), _resident(w["wv"].shape),
                 _resident(conv_w.shape), _resident((1, hy_cols))]
    args += [kv_norm_g.reshape(1, kv_lora), w["wk"], w["wv"], conv_w, conv_b.reshape(1, hy_cols)]
    if rope:
        in_specs += [pl.BlockSpec((tm, HEAD_LANES), lambda bi, i: (i, 0))] * 2
        args += list(rope_tabs)
    aliases = {}
    if kv_into is not None:
        aliases = {len(args): 1, len(args) + 1: 2}
        in_specs += [pl.BlockSpec(memory_space=pl.ANY)] * 2
        args += list(kv_into)
    kern = functools.partial(_ab_proj_kernel, rope=rope, q_lora=q_lora, kv_lora=kv_lora, heads=heads, ch=ch,
                             kv_aliased=kv_into is not None,
                             scale=math.log2(math.e) / math.sqrt(MLA_NOPE + MLA_ROPE))
    kv_spec = pl.BlockSpec((None, tm, hw), lambda bi, i: (bi, i + kv_row0 // tm, 0))
    kv_shape = jax.ShapeDtypeStruct((b, kv_rows, hw), BF16)
    return pl.pallas_call(
        kern,
        grid=(b, n // tm),
        in_specs=in_specs,
        out_specs=[tok(hw), kv_spec, kv_spec, tok(hy_cols), tok(ch)],
        out_shape=[jax.ShapeDtypeStruct((b, n, hw), BF16), kv_shape, kv_shape,
                   jax.ShapeDtypeStruct((b, n, hy_cols), F32), jax.ShapeDtypeStruct((b, n, ch), BF16)],
        input_output_aliases=aliases,
        compiler_params=_cparams("parallel", "parallel"),
        name="ab_project",
    )(*args)


def _attn_kernel(*refs, sub, nseg):
    q_ref, o_ref = refs[0], refs[-1]
    ks = [r[...] for r in refs[1:1 + nseg]]
    vs = [r[...] for r in refs[1 + nseg:1 + 2 * nseg]]
    for r in range(0, q_ref.shape[0], sub):
        q = q_ref[r:r + sub, :]
        ss = [_dot(q, k, _NT) for k in ks]
        m = functools.reduce(jnp.maximum, [jnp.max(s, axis=-1, keepdims=True) for s in ss])
        ps = [jnp.exp2(s - m) for s in ss]
        l = sum(jnp.sum(p, axis=-1, keepdims=True) for p in ps)
        o = sum(_dot(p.astype(BF16), v) for p, v in zip(ps, vs))
        o_ref[r:r + sub, :] = (o / l).astype(o_ref.dtype)


def _attention(q, ks, vs, heads, kv_rows=None, kv_row0=0):
    b, n, hw = q.shape
    tq = min(ATTN_TQ, n)
    tile = pl.BlockSpec((None, tq, HEAD_LANES), lambda bi, h, i: (bi, i, h))
    rows = lambda a: kv_rows or a.shape[1]
    assert all(kv_row0 % rows(a) == 0 for a in ks)
    whole = lambda a: pl.BlockSpec((None, rows(a), HEAD_LANES), lambda bi, h, i: (bi, kv_row0 // rows(a), h))
    return pl.pallas_call(
        functools.partial(_attn_kernel, sub=min(ATTN_SUB, tq), nseg=len(ks)),
        grid=(b, heads, n // tq),
        in_specs=[tile] + [whole(a) for a in ks] + [whole(a) for a in vs],
        out_specs=tile,
        out_shape=jax.ShapeDtypeStruct((b, n, hw), BF16),
        compiler_params=_cparams("parallel", "parallel", "parallel"),
        name="attention",
    )(q, *ks, *vs)


def _hy_filter_kernel(z_ref, t_ref, w1_ref, b1_ref, w2_ref, b2_ref, w3_ref, dl_ref, o_ref, *, ch):
    hdot = lambda a, w: jnp.dot(a, w, precision=HIGHEST, preferred_element_type=F32)
    a = jnp.sin(HY_SIN_FREQ * (hdot(z_ref[...], w1_ref[...]) + b1_ref[...]))
    a = jnp.sin(HY_SIN_FREQ * (hdot(a, w2_ref[...]) + b2_ref[...]))
    f = hdot(a, w3_ref[...])
    dec = jnp.exp(-t_ref[...] * dl_ref[...])
    tl = z_ref.shape[0]
    row = pl.program_id(0) * tl + lax.broadcasted_iota(jnp.int32, (tl, ch), 0)
    for o in range(HY_ORDER):
        hf = f[:, (2 * o) * ch:(2 * o + 1) * ch] * dec
        hb = jnp.where(row > 0, f[:, (2 * o + 1) * ch:(2 * o + 2) * ch] * dec, 0.0)
        o_ref[:, (2 * o) * ch:(2 * o + 1) * ch] = (hf + hb).astype(BF16)
        o_ref[:, (2 * o + 1) * ch:(2 * o + 2) * ch] = (hf - hb).astype(BF16)


def _hy_filter_inputs(length, w1, b1, w2, b2, w3, ch):
    t = jnp.linspace(0.0, 1.0, length, dtype=F32)[:, None]
    wv = 2.0 * math.pi * jnp.arange(length, dtype=F32)[:, None] / length
    f = jnp.linspace(1e-4, HY_BANDS - 1, HY_BANDS, dtype=F32)
    z = jnp.concatenate([t, jnp.cos(f * wv), -jnp.sin(f * wv)], axis=-1)
    emb = z.shape[1]
    z = jnp.pad(z, ((0, 0), (0, HEAD_LANES - emb)))
    w1p = jnp.pad(w1, ((0, HEAD_LANES - emb), (0, 0)))
    deltas = jnp.abs(jnp.linspace(HY_MIN_DECAY, HY_MAX_DECAY, ch, dtype=F32))[None, :]
    fw = w1.shape[1]
    tl = min(512, length)
    return pl.pallas_call(
        functools.partial(_hy_filter_kernel, ch=ch),
        grid=(length // tl,),
        in_specs=[pl.BlockSpec((tl, HEAD_LANES), lambda i: (i, 0)), pl.BlockSpec((tl, 1), lambda i: (i, 0)),
                  _resident((HEAD_LANES, fw)), _resident((1, fw)), _resident((fw, fw)), _resident((1, fw)),
                  _resident(w3.shape), _resident((1, ch))],
        out_specs=pl.BlockSpec((tl, 2 * HY_ORDER * ch), lambda i: (i, 0)),
        out_shape=jax.ShapeDtypeStruct((length, 2 * HY_ORDER * ch), BF16),
        compiler_params=_cparams("parallel"),
        name="hy_filter",
    )(z, t, w1p, b1.reshape(1, fw), w2, b2.reshape(1, fw), w3, deltas)


def _dft_tables(length):
    n2 = 2 * length
    theta = 2.0 * math.pi / n2
    n = jnp.arange(length, dtype=jnp.int32)[None, :]
    a = 8 * jnp.arange(length // 8, dtype=jnp.int32)[:, None]
    b = jnp.arange(8, dtype=jnp.int32)[:, None]
    ang_a = ((a * n) % n2).astype(F32) * theta
    ang_b = ((b * n) % n2).astype(F32) * theta
    ca, sa = jnp.cos(ang_a)[:, None, :], jnp.sin(ang_a)[:, None, :]
    cb, sb = jnp.cos(ang_b)[None, :, :], jnp.sin(ang_b)[None, :, :]
    fc = (ca * cb - sa * sb).reshape(length, length)
    fs = -(sa * cb + ca * sb).reshape(length, length)
    return fc.astype(BF16), fs.astype(BF16)


def _alternating_sum(x):
    col = lax.broadcasted_iota(jnp.int32, (8, x.shape[0]), 1)
    alt = (1 - 2 * (col & 1)).astype(F32).astype(BF16)
    return _dot(alt, x)[0:1]


def _filter_dft_kernel(hs_ref, hd_ref, fc_ref, fs_ref, o_ref):
    hs = hs_ref[...]
    o_ref[0] = _dot(fc_ref[...], hs)
    ki = _dot(fs_ref[...], hd_ref[...])
    nyq = _alternating_sum(hs)
    tf = ki.shape[0]
    row = pl.program_id(1) * tf + lax.broadcasted_iota(jnp.int32, (tf, 1), 0)
    o_ref[1] = jnp.where(row == 0, nyq, ki)


def _dft_spec_kernel(x_ref, fc_ref, fs_ref, k_ref, o_ref, *, inv_n):
    x = x_ref[...]
    ur = _dot(fc_ref[...], x)
    kr = k_ref[0]
    ki = k_ref[1]
    tf = ur.shape[0]
    row = pl.program_id(1) * tf + lax.broadcasted_iota(jnp.int32, (tf, 1), 0)
    first = row == 0
    ui = jnp.where(first, _alternating_sum(x), _dot(fs_ref[...], x))
    uiki = ui * ki
    yr = ur * kr - jnp.where(first, 0.0, uiki)
    yi = jnp.where(first, uiki, ur * ki + ui * kr)
    scale = jnp.where(first, inv_n, 2.0 * inv_n)
    o_ref[0] = (yr * scale).astype(BF16)
    o_ref[1] = (yi * scale).astype(BF16)


def _filter_spectrum(p, fc, fs, ch):
    length, cols = p.shape
    tf = min(512, length)
    return pl.pallas_call(
        _filter_dft_kernel,
        grid=(HY_ORDER, length // tf),
        in_specs=[pl.BlockSpec((length, ch), lambda o, i: (0, 2 * o)),
                  pl.BlockSpec((length, ch), lambda o, i: (0, 2 * o + 1)),
                  pl.BlockSpec((tf, length), lambda o, i: (i, 0)),
                  pl.BlockSpec((tf, length), lambda o, i: (i, 0))],
        out_specs=pl.BlockSpec((None, 2, tf, ch), lambda o, i: (o, 0, i, 0)),
        out_shape=jax.ShapeDtypeStruct((HY_ORDER, 2, length, ch), F32),
        compiler_params=_cparams("parallel", "arbitrary"),
        name="hy_filter_dft",
    )(p, p, fc, fs)


def _hy_spectral(xb, col, ch, fc, fs, kspec):
    b, length, _ = xb.shape
    tf = min(512, length)
    return pl.pallas_call(
        functools.partial(_dft_spec_kernel, inv_n=1.0 / (2 * length)),
        grid=(b, length // tf),
        in_specs=[pl.BlockSpec((None, length, ch), lambda bi, i: (bi, 0, col)),
                  pl.BlockSpec((tf, length), lambda bi, i: (i, 0)),
                  pl.BlockSpec((tf, length), lambda bi, i: (i, 0)),
                  pl.BlockSpec((2, tf, ch), lambda bi, i: (0, i, 0))],
        out_specs=pl.BlockSpec((None, 2, tf, ch), lambda bi, i: (bi, 0, i, 0)),
        out_shape=jax.ShapeDtypeStruct((b, 2, length, ch), BF16),
        compiler_params=_cparams("parallel", "arbitrary"),
        name="hy_dft_spectral",
    )(xb, fc, fs, kspec)


def _idft_gate_kernel(y_ref, fc_ref, fs_ref, u_ref, xg_ref, d_ref, o_ref, ob_ref):
    tt = fc_ref.shape[0]
    t = pl.program_id(1) * tt + lax.broadcasted_iota(jnp.int32, (tt, 1), 0)
    sign = (1 - 2 * (t & 1)).astype(F32)
    conv = (_dot(fc_ref[...], y_ref[0]) + _dot(fs_ref[...], y_ref[1])
            + sign * y_ref[1, 0:1, :].astype(F32))
    out = xg_ref[...] * (conv + u_ref[...] * d_ref[...])
    o_ref[...] = out
    ob_ref[...] = out.astype(BF16)


def _hy_inverse_gate(y, fc, fs, u, u_col, xg, xg_col, dvec, ch):
    b, _, length, _ = y.shape
    tt = min(512, length)
    out = pl.BlockSpec((None, tt, ch), lambda bi, i: (bi, i, 0))
    return pl.pallas_call(
        _idft_gate_kernel,
        grid=(b, length // tt),
        in_specs=[pl.BlockSpec((None, 2, length, ch), lambda bi, i: (bi, 0, 0, 0)),
                  pl.BlockSpec((tt, length), lambda bi, i: (i, 0)),
                  pl.BlockSpec((tt, length), lambda bi, i: (i, 0)),
                  pl.BlockSpec((None, tt, ch), lambda bi, i: (bi, i, u_col)),
                  pl.BlockSpec((None, tt, ch), lambda bi, i: (bi, i, xg_col)),
                  pl.BlockSpec((1, ch), lambda bi, i: (0, 0))],
        out_specs=[out, out],
        out_shape=[jax.ShapeDtypeStruct((b, length, ch), F32), jax.ShapeDtypeStruct((b, length, ch), BF16)],
        compiler_params=_cparams("parallel", "arbitrary"),
        name="hy_idft_gate",
    )(y, fc, fs, u, xg, dvec.reshape(1, ch))


def _hyena(hyc, hyb, w1, b1, w2, b2, w3, hy_bias):
    ch = hy_bias.shape[-1]
    length = hyc.shape[1]
    fc, fs = _dft_tables(length)
    kspec = _filter_spectrum(_hy_filter_inputs(length, w1, b1, w2, b2, w3, ch), fc, fs, ch)
    y = _hy_spectral(hyb, 0, ch, fc, fs, kspec[0])
    y1, y1b = _hy_inverse_gate(y, fc, fs, hyc, 0, hyc, 1, hy_bias[0], ch)
    y = _hy_spectral(y1b, 0, ch, fc, fs, kspec[1])
    y2, _ = _hy_inverse_gate(y, fc, fs, y1, 0, hyc, 2, hy_bias[1], ch)
    return y2


def _ab_out_kernel(o_ref, y_ref, x_ref, g1_ref, woa_ref, woh_ref, out_ref):
    y = _dot(o_ref[...], woa_ref[...]) + _dot(y_ref[...].astype(BF16), woh_ref[...])
    out_ref[...] = x_ref[...] + g1_ref[...] * y


def _ab_out(o, y2, x, mods, row_fn, w):
    b, n, d = x.shape
    tm = min(ROW_TILE, n)
    tok = lambda width: pl.BlockSpec((None, tm, width), lambda bi, i: (bi, i, 0))
    return pl.pallas_call(
        _ab_out_kernel,
        grid=(b, n // tm),
        in_specs=[tok(o.shape[-1]), tok(y2.shape[-1]), tok(d), _mod_spec(row_fn, 2, d),
                  _resident(w["wo_a"].shape), _resident(w["wo_h"].shape)],
        out_specs=tok(d),
        out_shape=jax.ShapeDtypeStruct((b, n, d), F32),
        compiler_params=_cparams("parallel", "parallel"),
        name="ab_out",
    )(o, y2, x, mods, w["wo_a"], w["wo_h"])


FFN_ROWS = 512
FFN_CHUNK_MAX = 2816


def _ffn_kernel(*refs, hidden, hc, final):
    if final:
        (x_ref, xp_ref, xn_ref, sh_ref, sc_ref, g2_ref, ng_ref, wup_ref, cw_ref, cb_ref, wdn_ref,
         fg_ref, o_ref) = refs
    else:
        (x_ref, xp_ref, xn_ref, sh_ref, sc_ref, g2_ref, ng_ref, wup_ref, cw_ref, cb_ref, wdn_ref,
         o_ref) = refs
    x = x_ref[...]
    tm, d = x.shape
    he = _halo_rows(x_ref, xp_ref, xn_ref, lambda a: _modulate(a, ng_ref[...], sh_ref[...], sc_ref[...]))

    def conv(a, col):
        return _conv3_rows(a, cw_ref[:, col:col + hc], cb_ref[:, col:col + hc], tm)

    acc = jnp.zeros((tm, d), F32)
    for j in range(hidden // hc):
        gate = conv(_dot(he, wup_ref[:, j * hc:(j + 1) * hc]), j * hc)
        val = conv(_dot(he, wup_ref[:, hidden + j * hc:hidden + (j + 1) * hc]), hidden + j * hc)
        mid = (_silu(gate) * val).astype(BF16)
        acc = acc + _dot(mid, wdn_ref[j * hc:(j + 1) * hc, :])
    out = x + g2_ref[...] * acc
    if final:
        out = _rms(out, fg_ref[...])
    o_ref[...] = out


def _ffn_chunk(hidden):
    for hc in range(min(FFN_CHUNK_MAX, hidden) // HEAD_LANES * HEAD_LANES, 0, -HEAD_LANES):
        if hidden % hc == 0:
            return hc
    raise ValueError(f"FFN hidden width {hidden} is not a multiple of {HEAD_LANES}")


def _conv_ffn(x, mods, row_fn, norm_g, w_up, conv_w, conv_b, w_down, final_g=None):
    b, n, d = x.shape
    hidden = w_down.shape[0]
    tm = min(FFN_ROWS, n)
    final = final_g is not None
    tok = pl.BlockSpec((None, tm, d), lambda bi, i: (bi, i, 0))
    in_specs = _halo_specs(tm, n, d) + [
        _mod_spec(row_fn, 3, d), _mod_spec(row_fn, 4, d), _mod_spec(row_fn, 5, d),
        _resident((1, d)), _resident(w_up.shape), _resident(conv_w.shape),
        _resident((1, 2 * hidden)), _resident(w_down.shape)]
    args = [x, x, x, mods, mods, mods, norm_g.reshape(1, d), w_up, conv_w, conv_b.reshape(1, 2 * hidden), w_down]
    if final:
        in_specs.append(_resident((1, d)))
        args.append(final_g.reshape(1, d))
    return pl.pallas_call(
        functools.partial(_ffn_kernel, hidden=hidden, hc=_ffn_chunk(hidden), final=final),
        grid=(b, n // tm),
        in_specs=in_specs,
        out_specs=tok,
        out_shape=jax.ShapeDtypeStruct((b, n, d), F32),
        compiler_params=_cparams("parallel", "parallel"),
        name="conv_ffn",
    )(*args)


def _hg_proj_kernel(x_ref, sh_ref, sc_ref, g_ref, w_ref, o_ref, *, width):
    h = _modulate(x_ref[...], g_ref[...], sh_ref[...], sc_ref[...]).astype(BF16)
    z = _dot(h, w_ref[...])
    o_ref[:, :width] = _silu(z[:, :width])
    o_ref[:, width:] = z[:, width:]


def _hg_project(x, mods, row_fn, norm_g, w_in):
    b, n, d = x.shape
    cols = w_in.shape[1]
    tm = min(ROW_TILE, n)
    tok = lambda width: pl.BlockSpec((None, tm, width), lambda bi, i: (bi, i, 0))
    return pl.pallas_call(
        functools.partial(_hg_proj_kernel, width=cols // 5),
        grid=(b, n // tm),
        in_specs=[tok(d), _mod_spec(row_fn, 0, d), _mod_spec(row_fn, 1, d), _resident((1, d)),
                  _resident(w_in.shape)],
        out_specs=tok(cols),
        out_shape=jax.ShapeDtypeStruct((b, n, cols), F32),
        compiler_params=_cparams("parallel", "parallel"),
        name="hg_project",
    )(x, mods, mods, norm_g.reshape(1, d), w_in)


def _scan_tables(direction):
    c = HG_CHUNK
    idx = np.arange(c)
    level = np.full((c, c), -1, np.int32)
    for li, m in enumerate(HG_LEVELS):
        same = (idx[:, None] // (2 * m)) == (idx[None, :] // (2 * m))
        up = (idx % (2 * m)) >= m
        pair = same & (up[:, None] & ~up[None, :] if direction == 0 else ~up[:, None] & up[None, :])
        level[pair] = li
    level[idx, idx] = len(HG_LEVELS)
    if direction == 0:
        to_q = (idx[None, :] <= idx[:, None])
    else:
        to_q = (idx[None, :] >= idx[:, None])
    em = np.concatenate([to_q.astype(np.float32), np.ones((8, c), np.float32)], axis=0)
    return jnp.asarray(np.concatenate([em, em], axis=1), BF16), jnp.asarray(level)


def _level_exponent(run, m, rev):
    c, w = run.shape
    off = m if rev else m - 1
    if 2 * m >= 8:
        parts = [jnp.broadcast_to(run[a + off:a + off + 1, :], (2 * m, w)) for a in range(0, c, 2 * m)]
    elif m == 2:
        low = lax.broadcasted_iota(jnp.int32, (8, w), 0) < 4
        parts = [jnp.where(low, jnp.broadcast_to(run[a + off:a + off + 1, :], (8, w)),
                           jnp.broadcast_to(run[a + 4 + off:a + 5 + off, :], (8, w))) for a in range(0, c, 8)]
    else:
        odd = (lax.broadcasted_iota(jnp.int32, (c, w), 0) & 1) == 1
        if rev:
            parts = [jnp.where(odd, run, pltpu.roll(run, c - 1, axis=0))]
        else:
            parts = [jnp.where(odd, pltpu.roll(run, 1, axis=0), run)]
    ref = parts[0] if len(parts) == 1 else jnp.concatenate(parts, axis=0)
    bits = lax.bitcast_convert_type(run - ref, jnp.uint32) | jnp.uint32(0x80000000)
    return lax.bitcast_convert_type(bits, F32)


def _gla_group(chunks, lb, em2, lv, st, rev):
    c = HG_CHUNK
    e_ = HG_EXPAND
    nl = len(HG_LEVELS)
    log_lb = jnp.log(lb)
    log_1m = jnp.log1p(-lb)
    log_f = []
    keys = []
    for _, fr, _ in chunks:
        t = jnp.exp(-jnp.abs(fr))
        d = 1.0 + t
        keys.append((1.0 - lb) * (jnp.where(fr > 0, t, 1.0) / d))
        b = log_1m + (jnp.minimum(fr, 0.0) - jnp.log(d))
        log_f.append(jnp.maximum(log_lb, b) + jnp.log(1.0 + jnp.exp(-jnp.abs(log_lb - b))))
    g = jnp.concatenate(log_f, axis=1) * math.log2(math.e)
    hi = g.astype(BF16)
    lo = (g - hi.astype(F32)).astype(BF16)
    e_all = _dot(em2, jnp.concatenate([hi, lo], axis=0))
    outs = []
    for gi, (q, _, val) in enumerate(chunks):
        to_q = e_all[:c, gi * e_:(gi + 1) * e_]
        total = e_all[c:c + 1, gi * e_:(gi + 1) * e_]
        key = keys[gi]
        att = jnp.zeros((c, c), F32)
        for li, m in enumerate(HG_LEVELS):
            dec = jnp.exp2(_level_exponent(to_q, m, rev))
            att = jnp.where(lv == li, _dot((q * dec).astype(BF16), (key * dec).astype(BF16), _NT), att)
        att = jnp.where(lv == nl, jnp.sum(q * key, axis=1, keepdims=True), att)
        vb = val.astype(BF16)
        q_in = (q * jnp.exp2(to_q)).astype(BF16)
        outs.append(_dot(att.astype(BF16), vb) + _dot(q_in, st.astype(BF16), _NT))
        k_out = (key * jnp.exp2(total - to_q)).astype(BF16)
        st = st * jnp.exp2(total) + _dot(vb, k_out, _TN)
    return outs, st


def _hg_scan_kernel(*refs, layer, need_ctx):
    (lg_ref, emf_ref, emb_ref, lvf_ref, lvb_ref,
     qc_ref, ffc_ref, fbc_ref, ic_ref, q_ref, ff_ref, fb_ref, i_ref) = refs[:13]
    if need_ctx:
        o_ref, oc_ref, st_ref = refs[13:]
    else:
        o_ref, st_ref = refs[13:]
        oc_ref = None
    c = HG_CHUNK
    lg = lg_ref[...]
    ex = jnp.exp(lg - jnp.max(lg, axis=0, keepdims=True))
    probs = ex / jnp.sum(ex, axis=0, keepdims=True)
    acc = probs[0]
    for l in range(1, layer + 1):
        acc = acc + probs[l]
    lb_all = acc - probs[0]
    lb_f = lb_all[0:1]
    lb_b = lb_all[1:2]

    st_ref[...] = jnp.zeros_like(st_ref)
    o_ref[...] = jnp.zeros_like(o_ref)
    if need_ctx:
        oc_ref[...] = jnp.zeros_like(oc_ref)

    def run(qr, ffr, fbr, ir, outr):
        nc = qr.shape[0] // c
        grp = math.gcd(nc, HG_GROUP)
        ng = nc // grp

        def body(gi, carry):
            rows_f = [pl.multiple_of((gi * grp + k) * c, c) for k in range(grp)]
            rows_b = [pl.multiple_of((nc - 1 - gi * grp - k) * c, c) for k in range(grp)]
            load = lambda gate_ref, r: (qr[pl.ds(r, c), :], gate_ref[pl.ds(r, c), :], ir[pl.ds(r, c), :])
            of, stf = _gla_group([load(ffr, r) for r in rows_f], lb_f, emf_ref[...], lvf_ref[...], st_ref[0], False)
            st_ref[0] = stf
            ob, stb = _gla_group([load(fbr, r) for r in rows_b], lb_b, emb_ref[...], lvb_ref[...], st_ref[1], True)
            st_ref[1] = stb
            if outr is not None:
                for r, o in zip(rows_f + rows_b, of + ob):
                    outr[pl.ds(r, c), :] += o
            return carry

        lax.fori_loop(0, ng, body, 0, unroll=min(HG_UNROLL, ng))

    run(qc_ref, ffc_ref, fbc_ref, ic_ref, oc_ref)
    run(q_ref, ff_ref, fb_ref, i_ref, o_ref)


def _hg_scan(z, zc, lb_logits, layer, need_ctx):
    b, n, cols = z.shape
    nc_ = zc.shape[1]
    width = cols // 5
    heads = width // HG_EXPAND
    e = HG_EXPAND
    depth = lb_logits.shape[0]
    emf, lvf = _scan_tables(0)
    emb, lvb = _scan_tables(1)
    col = lambda rows, group: pl.BlockSpec((None, rows, e), lambda bi, h: (bi, 0, group * heads + h))
    in_specs = [pl.BlockSpec((depth, 2, e), lambda bi, h: (0, 0, h)),
                _resident(emf.shape), _resident(emb.shape), _resident(lvf.shape), _resident(lvb.shape),
                col(nc_, 0), col(nc_, 1), col(nc_, 2), col(nc_, 3),
                col(n, 0), col(n, 1), col(n, 2), col(n, 3)]
    out_specs = [pl.BlockSpec((None, n, e), lambda bi, h: (bi, 0, h))]
    out_shape = [jax.ShapeDtypeStruct((b, n, width), F32)]
    if need_ctx:
        out_specs.append(pl.BlockSpec((None, nc_, e), lambda bi, h: (bi, 0, h)))
        out_shape.append(jax.ShapeDtypeStruct((b, nc_, width), F32))
    res = pl.pallas_call(
        functools.partial(_hg_scan_kernel, layer=layer, need_ctx=need_ctx),
        grid=(b, heads),
        in_specs=in_specs,
        out_specs=out_specs,
        out_shape=out_shape,
        scratch_shapes=[pltpu.VMEM((2, e, e), F32)],
        compiler_params=_cparams("parallel", "parallel"),
        name="hg_scan",
    )(lb_logits, emf, emb, lvf, lvb, zc, zc, zc, zc, z, z, z, z)
    return (res[0], res[1]) if need_ctx else (res[0], None)


def _hg_out_kernel(o_ref, g_ref, x_ref, g1_ref, ng_ref, w_ref, out_ref, *, heads):
    ng = ng_ref[...]
    parts = [_rms(o_ref[:, h * HG_EXPAND:(h + 1) * HG_EXPAND], ng) for h in range(heads)]
    o = jnp.concatenate(parts, axis=1) * _silu(g_ref[...])
    out_ref[...] = x_ref[...] + g1_ref[...] * _dot(o.astype(BF16), w_ref[...])


def _hg_out(o, z, x, mods, row_fn, norm_g, w_out):
    b, n, d = x.shape
    width = o.shape[-1]
    tm = min(ROW_TILE, n)
    tok = lambda wd: pl.BlockSpec((None, tm, wd), lambda bi, i: (bi, i, 0))
    return pl.pallas_call(
        functools.partial(_hg_out_kernel, heads=width // HG_EXPAND),
        grid=(b, n // tm),
        in_specs=[tok(width), pl.BlockSpec((None, tm, width), lambda bi, i: (bi, i, 4)), tok(d),
                  _mod_spec(row_fn, 2, d), _resident((1, HG_EXPAND)), _resident(w_out.shape)],
        out_specs=tok(d),
        out_shape=jax.ShapeDtypeStruct((b, n, d), F32),
        compiler_params=_cparams("parallel", "parallel"),
        name="hg_out",
    )(o, z, x, mods, norm_g.reshape(1, HG_EXPAND), w_out)


def kernel(x, c, ctx, c_ctx, mod_w, mod_b, norm1_g, norm2_g, ffn_w_up, ffn_conv_w, ffn_conv_b, ffn_w_down, ab_w_in, mla_q_norm_g, mla_w_q_b, mla_kv_norm_g, mla_w_kv_b, hy_conv_w, hy_conv_b, hy_w1, hy_b1, hy_w2, hy_b2, hy_w3, hy_bias, ab_w_out, hg_w_in, hg_lb_logits, hg_norm_g, hg_w_out, final_norm_g):
    batch, n, d = x.shape
    depth = mod_w.shape[0]
    assert batch < MOD_ROWS
    ctx_row = batch
    cc = jnp.zeros((MOD_ROWS, d), F32).at[:batch].set(c).at[ctx_row].set(c_ctx)
    mods_all = _mod_vectors(cc, mod_w, mod_b)
    lat_row = lambda bi, *_: bi
    ctx_row_fn = lambda bi, *_: ctx_row
    rope_tabs = _rope_tables(n)

    xc = ctx
    for layer in range(depth):
        need_ctx = layer < depth - 1
        j = layer // 2
        mods = mods_all[layer]
        if layer % 2 == 0:
            q_lora = mla_q_norm_g.shape[-1]
            kv_lora = mla_kv_norm_g.shape[-1]
            hy_ch = hy_bias.shape[-1]
            w = _ab_weights(ab_w_in[j], mla_w_q_b[j], mla_w_kv_b[j], ab_w_out[j], q_lora, kv_lora, hy_ch)
            hy_args = (hy_w1[j], hy_b1[j], hy_w2[j], hy_b2[j], hy_w3[j], hy_bias[j])
            n_ctx = xc.shape[1]
            q, k, v, hy, hyb = _ab_project(x, mods, lat_row, norm1_g[layer], w, mla_q_norm_g[j],
                                           mla_kv_norm_g[j], rope_tabs, hy_conv_w[j], hy_conv_b[j], hy_ch,
                                           kv_rows=n + n_ctx)
            q_c, k, v, hy_c, hyb_c = _ab_project(xc, mods, ctx_row_fn, norm1_g[layer], w, mla_q_norm_g[j],
                                                 mla_kv_norm_g[j], None, hy_conv_w[j], hy_conv_b[j], hy_ch,
                                                 kv_rows=n + n_ctx, kv_row0=n, kv_into=(k, v))
            o = _attention(q, [k], [v], w["heads"])
            x_new = _ab_out(o, _hyena(hy, hyb, *hy_args), x, mods, lat_row, w)
            if need_ctx:
                o_c = _attention(q_c, [k], [v], w["heads"], kv_rows=n_ctx, kv_row0=n)
                xc = _ab_out(o_c, _hyena(hy_c, hyb_c, *hy_args), xc, mods, ctx_row_fn, w)
            x = x_new
        else:
            w_in = hg_w_in[j].astype(BF16)
            z_c = _hg_project(xc, mods, ctx_row_fn, norm1_g[layer], w_in)
            z = _hg_project(x, mods, lat_row, norm1_g[layer], w_in)
            o, o_c = _hg_scan(z, z_c, hg_lb_logits, layer, need_ctx)
            w_out = hg_w_out[j].astype(BF16)
            x = _hg_out(o, z, x, mods, lat_row, hg_norm_g[j], w_out)
            if need_ctx:
                xc = _hg_out(o_c, z_c, xc, mods, ctx_row_fn, hg_norm_g[j], w_out)
        w_up = ffn_w_up[layer].astype(BF16)
        w_dn = ffn_w_down[layer].astype(BF16)
        last = layer == depth - 1
        x = _conv_ffn(x, mods, lat_row, norm2_g[layer], w_up, ffn_conv_w[layer], ffn_conv_b[layer], w_dn,
                      final_norm_g if last else None)
        if need_ctx:
            xc = _conv_ffn(xc, mods, ctx_row_fn, norm2_g[layer], w_up, ffn_conv_w[layer], ffn_conv_b[layer], w_dn)
    return x
```

```python
import functools
import math

import numpy as np
import jax
import jax.numpy as jnp
from jax import lax
from jax.experimental import pallas as pl
from jax.experimental.pallas import tpu as pltpu

F32 = jnp.float32
BF16 = jnp.bfloat16
HIGHEST = lax.Precision.HIGHEST

EPS = 1e-6
GRID_W = 64

MLA_NOPE = 64
MLA_ROPE = 32
MLA_V = 64
ROPE_NF = MLA_ROPE // 4
ROPE_BASE = 10000.0
HEAD_LANES = 128
ATTN_TQ = 4096
ATTN_SUB = 256

HY_ORDER = 2
HY_BANDS = 16
HY_SIN_FREQ = 1.0
HY_DECAY_TARGET = 1e-2
HY_FAST_DECAY = 0.3
HY_SLOW_DECAY = 1.5
HY_MIN_DECAY = math.log(HY_DECAY_TARGET) / HY_SLOW_DECAY
HY_MAX_DECAY = math.log(HY_DECAY_TARGET) / HY_FAST_DECAY

HG_EXPAND = 128
HG_CHUNK = 64
HG_LEVELS = (32, 16, 8, 4, 2, 1)
HG_GROUP = 4
HG_UNROLL = 2

ROW_TILE = 512
MOD_ROWS = 16
HALO = 8
VMEM_LIMIT = 56 * 1024 * 1024

_NT = (((1,), (1,)), ((), ()))
_TN = (((0,), (0,)), ((), ()))


def _cparams(*sem):
    return pltpu.CompilerParams(dimension_semantics=sem, vmem_limit_bytes=VMEM_LIMIT)


def _dot(a, b, dims=(((1,), (0,)), ((), ()))):
    return lax.dot_general(a, b, dims, precision=lax.Precision.DEFAULT, preferred_element_type=F32)


def _rms(x, g):
    return x * lax.rsqrt(jnp.mean(x * x, axis=-1, keepdims=True) + EPS) * g


def _modulate(x, g, shift, scale):
    return _rms(x, g) * (1.0 + scale) + shift


def _silu(x):
    return x * jax.nn.sigmoid(x)


def _mod_spec(row_fn, k, d):
    return pl.BlockSpec((None, None, 1, d), lambda *idx: (row_fn(*idx), k, 0, 0))


def _resident(shape):
    nd = len(shape)
    return pl.BlockSpec(shape, lambda *idx: (0,) * nd, pipeline_mode=pl.Buffered(1))


def _mod_kernel(c_ref, w_ref, b_ref, o_ref):
    a = _silu(c_ref[...])
    o_ref[...] = jnp.dot(a, w_ref[...], precision=HIGHEST, preferred_element_type=F32) + b_ref[...]


def _mod_vectors(cc, mod_w, mod_b):
    depth, d, n6 = mod_w.shape
    tn = 512
    out = pl.pallas_call(
        _mod_kernel,
        grid=(depth, n6 // tn),
        in_specs=[pl.BlockSpec((MOD_ROWS, d), lambda l, j: (0, 0)),
                  pl.BlockSpec((None, d, tn), lambda l, j: (l, 0, j)),
                  pl.BlockSpec((None, 1, tn), lambda l, j: (l, 0, j))],
        out_specs=pl.BlockSpec((None, MOD_ROWS, tn), lambda l, j: (l, 0, j)),
        out_shape=jax.ShapeDtypeStruct((depth, MOD_ROWS, n6), F32),
        compiler_params=_cparams("parallel", "parallel"),
        name="mod_vectors",
    )(cc, mod_w, mod_b.reshape(depth, 1, n6))
    return out.reshape(depth, MOD_ROWS, 6, 1, d)


def _halo_rows(x_ref, xp_ref, xn_ref, mod):
    i = pl.program_id(1)
    last = pl.num_programs(1) - 1
    hp = mod(xp_ref[...]) * (i > 0).astype(F32)
    hn = mod(xn_ref[...]) * (i < last).astype(F32)
    return jnp.concatenate([hp, mod(x_ref[...]), hn], axis=0).astype(BF16)


def _conv3_rows(a, w, bias, tm):
    ne = tm + 2 * HALO
    prev = pltpu.roll(a, 1, axis=0)[HALO:HALO + tm]
    nxt = pltpu.roll(a, ne - 1, axis=0)[HALO:HALO + tm]
    return prev * w[0:1] + a[HALO:HALO + tm] * w[1:2] + nxt * w[2:3] + bias


def _ab_proj_kernel(*refs, rope, q_lora, kv_lora, heads, scale, ch, kv_aliased):
    if kv_aliased:
        refs = refs[:-7] + refs[-5:]
    if rope:
        (x_ref, xp_ref, xn_ref, sh_ref, sc_ref, g_ref, win_ref, qg_ref, wqa_ref, wqb_ref, kvg_ref, wk_ref,
         wv_ref, cw_ref, cb_ref, cos_ref, sin_ref, q_ref, k_ref, v_ref, hy_ref, hyb_ref) = refs
    else:
        (x_ref, xp_ref, xn_ref, sh_ref, sc_ref, g_ref, win_ref, qg_ref, wqa_ref, kvg_ref, wk_ref,
         wv_ref, cw_ref, cb_ref, q_ref, k_ref, v_ref, hy_ref, hyb_ref) = refs
    tm = x_ref.shape[0]
    he = _halo_rows(x_ref, xp_ref, xn_ref, lambda a: _modulate(a, g_ref[...], sh_ref[...], sc_ref[...]))
    ze = _dot(he, win_ref[...])
    o = q_lora + kv_lora
    hy = _conv3_rows(ze[:, o + 2 * HEAD_LANES:], cw_ref[...], cb_ref[...], tm)
    hy_ref[...] = hy
    hyb_ref[...] = hy[:, :ch].astype(BF16)
    z = ze[HALO:HALO + tm]
    qn = _rms(z[:, :q_lora], qg_ref[...]).astype(BF16)
    kvn = _rms(z[:, q_lora:o], kvg_ref[...]).astype(BF16)
    v_ref[...] = _dot(kvn, wv_ref[...]).astype(BF16)
    qa = _dot(qn, wqa_ref[...])
    kk = _dot(kvn, wk_ref[...])
    kr = z[:, o:o + HEAD_LANES]
    if rope:
        cos = cos_ref[...]
        sin = sin_ref[...]
        qb = _dot(qn, wqb_ref[...])
        kr = kr * cos + z[:, o + HEAD_LANES:o + 2 * HEAD_LANES] * sin
    for hh in range(heads):
        sl = slice(hh * HEAD_LANES, (hh + 1) * HEAD_LANES)
        qh = qa[:, sl]
        if rope:
            qh = qh * cos + qb[:, sl] * sin
        q_ref[:, sl] = (qh * scale).astype(BF16)
        k_ref[:, sl] = (kk[:, sl] + kr).astype(BF16)


def _rope_swap(w):
    nf = ROPE_NF
    return jnp.concatenate([-w[:, nf:2 * nf], w[:, :nf], -w[:, 3 * nf:4 * nf], w[:, 2 * nf:3 * nf]], axis=1)


def _ab_weights(w_in, w_q_b, w_kv_b, w_out, q_lora, kv_lora, hy_ch):
    d = w_in.shape[0]
    heads = w_q_b.shape[1] // (MLA_NOPE + MLA_ROPE)
    o = q_lora + kv_lora
    w_kr = w_in[:, o:o + MLA_ROPE]
    zpad = lambda n: jnp.zeros((d, n), F32)
    tail = HEAD_LANES - MLA_NOPE - MLA_ROPE
    win = jnp.concatenate(
        [w_in[:, :o],
         zpad(MLA_NOPE), w_kr, zpad(tail),
         zpad(MLA_NOPE), _rope_swap(w_kr), zpad(tail),
         w_in[:, o + MLA_ROPE:]], axis=1).astype(BF16)
    wq = w_q_b.reshape(q_lora, heads, MLA_NOPE + MLA_ROPE)
    zq = jnp.zeros((q_lora, heads, tail), F32)
    wqa = jnp.concatenate([wq, zq], axis=2).reshape(q_lora, heads * HEAD_LANES).astype(BF16)
    wq_rope = wq[:, :, MLA_NOPE:].reshape(q_lora * heads, MLA_ROPE)
    wqb = jnp.concatenate([jnp.zeros((q_lora, heads, MLA_NOPE), F32),
                           _rope_swap(wq_rope).reshape(q_lora, heads, MLA_ROPE), zq],
                          axis=2).reshape(q_lora, heads * HEAD_LANES).astype(BF16)
    wkv = w_kv_b.reshape(kv_lora, heads, MLA_NOPE + MLA_V)
    zk = jnp.zeros((kv_lora, heads, HEAD_LANES - MLA_NOPE), F32)
    wk = jnp.concatenate([wkv[:, :, :MLA_NOPE], zk], axis=2).reshape(kv_lora, heads * HEAD_LANES).astype(BF16)
    zv = jnp.zeros((kv_lora, heads, HEAD_LANES - MLA_V), F32)
    wv = jnp.concatenate([wkv[:, :, MLA_NOPE:], zv], axis=2).reshape(kv_lora, heads * HEAD_LANES).astype(BF16)
    wo = w_out[:heads * MLA_V].reshape(heads, MLA_V, d)
    wo_a = jnp.concatenate([wo, jnp.zeros((heads, HEAD_LANES - MLA_V, d), F32)],
                           axis=1).reshape(heads * HEAD_LANES, d).astype(BF16)
    wo_h = w_out[heads * MLA_V:].astype(BF16)
    return dict(win=win, wqa=wqa, wqb=wqb, wk=wk, wv=wv, wo_a=wo_a, wo_h=wo_h, heads=heads)


def _rope_tables(n):
    rows = n // GRID_W
    row = jnp.repeat(jnp.arange(rows), GRID_W).astype(F32)
    col = jnp.tile(jnp.arange(GRID_W), rows).astype(F32)
    inv = ROPE_BASE ** (-jnp.arange(ROPE_NF, dtype=F32) / ROPE_NF)
    ar = row[:, None] * inv
    ac = col[:, None] * inv
    ones = jnp.ones((n, MLA_NOPE), F32)
    tail = HEAD_LANES - MLA_NOPE - MLA_ROPE
    cos = jnp.concatenate([ones, jnp.cos(ar), jnp.cos(ar), jnp.cos(ac), jnp.cos(ac),
                           jnp.ones((n, tail), F32)], axis=1)
    sin = jnp.concatenate([0 * ones, jnp.sin(ar), jnp.sin(ar), jnp.sin(ac), jnp.sin(ac),
                           jnp.zeros((n, tail), F32)], axis=1)
    return cos, sin


def _halo_specs(tm, n, d):
    nh = n // HALO
    tpb = tm // HALO
    return [pl.BlockSpec((None, tm, d), lambda bi, i: (bi, i, 0)),
            pl.BlockSpec((None, HALO, d), lambda bi, i: (bi, jnp.maximum(i * tpb - 1, 0), 0)),
            pl.BlockSpec((None, HALO, d), lambda bi, i: (bi, jnp.minimum((i + 1) * tpb, nh - 1), 0))]


def _ab_project(x, mods, row_fn, norm_g, w, q_norm_g, kv_norm_g, rope_tabs, conv_w, conv_b, ch,
                kv_rows=None, kv_row0=0, kv_into=None):
    b, n, d = x.shape
    tm = min(ROW_TILE, n)
    kv_rows = kv_rows or n
    assert kv_row0 % tm == 0
    heads = w["heads"]
    q_lora = q_norm_g.shape[-1]
    kv_lora = kv_norm_g.shape[-1]
    hw = heads * HEAD_LANES
    hy_cols = conv_w.shape[-1]
    rope = rope_tabs is not None
    tok = lambda width: pl.BlockSpec((None, tm, width), lambda bi, i: (bi, i, 0))
    in_specs = _halo_specs(tm, n, d) + [
        _mod_spec(row_fn, 0, d), _mod_spec(row_fn, 1, d), _resident((1, d)),
        _resident(w["win"].shape), _resident((1, q_lora)), _resident(w["wqa"].shape)]
    args = [x, x, x, mods, mods, norm_g.reshape(1, d), w["win"], q_norm_g.reshape(1, q_lora), w["wqa"]]
    if rope:
        in_specs.append(_resident(w["wqb"].shape))
        args.append(w["wqb"])
    in_specs += [_resident((1, kv_lora)), _resident(w["wk"].shape), _resident(w["wv"].shape),
                 _resident(conv_w.shape), _resident((1, hy_cols))]
    args += [kv_norm_g.reshape(1, kv_lora), w["wk"], w["wv"], conv_w, conv_b.reshape(1, hy_cols)]
    if rope:
        in_specs += [pl.BlockSpec((tm, HEAD_LANES), lambda bi, i: (i, 0))] * 2
        args += list(rope_tabs)
    aliases = {}
    if kv_into is not None:
        aliases = {len(args): 1, len(args) + 1: 2}
        in_specs += [pl.BlockSpec(memory_space=pl.ANY)] * 2
        args += list(kv_into)
    kern = functools.partial(_ab_proj_kernel, rope=rope, q_lora=q_lora, kv_lora=kv_lora, heads=heads, ch=ch,
                             kv_aliased=kv_into is not None,
                             scale=math.log2(math.e) / math.sqrt(MLA_NOPE + MLA_ROPE))
    kv_spec = pl.BlockSpec((None, tm, hw), lambda bi, i: (bi, i + kv_row0 // tm, 0))
    kv_shape = jax.ShapeDtypeStruct((b, kv_rows, hw), BF16)
    return pl.pallas_call(
        kern,
        grid=(b, n // tm),
        in_specs=in_specs,
        out_specs=[tok(hw), kv_spec, kv_spec, tok(hy_cols), tok(ch)],
        out_shape=[jax.ShapeDtypeStruct((b, n, hw), BF16), kv_shape, kv_shape,
                   jax.ShapeDtypeStruct((b, n, hy_cols), F32), jax.ShapeDtypeStruct((b, n, ch), BF16)],
        input_output_aliases=aliases,
        compiler_params=_cparams("parallel", "parallel"),
        name="ab_project",
    )(*args)


def _attn_kernel(*refs, sub, nseg):
    q_ref, o_ref = refs[0], refs[-1]
    ks = [r[...] for r in refs[1:1 + nseg]]
    vs = [r[...] for r in refs[1 + nseg:1 + 2 * nseg]]
    for r in range(0, q_ref.shape[0], sub):
        q = q_ref[r:r + sub, :]
        ss = [_dot(q, k, _NT) for k in ks]
        m = functools.reduce(jnp.maximum, [jnp.max(s, axis=-1, keepdims=True) for s in ss])
        ps = [jnp.exp2(s - m) for s in ss]
        l = sum(jnp.sum(p, axis=-1, keepdims=True) for p in ps)
        o = sum(_dot(p.astype(BF16), v) for p, v in zip(ps, vs))
        o_ref[r:r + sub, :] = (o / l).astype(o_ref.dtype)


def _attention(q, ks, vs, heads, kv_rows=None, kv_row0=0):
    b, n, hw = q.shape
    tq = min(ATTN_TQ, n)
    tile = pl.BlockSpec((None, tq, HEAD_LANES), lambda bi, h, i: (bi, i, h))
    rows = lambda a: kv_rows or a.shape[1]
    assert all(kv_row0 % rows(a) == 0 for a in ks)
    whole = lambda a: pl.BlockSpec((None, rows(a), HEAD_LANES), lambda bi, h, i: (bi, kv_row0 // rows(a), h))
    return pl.pallas_call(
        functools.partial(_attn_kernel, sub=min(ATTN_SUB, tq), nseg=len(ks)),
        grid=(b, heads, n // tq),
        in_specs=[tile] + [whole(a) for a in ks] + [whole(a) for a in vs],
        out_specs=tile,
        out_shape=jax.ShapeDtypeStruct((b, n, hw), BF16),
        compiler_params=_cparams("parallel", "parallel", "parallel"),
        name="attention",
    )(q, *ks, *vs)


def _hy_filter_kernel(z_ref, t_ref, w1_ref, b1_ref, w2_ref, b2_ref, w3_ref, dl_ref, o_ref, *, ch):
    hdot = lambda a, w: jnp.dot(a, w, precision=HIGHEST, preferred_element_type=F32)
    a = jnp.sin(HY_SIN_FREQ * (hdot(z_ref[...], w1_ref[...]) + b1_ref[...]))
    a = jnp.sin(HY_SIN_FREQ * (hdot(a, w2_ref[...]) + b2_ref[...]))
    f = hdot(a, w3_ref[...])
    dec = jnp.exp(-t_ref[...] * dl_ref[...])
    tl = z_ref.shape[0]
    row = pl.program_id(0) * tl + lax.broadcasted_iota(jnp.int32, (tl, ch), 0)
    for o in range(HY_ORDER):
        hf = f[:, (2 * o) * ch:(2 * o + 1) * ch] * dec
        hb = jnp.where(row > 0, f[:, (2 * o + 1) * ch:(2 * o + 2) * ch] * dec, 0.0)
        o_ref[:, (2 * o) * ch:(2 * o + 1) * ch] = (hf + hb).astype(BF16)
        o_ref[:, (2 * o + 1) * ch:(2 * o + 2) * ch] = (hf - hb).astype(BF16)


def _hy_filter_inputs(length, w1, b1, w2, b2, w3, ch):
    t = jnp.linspace(0.0, 1.0, length, dtype=F32)[:, None]
    wv = 2.0 * math.pi * jnp.arange(length, dtype=F32)[:, None] / length
    f = jnp.linspace(1e-4, HY_BANDS - 1, HY_BANDS, dtype=F32)
    z = jnp.concatenate([t, jnp.cos(f * wv), -jnp.sin(f * wv)], axis=-1)
    emb = z.shape[1]
    z = jnp.pad(z, ((0, 0), (0, HEAD_LANES - emb)))
    w1p = jnp.pad(w1, ((0, HEAD_LANES - emb), (0, 0)))
    deltas = jnp.abs(jnp.linspace(HY_MIN_DECAY, HY_MAX_DECAY, ch, dtype=F32))[None, :]
    fw = w1.shape[1]
    tl = min(512, length)
    return pl.pallas_call(
        functools.partial(_hy_filter_kernel, ch=ch),
        grid=(length // tl,),
        in_specs=[pl.BlockSpec((tl, HEAD_LANES), lambda i: (i, 0)), pl.BlockSpec((tl, 1), lambda i: (i, 0)),
                  _resident((HEAD_LANES, fw)), _resident((1, fw)), _resident((fw, fw)), _resident((1, fw)),
                  _resident(w3.shape), _resident((1, ch))],
        out_specs=pl.BlockSpec((tl, 2 * HY_ORDER * ch), lambda i: (i, 0)),
        out_shape=jax.ShapeDtypeStruct((length, 2 * HY_ORDER * ch), BF16),
        compiler_params=_cparams("parallel"),
        name="hy_filter",
    )(z, t, w1p, b1.reshape(1, fw), w2, b2.reshape(1, fw), w3, deltas)


def _dft_tables(length):
    n2 = 2 * length
    theta = 2.0 * math.pi / n2
    n = jnp.arange(length, dtype=jnp.int32)[None, :]
    a = 8 * jnp.arange(length // 8, dtype=jnp.int32)[:, None]
    b = jnp.arange(8, dtype=jnp.int32)[:, None]
    ang_a = ((a * n) % n2).astype(F32) * theta
    ang_b = ((b * n) % n2).astype(F32) * theta
    ca, sa = jnp.cos(ang_a)[:, None, :], jnp.sin(ang_a)[:, None, :]
    cb, sb = jnp.cos(ang_b)[None, :, :], jnp.sin(ang_b)[None, :, :]
    fc = (ca * cb - sa * sb).reshape(length, length)
    fs = -(sa * cb + ca * sb).reshape(length, length)
    return fc.astype(BF16), fs.astype(BF16)


def _alternating_sum(x):
    col = lax.broadcasted_iota(jnp.int32, (8, x.shape[0]), 1)
    alt = (1 - 2 * (col & 1)).astype(F32).astype(BF16)
    return _dot(alt, x)[0:1]


def _filter_dft_kernel(hs_ref, hd_ref, fc_ref, fs_ref, o_ref):
    hs = hs_ref[...]
    o_ref[0] = _dot(fc_ref[...], hs)
    ki = _dot(fs_ref[...], hd_ref[...])
    nyq = _alternating_sum(hs)
    tf = ki.shape[0]
    row = pl.program_id(1) * tf + lax.broadcasted_iota(jnp.int32, (tf, 1), 0)
    o_ref[1] = jnp.where(row == 0, nyq, ki)


def _dft_spec_kernel(x_ref, fc_ref, fs_ref, k_ref, o_ref, *, inv_n):
    x = x_ref[...]
    ur = _dot(fc_ref[...], x)
    kr = k_ref[0]
    ki = k_ref[1]
    tf = ur.shape[0]
    row = pl.program_id(1) * tf + lax.broadcasted_iota(jnp.int32, (tf, 1), 0)
    first = row == 0
    ui = jnp.where(first, _alternating_sum(x), _dot(fs_ref[...], x))
    uiki = ui * ki
    yr = ur * kr - jnp.where(first, 0.0, uiki)
    yi = jnp.where(first, uiki, ur * ki + ui * kr)
    scale = jnp.where(first, inv_n, 2.0 * inv_n)
    o_ref[0] = (yr * scale).astype(BF16)
    o_ref[1] = (yi * scale).astype(BF16)


def _filter_spectrum(p, fc, fs, ch):
    length, cols = p.shape
    tf = min(512, length)
    return pl.pallas_call(
        _filter_dft_kernel,
        grid=(HY_ORDER, length // tf),
        in_specs=[pl.BlockSpec((length, ch), lambda o, i: (0, 2 * o)),
                  pl.BlockSpec((length, ch), lambda o, i: (0, 2 * o + 1)),
                  pl.BlockSpec((tf, length), lambda o, i: (i, 0)),
                  pl.BlockSpec((tf, length), lambda o, i: (i, 0))],
        out_specs=pl.BlockSpec((None, 2, tf, ch), lambda o, i: (o, 0, i, 0)),
        out_shape=jax.ShapeDtypeStruct((HY_ORDER, 2, length, ch), F32),
        compiler_params=_cparams("parallel", "arbitrary"),
        name="hy_filter_dft",
    )(p, p, fc, fs)


def _hy_spectral(xb, col, ch, fc, fs, kspec):
    b, length, _ = xb.shape
    tf = min(512, length)
    return pl.pallas_call(
        functools.partial(_dft_spec_kernel, inv_n=1.0 / (2 * length)),
        grid=(b, length // tf),
        in_specs=[pl.BlockSpec((None, length, ch), lambda bi, i: (bi, 0, col)),
                  pl.BlockSpec((tf, length), lambda bi, i: (i, 0)),
                  pl.BlockSpec((tf, length), lambda bi, i: (i, 0)),
                  pl.BlockSpec((2, tf, ch), lambda bi, i: (0, i, 0))],
        out_specs=pl.BlockSpec((None, 2, tf, ch), lambda bi, i: (bi, 0, i, 0)),
        out_shape=jax.ShapeDtypeStruct((b, 2, length, ch), BF16),
        compiler_params=_cparams("parallel", "arbitrary"),
        name="hy_dft_spectral",
    )(xb, fc, fs, kspec)


def _idft_gate_kernel(y_ref, fc_ref, fs_ref, u_ref, xg_ref, d_ref, o_ref, ob_ref):
    tt = fc_ref.shape[0]
    t = pl.program_id(1) * tt + lax.broadcasted_iota(jnp.int32, (tt, 1), 0)
    sign = (1 - 2 * (t & 1)).astype(F32)
    conv = (_dot(fc_ref[...], y_ref[0]) + _dot(fs_ref[...], y_ref[1])
            + sign * y_ref[1, 0:1, :].astype(F32))
    out = xg_ref[...] * (conv + u_ref[...] * d_ref[...])
    o_ref[...] = out
    ob_ref[...] = out.astype(BF16)


def _hy_inverse_gate(y, fc, fs, u, u_col, xg, xg_col, dvec, ch):
    b, _, length, _ = y.shape
    tt = min(512, length)
    out = pl.BlockSpec((None, tt, ch), lambda bi, i: (bi, i, 0))
    return pl.pallas_call(
        _idft_gate_kernel,
        grid=(b, length // tt),
        in_specs=[pl.BlockSpec((None, 2, length, ch), lambda bi, i: (bi, 0, 0, 0)),
                  pl.BlockSpec((tt, length), lambda bi, i: (i, 0)),
                  pl.BlockSpec((tt, length), lambda bi, i: (i, 0)),
                  pl.BlockSpec((None, tt, ch), lambda bi, i: (bi, i, u_col)),
                  pl.BlockSpec((None, tt, ch), lambda bi, i: (bi, i, xg_col)),
                  pl.BlockSpec((1, ch), lambda bi, i: (0, 0))],
        out_specs=[out, out],
        out_shape=[jax.ShapeDtypeStruct((b, length, ch), F32), jax.ShapeDtypeStruct((b, length, ch), BF16)],
        compiler_params=_cparams("parallel", "arbitrary"),
        name="hy_idft_gate",
    )(y, fc, fs, u, xg, dvec.reshape(1, ch))


def _hyena(hyc, hyb, w1, b1, w2, b2, w3, hy_bias):
    ch = hy_bias.shape[-1]
    length = hyc.shape[1]
    fc, fs = _dft_tables(length)
    kspec = _filter_spectrum(_hy_filter_inputs(length, w1, b1, w2, b2, w3, ch), fc, fs, ch)
    y = _hy_spectral(hyb, 0, ch, fc, fs, kspec[0])
    y1, y1b = _hy_inverse_gate(y, fc, fs, hyc, 0, hyc, 1, hy_bias[0], ch)
    y = _hy_spectral(y1b, 0, ch, fc, fs, kspec[1])
    y2, _ = _hy_inverse_gate(y, fc, fs, y1, 0, hyc, 2, hy_bias[1], ch)
    return y2


def _ab_out_kernel(o_ref, y_ref, x_ref, g1_ref, woa_ref, woh_ref, out_ref):
    y = _dot(o_ref[...], woa_ref[...]) + _dot(y_ref[...].astype(BF16), woh_ref[...])
    out_ref[...] = x_ref[...] + g1_ref[...] * y


def _ab_out(o, y2, x, mods, row_fn, w):
    b, n, d = x.shape
    tm = min(ROW_TILE, n)
    tok = lambda width: pl.BlockSpec((None, tm, width), lambda bi, i: (bi, i, 0))
    return pl.pallas_call(
        _ab_out_kernel,
        grid=(b, n // tm),
        in_specs=[tok(o.shape[-1]), tok(y2.shape[-1]), tok(d), _mod_spec(row_fn, 2, d),
                  _resident(w["wo_a"].shape), _resident(w["wo_h"].shape)],
        out_specs=tok(d),
        out_shape=jax.ShapeDtypeStruct((b, n, d), F32),
        compiler_params=_cparams("parallel", "parallel"),
        name="ab_out",
    )(o, y2, x, mods, w["wo_a"], w["wo_h"])


FFN_ROWS = 512
FFN_CHUNK_MAX = 2816


def _ffn_kernel(*refs, hidden, hc, final):
    if final:
        (x_ref, xp_ref, xn_ref, sh_ref, sc_ref, g2_ref, ng_ref, wup_ref, cw_ref, cb_ref, wdn_ref,
         fg_ref, o_ref) = refs
    else:
        (x_ref, xp_ref, xn_ref, sh_ref, sc_ref, g2_ref, ng_ref, wup_ref, cw_ref, cb_ref, wdn_ref,
         o_ref) = refs
    x = x_ref[...]
    tm, d = x.shape
    he = _halo_rows(x_ref, xp_ref, xn_ref, lambda a: _modulate(a, ng_ref[...], sh_ref[...], sc_ref[...]))

    def conv(a, col):
        return _conv3_rows(a, cw_ref[:, col:col + hc], cb_ref[:, col:col + hc], tm)

    acc = jnp.zeros((tm, d), F32)
    for j in range(hidden // hc):
        gate = conv(_dot(he, wup_ref[:, j * hc:(j + 1) * hc]), j * hc)
        val = conv(_dot(he, wup_ref[:, hidden + j * hc:hidden + (j + 1) * hc]), hidden + j * hc)
        mid = (_silu(gate) * val).astype(BF16)
        acc = acc + _dot(mid, wdn_ref[j * hc:(j + 1) * hc, :])
    out = x + g2_ref[...] * acc
    if final:
        out = _rms(out, fg_ref[...])
    o_ref[...] = out


def _ffn_chunk(hidden):
    for hc in range(min(FFN_CHUNK_MAX, hidden) // HEAD_LANES * HEAD_LANES, 0, -HEAD_LANES):
        if hidden % hc == 0:
            return hc
    raise ValueError(f"FFN hidden width {hidden} is not a multiple of {HEAD_LANES}")


def _conv_ffn(x, mods, row_fn, norm_g, w_up, conv_w, conv_b, w_down, final_g=None):
    b, n, d = x.shape
    hidden = w_down.shape[0]
    tm = min(FFN_ROWS, n)
    final = final_g is not None
    tok = pl.BlockSpec((None, tm, d), lambda bi, i: (bi, i, 0))
    in_specs = _halo_specs(tm, n, d) + [
        _mod_spec(row_fn, 3, d), _mod_spec(row_fn, 4, d), _mod_spec(row_fn, 5, d),
        _resident((1, d)), _resident(w_up.shape), _resident(conv_w.shape),
        _resident((1, 2 * hidden)), _resident(w_down.shape)]
    args = [x, x, x, mods, mods, mods, norm_g.reshape(1, d), w_up, conv_w, conv_b.reshape(1, 2 * hidden), w_down]
    if final:
        in_specs.append(_resident((1, d)))
        args.append(final_g.reshape(1, d))
    return pl.pallas_call(
        functools.partial(_ffn_kernel, hidden=hidden, hc=_ffn_chunk(hidden), final=final),
        grid=(b, n // tm),
        in_specs=in_specs,
        out_specs=tok,
        out_shape=jax.ShapeDtypeStruct((b, n, d), F32),
        compiler_params=_cparams("parallel", "parallel"),
        name="conv_ffn",
    )(*args)


def _hg_proj_kernel(x_ref, sh_ref, sc_ref, g_ref, w_ref, o_ref, *, width):
    h = _modulate(x_ref[...], g_ref[...], sh_ref[...], sc_ref[...]).astype(BF16)
    z = _dot(h, w_ref[...])
    o_ref[:, :width] = _silu(z[:, :width])
    o_ref[:, width:] = z[:, width:]


def _hg_project(x, mods, row_fn, norm_g, w_in):
    b, n, d = x.shape
    cols = w_in.shape[1]
    tm = min(ROW_TILE, n)
    tok = lambda width: pl.BlockSpec((None, tm, width), lambda bi, i: (bi, i, 0))
    return pl.pallas_call(
        functools.partial(_hg_proj_kernel, width=cols // 5),
        grid=(b, n // tm),
        in_specs=[tok(d), _mod_spec(row_fn, 0, d), _mod_spec(row_fn, 1, d), _resident((1, d)),
                  _resident(w_in.shape)],
        out_specs=tok(cols),
        out_shape=jax.ShapeDtypeStruct((b, n, cols), F32),
        compiler_params=_cparams("parallel", "parallel"),
        name="hg_project",
    )(x, mods, mods, norm_g.reshape(1, d), w_in)


def _scan_tables(direction):
    c = HG_CHUNK
    idx = np.arange(c)
    level = np.full((c, c), -1, np.int32)
    for li, m in enumerate(HG_LEVELS):
        same = (idx[:, None] // (2 * m)) == (idx[None, :] // (2 * m))
        up = (idx % (2 * m)) >= m
        pair = same & (up[:, None] & ~up[None, :] if direction == 0 else ~up[:, None] & up[None, :])
        level[pair] = li
    level[idx, idx] = len(HG_LEVELS)
    if direction == 0:
        to_q = (idx[None, :] <= idx[:, None])
    else:
        to_q = (idx[None, :] >= idx[:, None])
    em = np.concatenate([to_q.astype(np.float32), np.ones((8, c), np.float32)], axis=0)
    return jnp.asarray(np.concatenate([em, em], axis=1), BF16), jnp.asarray(level)


def _level_exponent(run, m, rev):
    c, w = run.shape
    off = m if rev else m - 1
    if 2 * m >= 8:
        parts = [jnp.broadcast_to(run[a + off:a + off + 1, :], (2 * m, w)) for a in range(0, c, 2 * m)]
    elif m == 2:
        low = lax.broadcasted_iota(jnp.int32, (8, w), 0) < 4
        parts = [jnp.where(low, jnp.broadcast_to(run[a + off:a + off + 1, :], (8, w)),
                           jnp.broadcast_to(run[a + 4 + off:a + 5 + off, :], (8, w))) for a in range(0, c, 8)]
    else:
        odd = (lax.broadcasted_iota(jnp.int32, (c, w), 0) & 1) == 1
        if rev:
            parts = [jnp.where(odd, run, pltpu.roll(run, c - 1, axis=0))]
        else:
            parts = [jnp.where(odd, pltpu.roll(run, 1, axis=0), run)]
    ref = parts[0] if len(parts) == 1 else jnp.concatenate(parts, axis=0)
    bits = lax.bitcast_convert_type(run - ref, jnp.uint32) | jnp.uint32(0x80000000)
    return lax.bitcast_convert_type(bits, F32)


def _gla_group(chunks, lb, em2, lv, st, rev):
    c = HG_CHUNK
    e_ = HG_EXPAND
    nl = len(HG_LEVELS)
    log_lb = jnp.log(lb)
    log_1m = jnp.log1p(-lb)
    log_f = []
    keys = []
    for _, fr, _ in chunks:
        t = jnp.exp(-jnp.abs(fr))
        d = 1.0 + t
        keys.append((1.0 - lb) * (jnp.where(fr > 0, t, 1.0) / d))
        b = log_1m + (jnp.minimum(fr, 0.0) - jnp.log(d))
        log_f.append(jnp.maximum(log_lb, b) + jnp.log(1.0 + jnp.exp(-jnp.abs(log_lb - b))))
    g = jnp.concatenate(log_f, axis=1) * math.log2(math.e)
    hi = g.astype(BF16)
    lo = (g - hi.astype(F32)).astype(BF16)
    e_all = _dot(em2, jnp.concatenate([hi, lo], axis=0))
    outs = []
    for gi, (q, _, val) in enumerate(chunks):
        to_q = e_all[:c, gi * e_:(gi + 1) * e_]
        total = e_all[c:c + 1, gi * e_:(gi + 1) * e_]
        key = keys[gi]
        att = jnp.zeros((c, c), F32)
        for li, m in enumerate(HG_LEVELS):
            dec = jnp.exp2(_level_exponent(to_q, m, rev))
            att = jnp.where(lv == li, _dot((q * dec).astype(BF16), (key * dec).astype(BF16), _NT), att)
        att = jnp.where(lv == nl, jnp.sum(q * key, axis=1, keepdims=True), att)
        vb = val.astype(BF16)
        q_in = (q * jnp.exp2(to_q)).astype(BF16)
        outs.append(_dot(att.astype(BF16), vb) + _dot(q_in, st.astype(BF16), _NT))
        k_out = (key * jnp.exp2(total - to_q)).astype(BF16)
        st = st * jnp.exp2(total) + _dot(vb, k_out, _TN)
    return outs, st


def _hg_scan_kernel(*refs, layer, need_ctx):
    (lg_ref, emf_ref, emb_ref, lvf_ref, lvb_ref,
     qc_ref, ffc_ref, fbc_ref, ic_ref, q_ref, ff_ref, fb_ref, i_ref) = refs[:13]
    if need_ctx:
        o_ref, oc_ref, st_ref = refs[13:]
    else:
        o_ref, st_ref = refs[13:]
        oc_ref = None
    c = HG_CHUNK
    lg = lg_ref[...]
    ex = jnp.exp(lg - jnp.max(lg, axis=0, keepdims=True))
    probs = ex / jnp.sum(ex, axis=0, keepdims=True)
    acc = probs[0]
    for l in range(1, layer + 1):
        acc = acc + probs[l]
    lb_all = acc - probs[0]
    lb_f = lb_all[0:1]
    lb_b = lb_all[1:2]

    st_ref[...] = jnp.zeros_like(st_ref)
    o_ref[...] = jnp.zeros_like(o_ref)
    if need_ctx:
        oc_ref[...] = jnp.zeros_like(oc_ref)

    def run(qr, ffr, fbr, ir, outr):
        nc = qr.shape[0] // c
        grp = math.gcd(nc, HG_GROUP)
        ng = nc // grp

        def body(gi, carry):
            rows_f = [pl.multiple_of((gi * grp + k) * c, c) for k in range(grp)]
            rows_b = [pl.multiple_of((nc - 1 - gi * grp - k) * c, c) for k in range(grp)]
            load = lambda gate_ref, r: (qr[pl.ds(r, c), :], gate_ref[pl.ds(r, c), :], ir[pl.ds(r, c), :])
            of, stf = _gla_group([load(ffr, r) for r in rows_f], lb_f, emf_ref[...], lvf_ref[...], st_ref[0], False)
            st_ref[0] = stf
            ob, stb = _gla_group([load(fbr, r) for r in rows_b], lb_b, emb_ref[...], lvb_ref[...], st_ref[1], True)
            st_ref[1] = stb
            if outr is not None:
                for r, o in zip(rows_f + rows_b, of + ob):
                    outr[pl.ds(r, c), :] += o
            return carry

        lax.fori_loop(0, ng, body, 0, unroll=min(HG_UNROLL, ng))

    run(qc_ref, ffc_ref, fbc_ref, ic_ref, oc_ref)
    run(q_ref, ff_ref, fb_ref, i_ref, o_ref)


def _hg_scan(z, zc, lb_logits, layer, need_ctx):
    b, n, cols = z.shape
    nc_ = zc.shape[1]
    width = cols // 5
    heads = width // HG_EXPAND
    e = HG_EXPAND
    depth = lb_logits.shape[0]
    emf, lvf = _scan_tables(0)
    emb, lvb = _scan_tables(1)
    col = lambda rows, group: pl.BlockSpec((None, rows, e), lambda bi, h: (bi, 0, group * heads + h))
    in_specs = [pl.BlockSpec((depth, 2, e), lambda bi, h: (0, 0, h)),
                _resident(emf.shape), _resident(emb.shape), _resident(lvf.shape), _resident(lvb.shape),
                col(nc_, 0), col(nc_, 1), col(nc_, 2), col(nc_, 3),
                col(n, 0), col(n, 1), col(n, 2), col(n, 3)]
    out_specs = [pl.BlockSpec((None, n, e), lambda bi, h: (bi, 0, h))]
    out_shape = [jax.ShapeDtypeStruct((b, n, width), F32)]
    if need_ctx:
        out_specs.append(pl.BlockSpec((None, nc_, e), lambda bi, h: (bi, 0, h)))
        out_shape.append(jax.ShapeDtypeStruct((b, nc_, width), F32))
    res = pl.pallas_call(
        functools.partial(_hg_scan_kernel, layer=layer, need_ctx=need_ctx),
        grid=(b, heads),
        in_specs=in_specs,
        out_specs=out_specs,
        out_shape=out_shape,
        scratch_shapes=[pltpu.VMEM((2, e, e), F32)],
        compiler_params=_cparams("parallel", "parallel"),
        name="hg_scan",
    )(lb_logits, emf, emb, lvf, lvb, zc, zc, zc, zc, z, z, z, z)
    return (res[0], res[1]) if need_ctx else (res[0], None)


def _hg_out_kernel(o_ref, g_ref, x_ref, g1_ref, ng_ref, w_ref, out_ref, *, heads):
    ng = ng_ref[...]
    parts = [_rms(o_ref[:, h * HG_EXPAND:(h + 1) * HG_EXPAND], ng) for h in range(heads)]
    o = jnp.concatenate(parts, axis=1) * _silu(g_ref[...])
    out_ref[...] = x_ref[...] + g1_ref[...] * _dot(o.astype(BF16), w_ref[...])


def _hg_out(o, z, x, mods, row_fn, norm_g, w_out):
    b, n, d = x.shape
    width = o.shape[-1]
    tm = min(ROW_TILE, n)
    tok = lambda wd: pl.BlockSpec((None, tm, wd), lambda bi, i: (bi, i, 0))
    return pl.pallas_call(
        functools.partial(_hg_out_kernel, heads=width // HG_EXPAND),
        grid=(b, n // tm),
        in_specs=[tok(width), pl.BlockSpec((None, tm, width), lambda bi, i: (bi, i, 4)), tok(d),
                  _mod_spec(row_fn, 2, d), _resident((1, HG_EXPAND)), _resident(w_out.shape)],
        out_specs=tok(d),
        out_shape=jax.ShapeDtypeStruct((b, n, d), F32),
        compiler_params=_cparams("parallel", "parallel"),
        name="hg_out",
    )(o, z, x, mods, norm_g.reshape(1, HG_EXPAND), w_out)


def kernel(x, c, ctx, c_ctx, mod_w, mod_b, norm1_g, norm2_g, ffn_w_up, ffn_conv_w, ffn_conv_b, ffn_w_down, ab_w_in, mla_q_norm_g, mla_w_q_b, mla_kv_norm_g, mla_w_kv_b, hy_conv_w, hy_conv_b, hy_w1, hy_b1, hy_w2, hy_b2, hy_w3, hy_bias, ab_w_out, hg_w_in, hg_lb_logits, hg_norm_g, hg_w_out, final_norm_g):
    batch, n, d = x.shape
    depth = mod_w.shape[0]
    assert batch < MOD_ROWS
    ctx_row = batch
    cc = jnp.zeros((MOD_ROWS, d), F32).at[:batch].set(c).at[ctx_row].set(c_ctx)
    mods_all = _mod_vectors(cc, mod_w, mod_b)
    lat_row = lambda bi, *_: bi
    ctx_row_fn = lambda bi, *_: ctx_row
    rope_tabs = _rope_tables(n)

    xc = ctx
    for layer in range(depth):
        need_ctx = layer < depth - 1
        j = layer // 2
        mods = mods_all[layer]
        if layer % 2 == 0:
            q_lora = mla_q_norm_g.shape[-1]
            kv_lora = mla_kv_norm_g.shape[-1]
            hy_ch = hy_bias.shape[-1]
            w = _ab_weights(ab_w_in[j], mla_w_q_b[j], mla_w_kv_b[j], ab_w_out[j], q_lora, kv_lora, hy_ch)
            hy_args = (hy_w1[j], hy_b1[j], hy_w2[j], hy_b2[j], hy_w3[j], hy_bias[j])
            n_ctx = xc.shape[1]
            q, k, v, hy, hyb = _ab_project(x, mods, lat_row, norm1_g[layer], w, mla_q_norm_g[j],
                                           mla_kv_norm_g[j], rope_tabs, hy_conv_w[j], hy_conv_b[j], hy_ch,
                                           kv_rows=n + n_ctx)
            q_c, k, v, hy_c, hyb_c = _ab_project(xc, mods, ctx_row_fn, norm1_g[layer], w, mla_q_norm_g[j],
                                                 mla_kv_norm_g[j], None, hy_conv_w[j], hy_conv_b[j], hy_ch,
                                                 kv_rows=n + n_ctx, kv_row0=n, kv_into=(k, v))
            o = _attention(q, [k], [v], w["heads"])
            x_new = _ab_out(o, _hyena(hy, hyb, *hy_args), x, mods, lat_row, w)
            if need_ctx:
                o_c = _attention(q_c, [k], [v], w["heads"], kv_rows=n_ctx, kv_row0=n)
                xc = _ab_out(o_c, _hyena(hy_c, hyb_c, *hy_args), xc, mods, ctx_row_fn, w)
            x = x_new
        else:
            w_in = hg_w_in[j].astype(BF16)
            z_c = _hg_project(xc, mods, ctx_row_fn, norm1_g[layer], w_in)
            z = _hg_project(x, mods, lat_row, norm1_g[layer], w_in)
            o, o_c = _hg_scan(z, z_c, hg_lb_logits, layer, need_ctx)
            w_out = hg_w_out[j].astype(BF16)
            x = _hg_out(o, z, x, mods, lat_row, hg_norm_g[j], w_out)
            if need_ctx:
                xc = _hg_out(o_c, z_c, xc, mods, ctx_row_fn, hg_norm_g[j], w_out)
        w_up = ffn_w_up[layer].astype(BF16)
        w_dn = ffn_w_down[layer].astype(BF16)
        last = layer == depth - 1
        x = _conv_ffn(x, mods, lat_row, norm2_g[layer], w_up, ffn_conv_w[layer], ffn_conv_b[layer], w_dn,
                      final_norm_g if last else None)
        if need_ctx:
            xc = _conv_ffn(xc, mods, ctx_row_fn, norm2_g[layer], w_up, ffn_conv_w[layer], ffn_conv_b[layer], w_dn)
    return x
```

```python
import functools
import math

import numpy as np
import jax
import jax.numpy as jnp
from jax import lax
from jax.experimental import pallas as pl
from jax.experimental.pallas import tpu as pltpu

F32 = jnp.float32
BF16 = jnp.bfloat16
HIGHEST = lax.Precision.HIGHEST

EPS = 1e-6
GRID_W = 64

MLA_NOPE = 64
MLA_ROPE = 32
MLA_V = 64
ROPE_NF = MLA_ROPE // 4
ROPE_BASE = 10000.0
HEAD_LANES = 128
ATTN_TQ = 4096
ATTN_SUB = 256

HY_ORDER = 2
HY_BANDS = 16
HY_SIN_FREQ = 1.0
HY_DECAY_TARGET = 1e-2
HY_FAST_DECAY = 0.3
HY_SLOW_DECAY = 1.5
HY_MIN_DECAY = math.log(HY_DECAY_TARGET) / HY_SLOW_DECAY
HY_MAX_DECAY = math.log(HY_DECAY_TARGET) / HY_FAST_DECAY

HG_EXPAND = 128
HG_CHUNK = 64
HG_LEVELS = (32, 16, 8, 4, 2, 1)
HG_GROUP = 4
HG_UNROLL = 2

ROW_TILE = 512
MOD_ROWS = 16
HALO = 8
VMEM_LIMIT = 56 * 1024 * 1024

_NT = (((1,), (1,)), ((), ()))
_TN = (((0,), (0,)), ((), ()))


def _cparams(*sem):
    return pltpu.CompilerParams(dimension_semantics=sem, vmem_limit_bytes=VMEM_LIMIT)


def _dot(a, b, dims=(((1,), (0,)), ((), ()))):
    return lax.dot_general(a, b, dims, precision=lax.Precision.DEFAULT, preferred_element_type=F32)


def _rms(x, g):
    return x * lax.rsqrt(jnp.mean(x * x, axis=-1, keepdims=True) + EPS) * g


def _modulate(x, g, shift, scale):
    return _rms(x, g) * (1.0 + scale) + shift


def _silu(x):
    return x * jax.nn.sigmoid(x)


def _mod_spec(row_fn, k, d):
    return pl.BlockSpec((None, None, 1, d), lambda *idx: (row_fn(*idx), k, 0, 0))


def _resident(shape):
    nd = len(shape)
    return pl.BlockSpec(shape, lambda *idx: (0,) * nd, pipeline_mode=pl.Buffered(1))


def _mod_kernel(c_ref, w_ref, b_ref, o_ref):
    a = _silu(c_ref[...])
    o_ref[...] = jnp.dot(a, w_ref[...], precision=HIGHEST, preferred_element_type=F32) + b_ref[...]


def _mod_vectors(cc, mod_w, mod_b):
    depth, d, n6 = mod_w.shape
    tn = ROW_TILE
    out = pl.pallas_call(
        _mod_kernel,
        grid=(depth, n6 // tn),
        in_specs=[pl.BlockSpec((MOD_ROWS, d), lambda l, j: (0, 0)),
                  pl.BlockSpec((None, d, tn), lambda l, j: (l, 0, j)),
                  pl.BlockSpec((None, 1, tn), lambda l, j: (l, 0, j))],
        out_specs=pl.BlockSpec((None, MOD_ROWS, tn), lambda l, j: (l, 0, j)),
        out_shape=jax.ShapeDtypeStruct((depth, MOD_ROWS, n6), F32),
        compiler_params=_cparams("parallel", "parallel"),
        name="mod_vectors",
    )(cc, mod_w, mod_b.reshape(depth, 1, n6))
    return out.reshape(depth, MOD_ROWS, 6, 1, d)


def _halo_rows(x_ref, xp_ref, xn_ref, mod):
    i = pl.program_id(1)
    last = pl.num_programs(1) - 1
    hp = mod(xp_ref[...]) * (i > 0).astype(F32)
    hn = mod(xn_ref[...]) * (i < last).astype(F32)
    return jnp.concatenate([hp, mod(x_ref[...]), hn], axis=0).astype(BF16)


def _conv3_rows(a, w, bias, tm):
    ne = tm + 2 * HALO
    prev = pltpu.roll(a, 1, axis=0)[HALO:HALO + tm]
    nxt = pltpu.roll(a, ne - 1, axis=0)[HALO:HALO + tm]
    return prev * w[0:1] + a[HALO:HALO + tm] * w[1:2] + nxt * w[2:3] + bias


def _ab_proj_kernel(*refs, rope, q_lora, kv_lora, heads, scale, ch):
    if rope:
        (x_ref, xp_ref, xn_ref, sh_ref, sc_ref, g_ref, win_ref, qg_ref, wqa_ref, wqb_ref, kvg_ref, wk_ref,
         wv_ref, cw_ref, cb_ref, cos_ref, sin_ref, q_ref, k_ref, v_ref, hy_ref, hyb_ref) = refs
    else:
        (x_ref, xp_ref, xn_ref, sh_ref, sc_ref, g_ref, win_ref, qg_ref, wqa_ref, kvg_ref, wk_ref,
         wv_ref, cw_ref, cb_ref, q_ref, k_ref, v_ref, hy_ref, hyb_ref) = refs
    tm = x_ref.shape[0]
    he = _halo_rows(x_ref, xp_ref, xn_ref, lambda a: _modulate(a, g_ref[...], sh_ref[...], sc_ref[...]))
    ze = _dot(he, win_ref[...])
    o = q_lora + kv_lora
    hy = _conv3_rows(ze[:, o + 2 * HEAD_LANES:], cw_ref[...], cb_ref[...], tm)
    hy_ref[...] = hy
    hyb_ref[...] = hy[:, :ch].astype(BF16)
    z = ze[HALO:HALO + tm]
    qn = _rms(z[:, :q_lora], qg_ref[...]).astype(BF16)
    kvn = _rms(z[:, q_lora:o], kvg_ref[...]).astype(BF16)
    v_ref[...] = _dot(kvn, wv_ref[...]).astype(BF16)
    qa = _dot(qn, wqa_ref[...])
    kk = _dot(kvn, wk_ref[...])
    kr = z[:, o:o + HEAD_LANES]
    if rope:
        cos = cos_ref[...]
        sin = sin_ref[...]
        qb = _dot(qn, wqb_ref[...])
        kr = kr * cos + z[:, o + HEAD_LANES:o + 2 * HEAD_LANES] * sin
    for hh in range(heads):
        sl = slice(hh * HEAD_LANES, (hh + 1) * HEAD_LANES)
        qh = qa[:, sl]
        if rope:
            qh = qh * cos + qb[:, sl] * sin
        q_ref[:, sl] = (qh * scale).astype(BF16)
        k_ref[:, sl] = (kk[:, sl] + kr).astype(BF16)


def _rope_swap(w):
    nf = ROPE_NF
    return jnp.concatenate([-w[:, nf:2 * nf], w[:, :nf], -w[:, 3 * nf:4 * nf], w[:, 2 * nf:3 * nf]], axis=1)


def _ab_weights(w_in, w_q_b, w_kv_b, w_out, q_lora, kv_lora, hy_ch):
    d = w_in.shape[0]
    heads = w_q_b.shape[1] // (MLA_NOPE + MLA_ROPE)
    o = q_lora + kv_lora
    w_kr = w_in[:, o:o + MLA_ROPE]
    zpad = lambda n: jnp.zeros((d, n), F32)
    tail = HEAD_LANES - MLA_NOPE - MLA_ROPE
    win = jnp.concatenate(
        [w_in[:, :o],
         zpad(MLA_NOPE), w_kr, zpad(tail),
         zpad(MLA_NOPE), _rope_swap(w_kr), zpad(tail),
         w_in[:, o + MLA_ROPE:]], axis=1).astype(BF16)
    wq = w_q_b.reshape(q_lora, heads, MLA_NOPE + MLA_ROPE)
    zq = jnp.zeros((q_lora, heads, tail), F32)
    wqa = jnp.concatenate([wq, zq], axis=2).reshape(q_lora, heads * HEAD_LANES).astype(BF16)
    wq_rope = wq[:, :, MLA_NOPE:].reshape(q_lora * heads, MLA_ROPE)
    wqb = jnp.concatenate([jnp.zeros((q_lora, heads, MLA_NOPE), F32),
                           _rope_swap(wq_rope).reshape(q_lora, heads, MLA_ROPE), zq],
                          axis=2).reshape(q_lora, heads * HEAD_LANES).astype(BF16)
    wkv = w_kv_b.reshape(kv_lora, heads, MLA_NOPE + MLA_V)
    zk = jnp.zeros((kv_lora, heads, HEAD_LANES - MLA_NOPE), F32)
    wk = jnp.concatenate([wkv[:, :, :MLA_NOPE], zk], axis=2).reshape(kv_lora, heads * HEAD_LANES).astype(BF16)
    zv = jnp.zeros((kv_lora, heads, HEAD_LANES - MLA_V), F32)
    wv = jnp.concatenate([wkv[:, :, MLA_NOPE:], zv], axis=2).reshape(kv_lora, heads * HEAD_LANES).astype(BF16)
    wo = w_out[:heads * MLA_V].reshape(heads, MLA_V, d)
    wo_a = jnp.concatenate([wo, jnp.zeros((heads, HEAD_LANES - MLA_V, d), F32)],
                           axis=1).reshape(heads * HEAD_LANES, d).astype(BF16)
    wo_h = w_out[heads * MLA_V:].astype(BF16)
    return dict(win=win, wqa=wqa, wqb=wqb, wk=wk, wv=wv, wo_a=wo_a, wo_h=wo_h, heads=heads)


def _rope_tables(n):
    rows = n // GRID_W
    row = jnp.repeat(jnp.arange(rows), GRID_W).astype(F32)
    col = jnp.tile(jnp.arange(GRID_W), rows).astype(F32)
    inv = ROPE_BASE ** (-jnp.arange(ROPE_NF, dtype=F32) / ROPE_NF)
    ar = row[:, None] * inv
    ac = col[:, None] * inv
    ones = jnp.ones((n, MLA_NOPE), F32)
    tail = HEAD_LANES - MLA_NOPE - MLA_ROPE
    cos = jnp.concatenate([ones, jnp.cos(ar), jnp.cos(ar), jnp.cos(ac), jnp.cos(ac),
                           jnp.ones((n, tail), F32)], axis=1)
    sin = jnp.concatenate([0 * ones, jnp.sin(ar), jnp.sin(ar), jnp.sin(ac), jnp.sin(ac),
                           jnp.zeros((n, tail), F32)], axis=1)
    return cos, sin


def _halo_specs(tm, n, d):
    nh = n // HALO
    tpb = tm // HALO
    return [pl.BlockSpec((None, tm, d), lambda bi, i: (bi, i, 0)),
            pl.BlockSpec((None, HALO, d), lambda bi, i: (bi, jnp.maximum(i * tpb - 1, 0), 0)),
            pl.BlockSpec((None, HALO, d), lambda bi, i: (bi, jnp.minimum((i + 1) * tpb, nh - 1), 0))]


def _ab_project(x, mods, row_fn, norm_g, w, q_norm_g, kv_norm_g, rope_tabs, conv_w, conv_b, ch):
    b, n, d = x.shape
    tm = min(ROW_TILE, n)
    heads = w["heads"]
    q_lora = q_norm_g.shape[-1]
    kv_lora = kv_norm_g.shape[-1]
    hw = heads * HEAD_LANES
    hy_cols = conv_w.shape[-1]
    rope = rope_tabs is not None
    tok = lambda width: pl.BlockSpec((None, tm, width), lambda bi, i: (bi, i, 0))
    in_specs = _halo_specs(tm, n, d) + [
        _mod_spec(row_fn, 0, d), _mod_spec(row_fn, 1, d), _resident((1, d)),
        _resident(w["win"].shape), _resident((1, q_lora)), _resident(w["wqa"].shape)]
    args = [x, x, x, mods, mods, norm_g.reshape(1, d), w["win"], q_norm_g.reshape(1, q_lora), w["wqa"]]
    if rope:
        in_specs.append(_resident(w["wqb"].shape))
        args.append(w["wqb"])
    in_specs += [_resident((1, kv_lora)), _resident(w["wk"].shape), _resident(w["wv"].shape),
                 _resident(conv_w.shape), _resident((1, hy_cols))]
    args += [kv_norm_g.reshape(1, kv_lora), w["wk"], w["wv"], conv_w, conv_b.reshape(1, hy_cols)]
    if rope:
        in_specs += [pl.BlockSpec((tm, HEAD_LANES), lambda bi, i: (i, 0))] * 2
        args += list(rope_tabs)
    kern = functools.partial(_ab_proj_kernel, rope=rope, q_lora=q_lora, kv_lora=kv_lora, heads=heads, ch=ch,
                             scale=math.log2(math.e) / math.sqrt(MLA_NOPE + MLA_ROPE))
    return pl.pallas_call(
        kern,
        grid=(b, n // tm),
        in_specs=in_specs,
        out_specs=[tok(hw), tok(hw), tok(hw), tok(hy_cols), tok(ch)],
        out_shape=[jax.ShapeDtypeStruct((b, n, hw), BF16)] * 3 + [jax.ShapeDtypeStruct((b, n, hy_cols), F32),
                                                                   jax.ShapeDtypeStruct((b, n, ch), BF16)],
        compiler_params=_cparams("parallel", "parallel"),
        name="ab_project",
    )(*args)


def _attn_kernel(*refs, sub, nseg):
    q_ref, o_ref = refs[0], refs[-1]
    ks = [r[...] for r in refs[1:1 + nseg]]
    vs = [r[...] for r in refs[1 + nseg:1 + 2 * nseg]]
    for r in range(0, q_ref.shape[0], sub):
        q = q_ref[r:r + sub, :]
        ss = [_dot(q, k, _NT) for k in ks]
        m = functools.reduce(jnp.maximum, [jnp.max(s, axis=-1, keepdims=True) for s in ss])
        ps = [jnp.exp2(s - m) for s in ss]
        l = sum(jnp.sum(p, axis=-1, keepdims=True) for p in ps)
        o = sum(_dot(p.astype(BF16), v) for p, v in zip(ps, vs))
        o_ref[r:r + sub, :] = (o / l).astype(o_ref.dtype)


def _attention(q, ks, vs, heads):
    b, n, hw = q.shape
    tq = min(ATTN_TQ, n)
    tile = pl.BlockSpec((None, tq, HEAD_LANES), lambda bi, h, i: (bi, i, h))
    whole = lambda a: pl.BlockSpec((None, a.shape[1], HEAD_LANES), lambda bi, h, i: (bi, 0, h))
    return pl.pallas_call(
        functools.partial(_attn_kernel, sub=min(ATTN_SUB, tq), nseg=len(ks)),
        grid=(b, heads, n // tq),
        in_specs=[tile] + [whole(a) for a in ks] + [whole(a) for a in vs],
        out_specs=tile,
        out_shape=jax.ShapeDtypeStruct((b, n, hw), BF16),
        compiler_params=_cparams("parallel", "parallel", "parallel"),
        name="attention",
    )(q, *ks, *vs)


def _hy_filter_kernel(z_ref, t_ref, w1_ref, b1_ref, w2_ref, b2_ref, w3_ref, dl_ref, o_ref, *, ch):
    hdot = lambda a, w: jnp.dot(a, w, precision=HIGHEST, preferred_element_type=F32)
    a = jnp.sin(HY_SIN_FREQ * (hdot(z_ref[...], w1_ref[...]) + b1_ref[...]))
    a = jnp.sin(HY_SIN_FREQ * (hdot(a, w2_ref[...]) + b2_ref[...]))
    f = hdot(a, w3_ref[...])
    dec = jnp.exp(-t_ref[...] * dl_ref[...])
    tl = z_ref.shape[0]
    row = pl.program_id(0) * tl + lax.broadcasted_iota(jnp.int32, (tl, ch), 0)
    for o in range(HY_ORDER):
        hf = f[:, (2 * o) * ch:(2 * o + 1) * ch] * dec
        hb = jnp.where(row > 0, f[:, (2 * o + 1) * ch:(2 * o + 2) * ch] * dec, 0.0)
        o_ref[:, (2 * o) * ch:(2 * o + 1) * ch] = (hf + hb).astype(BF16)
        o_ref[:, (2 * o + 1) * ch:(2 * o + 2) * ch] = (hf - hb).astype(BF16)


def _hy_filter_inputs(length, w1, b1, w2, b2, w3, ch):
    t = jnp.linspace(0.0, 1.0, length, dtype=F32)[:, None]
    wv = 2.0 * math.pi * jnp.arange(length, dtype=F32)[:, None] / length
    f = jnp.linspace(1e-4, HY_BANDS - 1, HY_BANDS, dtype=F32)
    z = jnp.concatenate([t, jnp.cos(f * wv), -jnp.sin(f * wv)], axis=-1)
    emb = z.shape[1]
    z = jnp.pad(z, ((0, 0), (0, HEAD_LANES - emb)))
    w1p = jnp.pad(w1, ((0, HEAD_LANES - emb), (0, 0)))
    deltas = jnp.abs(jnp.linspace(HY_MIN_DECAY, HY_MAX_DECAY, ch, dtype=F32))[None, :]
    fw = w1.shape[1]
    tl = min(ROW_TILE, length)
    return pl.pallas_call(
        functools.partial(_hy_filter_kernel, ch=ch),
        grid=(length // tl,),
        in_specs=[pl.BlockSpec((tl, HEAD_LANES), lambda i: (i, 0)), pl.BlockSpec((tl, 1), lambda i: (i, 0)),
                  _resident((HEAD_LANES, fw)), _resident((1, fw)), _resident((fw, fw)), _resident((1, fw)),
                  _resident(w3.shape), _resident((1, ch))],
        out_specs=pl.BlockSpec((tl, 2 * HY_ORDER * ch), lambda i: (i, 0)),
        out_shape=jax.ShapeDtypeStruct((length, 2 * HY_ORDER * ch), BF16),
        compiler_params=_cparams("parallel"),
        name="hy_filter",
    )(z, t, w1p, b1.reshape(1, fw), w2, b2.reshape(1, fw), w3, deltas)


def _dft_tables(length):
    n2 = 2 * length
    theta = 2.0 * math.pi / n2
    n = jnp.arange(length, dtype=jnp.int32)[None, :]
    a = 8 * jnp.arange(length // 8, dtype=jnp.int32)[:, None]
    b = jnp.arange(8, dtype=jnp.int32)[:, None]
    ang_a = ((a * n) % n2).astype(F32) * theta
    ang_b = ((b * n) % n2).astype(F32) * theta
    ca, sa = jnp.cos(ang_a)[:, None, :], jnp.sin(ang_a)[:, None, :]
    cb, sb = jnp.cos(ang_b)[None, :, :], jnp.sin(ang_b)[None, :, :]
    fc = (ca * cb - sa * sb).reshape(length, length)
    fs = -(sa * cb + ca * sb).reshape(length, length)
    return fc.astype(BF16), fs.astype(BF16)


def _alternating_sum(x):
    col = lax.broadcasted_iota(jnp.int32, (8, x.shape[0]), 1)
    alt = (1 - 2 * (col & 1)).astype(F32).astype(BF16)
    return _dot(alt, x)[0:1]


def _filter_dft_kernel(hs_ref, hd_ref, fc_ref, fs_ref, o_ref):
    hs = hs_ref[...]
    o_ref[0] = _dot(fc_ref[...], hs)
    ki = _dot(fs_ref[...], hd_ref[...])
    nyq = _alternating_sum(hs)
    tf = ki.shape[0]
    row = pl.program_id(1) * tf + lax.broadcasted_iota(jnp.int32, (tf, 1), 0)
    o_ref[1] = jnp.where(row == 0, nyq, ki)


def _dft_spec_kernel(x_ref, fc_ref, fs_ref, k_ref, o_ref, *, inv_n):
    x = x_ref[...]
    ur = _dot(fc_ref[...], x)
    kr = k_ref[0]
    ki = k_ref[1]
    tf = ur.shape[0]
    row = pl.program_id(1) * tf + lax.broadcasted_iota(jnp.int32, (tf, 1), 0)
    first = row == 0
    ui = jnp.where(first, _alternating_sum(x), _dot(fs_ref[...], x))
    uiki = ui * ki
    yr = ur * kr - jnp.where(first, 0.0, uiki)
    yi = jnp.where(first, uiki, ur * ki + ui * kr)
    scale = jnp.where(first, inv_n, 2.0 * inv_n)
    o_ref[0] = (yr * scale).astype(BF16)
    o_ref[1] = (yi * scale).astype(BF16)


def _filter_spectrum(p, fc, fs, ch):
    length, cols = p.shape
    tf = min(ROW_TILE, length)
    return pl.pallas_call(
        _filter_dft_kernel,
        grid=(HY_ORDER, length // tf),
        in_specs=[pl.BlockSpec((length, ch), lambda o, i: (0, 2 * o)),
                  pl.BlockSpec((length, ch), lambda o, i: (0, 2 * o + 1)),
                  pl.BlockSpec((tf, length), lambda o, i: (i, 0)),
                  pl.BlockSpec((tf, length), lambda o, i: (i, 0))],
        out_specs=pl.BlockSpec((None, 2, tf, ch), lambda o, i: (o, 0, i, 0)),
        out_shape=jax.ShapeDtypeStruct((HY_ORDER, 2, length, ch), F32),
        compiler_params=_cparams("parallel", "arbitrary"),
        name="hy_filter_dft",
    )(p, p, fc, fs)


def _hy_spectral(xb, col, ch, fc, fs, kspec):
    b, length, _ = xb.shape
    tf = min(ROW_TILE, length)
    return pl.pallas_call(
        functools.partial(_dft_spec_kernel, inv_n=1.0 / (2 * length)),
        grid=(b, length // tf),
        in_specs=[pl.BlockSpec((None, length, ch), lambda bi, i: (bi, 0, col)),
                  pl.BlockSpec((tf, length), lambda bi, i: (i, 0)),
                  pl.BlockSpec((tf, length), lambda bi, i: (i, 0)),
                  pl.BlockSpec((2, tf, ch), lambda bi, i: (0, i, 0))],
        out_specs=pl.BlockSpec((None, 2, tf, ch), lambda bi, i: (bi, 0, i, 0)),
        out_shape=jax.ShapeDtypeStruct((b, 2, length, ch), BF16),
        compiler_params=_cparams("parallel", "arbitrary"),
        name="hy_dft_spectral",
    )(xb, fc, fs, kspec)


def _idft_gate_kernel(y_ref, fc_ref, fs_ref, u_ref, xg_ref, d_ref, o_ref, ob_ref):
    tt = fc_ref.shape[0]
    t = pl.program_id(1) * tt + lax.broadcasted_iota(jnp.int32, (tt, 1), 0)
    sign = (1 - 2 * (t & 1)).astype(F32)
    conv = (_dot(fc_ref[...], y_ref[0]) + _dot(fs_ref[...], y_ref[1])
            + sign * y_ref[1, 0:1, :].astype(F32))
    out = xg_ref[...] * (conv + u_ref[...] * d_ref[...])
    o_ref[...] = out
    ob_ref[...] = out.astype(BF16)


def _hy_inverse_gate(y, fc, fs, u, u_col, xg, xg_col, dvec, ch):
    b, _, length, _ = y.shape
    tt = min(ROW_TILE, length)
    out = pl.BlockSpec((None, tt, ch), lambda bi, i: (bi, i, 0))
    return pl.pallas_call(
        _idft_gate_kernel,
        grid=(b, length // tt),
        in_specs=[pl.BlockSpec((None, 2, length, ch), lambda bi, i: (bi, 0, 0, 0)),
                  pl.BlockSpec((tt, length), lambda bi, i: (i, 0)),
                  pl.BlockSpec((tt, length), lambda bi, i: (i, 0)),
                  pl.BlockSpec((None, tt, ch), lambda bi, i: (bi, i, u_col)),
                  pl.BlockSpec((None, tt, ch), lambda bi, i: (bi, i, xg_col)),
                  pl.BlockSpec((1, ch), lambda bi, i: (0, 0))],
        out_specs=[out, out],
        out_shape=[jax.ShapeDtypeStruct((b, length, ch), F32), jax.ShapeDtypeStruct((b, length, ch), BF16)],
        compiler_params=_cparams("parallel", "arbitrary"),
        name="hy_idft_gate",
    )(y, fc, fs, u, xg, dvec.reshape(1, ch))


def _hyena(hyc, hyb, w1, b1, w2, b2, w3, hy_bias):
    ch = hy_bias.shape[-1]
    length = hyc.shape[1]
    fc, fs = _dft_tables(length)
    kspec = _filter_spectrum(_hy_filter_inputs(length, w1, b1, w2, b2, w3, ch), fc, fs, ch)
    y = _hy_spectral(hyb, 0, ch, fc, fs, kspec[0])
    y1, y1b = _hy_inverse_gate(y, fc, fs, hyc, 0, hyc, 1, hy_bias[0], ch)
    y = _hy_spectral(y1b, 0, ch, fc, fs, kspec[1])
    y2, _ = _hy_inverse_gate(y, fc, fs, y1, 0, hyc, 2, hy_bias[1], ch)
    return y2


def _ab_out_kernel(o_ref, y_ref, x_ref, g1_ref, woa_ref, woh_ref, out_ref):
    y = _dot(o_ref[...], woa_ref[...]) + _dot(y_ref[...].astype(BF16), woh_ref[...])
    out_ref[...] = x_ref[...] + g1_ref[...] * y


def _ab_out(o, y2, x, mods, row_fn, w):
    b, n, d = x.shape
    tm = min(ROW_TILE, n)
    tok = lambda width: pl.BlockSpec((None, tm, width), lambda bi, i: (bi, i, 0))
    return pl.pallas_call(
        _ab_out_kernel,
        grid=(b, n // tm),
        in_specs=[tok(o.shape[-1]), tok(y2.shape[-1]), tok(d), _mod_spec(row_fn, 2, d),
                  _resident(w["wo_a"].shape), _resident(w["wo_h"].shape)],
        out_specs=tok(d),
        out_shape=jax.ShapeDtypeStruct((b, n, d), F32),
        compiler_params=_cparams("parallel", "parallel"),
        name="ab_out",
    )(o, y2, x, mods, w["wo_a"], w["wo_h"])


FFN_ROWS = 512
FFN_CHUNK_MAX = 2816


def _ffn_kernel(*refs, hidden, hc, final):
    if final:
        (x_ref, xp_ref, xn_ref, sh_ref, sc_ref, g2_ref, ng_ref, wup_ref, cw_ref, cb_ref, wdn_ref,
         fg_ref, o_ref) = refs
    else:
        (x_ref, xp_ref, xn_ref, sh_ref, sc_ref, g2_ref, ng_ref, wup_ref, cw_ref, cb_ref, wdn_ref,
         o_ref) = refs
    x = x_ref[...]
    tm, d = x.shape
    he = _halo_rows(x_ref, xp_ref, xn_ref, lambda a: _modulate(a, ng_ref[...], sh_ref[...], sc_ref[...]))

    def conv(a, col):
        return _conv3_rows(a, cw_ref[:, col:col + hc], cb_ref[:, col:col + hc], tm)

    acc = jnp.zeros((tm, d), F32)
    for j in range(hidden // hc):
        gate = conv(_dot(he, wup_ref[:, j * hc:(j + 1) * hc]), j * hc)
        val = conv(_dot(he, wup_ref[:, hidden + j * hc:hidden + (j + 1) * hc]), hidden + j * hc)
        mid = (_silu(gate) * val).astype(BF16)
        acc = acc + _dot(mid, wdn_ref[j * hc:(j + 1) * hc, :])
    out = x + g2_ref[...] * acc
    if final:
        out = _rms(out, fg_ref[...])
    o_ref[...] = out


def _ffn_chunk(hidden):
    for hc in range(min(FFN_CHUNK_MAX, hidden) // HEAD_LANES * HEAD_LANES, 0, -HEAD_LANES):
        if hidden % hc == 0:
            return hc
    raise ValueError(f"FFN hidden width {hidden} is not a multiple of {HEAD_LANES}")


def _conv_ffn(x, mods, row_fn, norm_g, w_up, conv_w, conv_b, w_down, final_g=None):
    b, n, d = x.shape
    hidden = w_down.shape[0]
    tm = min(FFN_ROWS, n)
    final = final_g is not None
    tok = pl.BlockSpec((None, tm, d), lambda bi, i: (bi, i, 0))
    in_specs = _halo_specs(tm, n, d) + [
        _mod_spec(row_fn, 3, d), _mod_spec(row_fn, 4, d), _mod_spec(row_fn, 5, d),
        _resident((1, d)), _resident(w_up.shape), _resident(conv_w.shape),
        _resident((1, 2 * hidden)), _resident(w_down.shape)]
    args = [x, x, x, mods, mods, mods, norm_g.reshape(1, d), w_up, conv_w, conv_b.reshape(1, 2 * hidden), w_down]
    if final:
        in_specs.append(_resident((1, d)))
        args.append(final_g.reshape(1, d))
    return pl.pallas_call(
        functools.partial(_ffn_kernel, hidden=hidden, hc=_ffn_chunk(hidden), final=final),
        grid=(b, n // tm),
        in_specs=in_specs,
        out_specs=tok,
        out_shape=jax.ShapeDtypeStruct((b, n, d), F32),
        compiler_params=_cparams("parallel", "parallel"),
        name="conv_ffn",
    )(*args)


def _hg_proj_kernel(x_ref, sh_ref, sc_ref, g_ref, w_ref, o_ref, *, width):
    h = _modulate(x_ref[...], g_ref[...], sh_ref[...], sc_ref[...]).astype(BF16)
    z = _dot(h, w_ref[...])
    o_ref[:, :width] = _silu(z[:, :width])
    o_ref[:, width:] = z[:, width:]


def _hg_project(x, mods, row_fn, norm_g, w_in):
    b, n, d = x.shape
    cols = w_in.shape[1]
    tm = min(ROW_TILE, n)
    tok = lambda width: pl.BlockSpec((None, tm, width), lambda bi, i: (bi, i, 0))
    return pl.pallas_call(
        functools.partial(_hg_proj_kernel, width=cols // 5),
        grid=(b, n // tm),
        in_specs=[tok(d), _mod_spec(row_fn, 0, d), _mod_spec(row_fn, 1, d), _resident((1, d)),
                  _resident(w_in.shape)],
        out_specs=tok(cols),
        out_shape=jax.ShapeDtypeStruct((b, n, cols), F32),
        compiler_params=_cparams("parallel", "parallel"),
        name="hg_project",
    )(x, mods, mods, norm_g.reshape(1, d), w_in)


def _scan_tables(direction):
    c = HG_CHUNK
    idx = np.arange(c)
    level = np.full((c, c), -1, np.int32)
    for li, m in enumerate(HG_LEVELS):
        same = (idx[:, None] // (2 * m)) == (idx[None, :] // (2 * m))
        up = (idx % (2 * m)) >= m
        pair = same & (up[:, None] & ~up[None, :] if direction == 0 else ~up[:, None] & up[None, :])
        level[pair] = li
    level[idx, idx] = len(HG_LEVELS)
    if direction == 0:
        to_q = (idx[None, :] <= idx[:, None])
    else:
        to_q = (idx[None, :] >= idx[:, None])
    em = np.concatenate([to_q.astype(np.float32), np.ones((8, c), np.float32)], axis=0)
    return jnp.asarray(np.concatenate([em, em], axis=1), BF16), jnp.asarray(level)


def _level_exponent(run, m, rev):
    c, w = run.shape
    off = m if rev else m - 1
    if 2 * m >= 8:
        parts = [jnp.broadcast_to(run[a + off:a + off + 1, :], (2 * m, w)) for a in range(0, c, 2 * m)]
    elif m == 2:
        low = lax.broadcasted_iota(jnp.int32, (8, w), 0) < 4
        parts = [jnp.where(low, jnp.broadcast_to(run[a + off:a + off + 1, :], (8, w)),
                           jnp.broadcast_to(run[a + 4 + off:a + 5 + off, :], (8, w))) for a in range(0, c, 8)]
    else:
        odd = (lax.broadcasted_iota(jnp.int32, (c, w), 0) & 1) == 1
        if rev:
            parts = [jnp.where(odd, run, pltpu.roll(run, c - 1, axis=0))]
        else:
            parts = [jnp.where(odd, pltpu.roll(run, 1, axis=0), run)]
    ref = parts[0] if len(parts) == 1 else jnp.concatenate(parts, axis=0)
    bits = lax.bitcast_convert_type(run - ref, jnp.uint32) | jnp.uint32(0x80000000)
    return lax.bitcast_convert_type(bits, F32)


def _gla_group(chunks, lb, em2, lv, st, rev):
    c = HG_CHUNK
    e_ = HG_EXPAND
    nl = len(HG_LEVELS)
    log_lb = jnp.log(lb)
    log_1m = jnp.log1p(-lb)
    log_f = []
    keys = []
    for _, fr, _ in chunks:
        t = jnp.exp(-jnp.abs(fr))
        d = 1.0 + t
        keys.append((1.0 - lb) * (jnp.where(fr > 0, t, 1.0) / d))
        b = log_1m + (jnp.minimum(fr, 0.0) - jnp.log(d))
        log_f.append(jnp.maximum(log_lb, b) + jnp.log(1.0 + jnp.exp(-jnp.abs(log_lb - b))))
    g = jnp.concatenate(log_f, axis=1) * math.log2(math.e)
    hi = g.astype(BF16)
    lo = (g - hi.astype(F32)).astype(BF16)
    e_all = _dot(em2, jnp.concatenate([hi, lo], axis=0))
    outs = []
    for gi, (q, _, val) in enumerate(chunks):
        to_q = e_all[:c, gi * e_:(gi + 1) * e_]
        total = e_all[c:c + 1, gi * e_:(gi + 1) * e_]
        key = keys[gi]
        att = jnp.zeros((c, c), F32)
        for li, m in enumerate(HG_LEVELS):
            dec = jnp.exp2(_level_exponent(to_q, m, rev))
            att = jnp.where(lv == li, _dot((q * dec).astype(BF16), (key * dec).astype(BF16), _NT), att)
        att = jnp.where(lv == nl, jnp.sum(q * key, axis=1, keepdims=True), att)
        vb = val.astype(BF16)
        q_in = (q * jnp.exp2(to_q)).astype(BF16)
        outs.append(_dot(att.astype(BF16), vb) + _dot(q_in, st.astype(BF16), _NT))
        k_out = (key * jnp.exp2(total - to_q)).astype(BF16)
        st = st * jnp.exp2(total) + _dot(vb, k_out, _TN)
    return outs, st


def _hg_scan_kernel(*refs, layer, need_ctx):
    (lg_ref, emf_ref, emb_ref, lvf_ref, lvb_ref,
     qc_ref, ffc_ref, fbc_ref, ic_ref, q_ref, ff_ref, fb_ref, i_ref) = refs[:13]
    if need_ctx:
        o_ref, oc_ref, st_ref = refs[13:]
    else:
        o_ref, st_ref = refs[13:]
        oc_ref = None
    c = HG_CHUNK
    lg = lg_ref[...]
    ex = jnp.exp(lg - jnp.max(lg, axis=0, keepdims=True))
    probs = ex / jnp.sum(ex, axis=0, keepdims=True)
    acc = probs[0]
    for l in range(1, layer + 1):
        acc = acc + probs[l]
    lb_all = acc - probs[0]
    lb_f = lb_all[0:1]
    lb_b = lb_all[1:2]

    st_ref[...] = jnp.zeros_like(st_ref)
    o_ref[...] = jnp.zeros_like(o_ref)
    if need_ctx:
        oc_ref[...] = jnp.zeros_like(oc_ref)

    def run(qr, ffr, fbr, ir, outr):
        nc = qr.shape[0] // c
        grp = math.gcd(nc, HG_GROUP)
        ng = nc // grp

        def body(gi, carry):
            rows_f = [pl.multiple_of((gi * grp + k) * c, c) for k in range(grp)]
            rows_b = [pl.multiple_of((nc - 1 - gi * grp - k) * c, c) for k in range(grp)]
            load = lambda gate_ref, r: (qr[pl.ds(r, c), :], gate_ref[pl.ds(r, c), :], ir[pl.ds(r, c), :])
            of, stf = _gla_group([load(ffr, r) for r in rows_f], lb_f, emf_ref[...], lvf_ref[...], st_ref[0], False)
            st_ref[0] = stf
            ob, stb = _gla_group([load(fbr, r) for r in rows_b], lb_b, emb_ref[...], lvb_ref[...], st_ref[1], True)
            st_ref[1] = stb
            if outr is not None:
                for r, o in zip(rows_f + rows_b, of + ob):
                    outr[pl.ds(r, c), :] += o
            return carry

        lax.fori_loop(0, ng, body, 0, unroll=min(HG_UNROLL, ng))

    run(qc_ref, ffc_ref, fbc_ref, ic_ref, oc_ref)
    run(q_ref, ff_ref, fb_ref, i_ref, o_ref)


def _hg_scan(z, zc, lb_logits, layer, need_ctx):
    b, n, cols = z.shape
    nc_ = zc.shape[1]
    width = cols // 5
    heads = width // HG_EXPAND
    e = HG_EXPAND
    depth = lb_logits.shape[0]
    emf, lvf = _scan_tables(0)
    emb, lvb = _scan_tables(1)
    col = lambda rows, group: pl.BlockSpec((None, rows, e), lambda bi, h: (bi, 0, group * heads + h))
    in_specs = [pl.BlockSpec((depth, 2, e), lambda bi, h: (0, 0, h)),
                _resident(emf.shape), _resident(emb.shape), _resident(lvf.shape), _resident(lvb.shape),
                col(nc_, 0), col(nc_, 1), col(nc_, 2), col(nc_, 3),
                col(n, 0), col(n, 1), col(n, 2), col(n, 3)]
    out_specs = [pl.BlockSpec((None, n, e), lambda bi, h: (bi, 0, h))]
    out_shape = [jax.ShapeDtypeStruct((b, n, width), F32)]
    if need_ctx:
        out_specs.append(pl.BlockSpec((None, nc_, e), lambda bi, h: (bi, 0, h)))
        out_shape.append(jax.ShapeDtypeStruct((b, nc_, width), F32))
    res = pl.pallas_call(
        functools.partial(_hg_scan_kernel, layer=layer, need_ctx=need_ctx),
        grid=(b, heads),
        in_specs=in_specs,
        out_specs=out_specs,
        out_shape=out_shape,
        scratch_shapes=[pltpu.VMEM((2, e, e), F32)],
        compiler_params=_cparams("parallel", "parallel"),
        name="hg_scan",
    )(lb_logits, emf, emb, lvf, lvb, zc, zc, zc, zc, z, z, z, z)
    return (res[0], res[1]) if need_ctx else (res[0], None)


def _hg_out_kernel(o_ref, g_ref, x_ref, g1_ref, ng_ref, w_ref, out_ref, *, heads):
    ng = ng_ref[...]
    parts = [_rms(o_ref[:, h * HG_EXPAND:(h + 1) * HG_EXPAND], ng) for h in range(heads)]
    o = jnp.concatenate(parts, axis=1) * _silu(g_ref[...])
    out_ref[...] = x_ref[...] + g1_ref[...] * _dot(o.astype(BF16), w_ref[...])


def _hg_out(o, z, x, mods, row_fn, norm_g, w_out):
    b, n, d = x.shape
    width = o.shape[-1]
    tm = min(ROW_TILE, n)
    tok = lambda wd: pl.BlockSpec((None, tm, wd), lambda bi, i: (bi, i, 0))
    return pl.pallas_call(
        functools.partial(_hg_out_kernel, heads=width // HG_EXPAND),
        grid=(b, n // tm),
        in_specs=[tok(width), pl.BlockSpec((None, tm, width), lambda bi, i: (bi, i, 4)), tok(d),
                  _mod_spec(row_fn, 2, d), _resident((1, HG_EXPAND)), _resident(w_out.shape)],
        out_specs=tok(d),
        out_shape=jax.ShapeDtypeStruct((b, n, d), F32),
        compiler_params=_cparams("parallel", "parallel"),
        name="hg_out",
    )(o, z, x, mods, norm_g.reshape(1, HG_EXPAND), w_out)


def kernel(x, c, ctx, c_ctx, mod_w, mod_b, norm1_g, norm2_g, ffn_w_up, ffn_conv_w, ffn_conv_b, ffn_w_down, ab_w_in, mla_q_norm_g, mla_w_q_b, mla_kv_norm_g, mla_w_kv_b, hy_conv_w, hy_conv_b, hy_w1, hy_b1, hy_w2, hy_b2, hy_w3, hy_bias, ab_w_out, hg_w_in, hg_lb_logits, hg_norm_g, hg_w_out, final_norm_g):
    batch, n, d = x.shape
    depth = mod_w.shape[0]
    assert batch < MOD_ROWS
    ctx_row = batch
    cc = jnp.zeros((MOD_ROWS, d), F32).at[:batch].set(c).at[ctx_row].set(c_ctx)
    mods_all = _mod_vectors(cc, mod_w, mod_b)
    lat_row = lambda bi, *_: bi
    ctx_row_fn = lambda bi, *_: ctx_row
    rope_tabs = _rope_tables(n)

    xc = ctx
    for layer in range(depth):
        need_ctx = layer < depth - 1
        j = layer // 2
        mods = mods_all[layer]
        if layer % 2 == 0:
            q_lora = mla_q_norm_g.shape[-1]
            kv_lora = mla_kv_norm_g.shape[-1]
            hy_ch = hy_bias.shape[-1]
            w = _ab_weights(ab_w_in[j], mla_w_q_b[j], mla_w_kv_b[j], ab_w_out[j], q_lora, kv_lora, hy_ch)
            hy_args = (hy_w1[j], hy_b1[j], hy_w2[j], hy_b2[j], hy_w3[j], hy_bias[j])
            q_c, k_c, v_c, hy_c, hyb_c = _ab_project(xc, mods, ctx_row_fn, norm1_g[layer], w, mla_q_norm_g[j],
                                                     mla_kv_norm_g[j], None, hy_conv_w[j], hy_conv_b[j], hy_ch)
            q, k, v, hy, hyb = _ab_project(x, mods, lat_row, norm1_g[layer], w, mla_q_norm_g[j],
                                           mla_kv_norm_g[j], rope_tabs, hy_conv_w[j], hy_conv_b[j], hy_ch)
            o = _attention(q, [jnp.concatenate([k_c, k], axis=1)], [jnp.concatenate([v_c, v], axis=1)], w["heads"])
            x_new = _ab_out(o, _hyena(hy, hyb, *hy_args), x, mods, lat_row, w)
            if need_ctx:
                o_c = _attention(q_c, [k_c], [v_c], w["heads"])
                xc = _ab_out(o_c, _hyena(hy_c, hyb_c, *hy_args), xc, mods, ctx_row_fn, w)
            x = x_new
        else:
            w_in = hg_w_in[j].astype(BF16)
            z_c = _hg_project(xc, mods, ctx_row_fn, norm1_g[layer], w_in)
            z = _hg_project(x, mods, lat_row, norm1_g[layer], w_in)
            o, o_c = _hg_scan(z, z_c, hg_lb_logits, layer, need_ctx)
            w_out = hg_w_out[j].astype(BF16)
            x = _hg_out(o, z, x, mods, lat_row, hg_norm_g[j], w_out)
            if need_ctx:
                xc = _hg_out(o_c, z_c, xc, mods, ctx_row_fn, hg_norm_g[j], w_out)
        w_up = ffn_w_up[layer].astype(BF16)
        w_dn = ffn_w_down[layer].astype(BF16)
        last = layer == depth - 1
        x = _conv_ffn(x, mods, lat_row, norm2_g[layer], w_up, ffn_conv_w[layer], ffn_conv_b[layer], w_dn,
                      final_norm_g if last else None)
        if need_ctx:
            xc = _conv_ffn(xc, mods, ctx_row_fn, norm2_g[layer], w_up, ffn_conv_w[layer], ffn_conv_b[layer], w_dn)
    return x
```

```python
import functools
import math

import numpy as np
import jax
import jax.numpy as jnp
from jax import lax
from jax.experimental import pallas as pl
from jax.experimental.pallas import tpu as pltpu

F32 = jnp.float32
BF16 = jnp.bfloat16
HIGHEST = lax.Precision.HIGHEST

EPS = 1e-6
GRID_W = 64

MLA_NOPE = 64
MLA_ROPE = 32
MLA_V = 64
ROPE_NF = MLA_ROPE // 4
ROPE_BASE = 10000.0
HEAD_LANES = 128
ATTN_TQ = 4096
ATTN_SUB = 256

HY_ORDER = 2
HY_BANDS = 16
HY_SIN_FREQ = 1.0
HY_DECAY_TARGET = 1e-2
HY_FAST_DECAY = 0.3
HY_SLOW_DECAY = 1.5
HY_MIN_DECAY = math.log(HY_DECAY_TARGET) / HY_SLOW_DECAY
HY_MAX_DECAY = math.log(HY_DECAY_TARGET) / HY_FAST_DECAY

HG_EXPAND = 128
HG_CHUNK = 64
HG_LEVELS = (32, 16, 8, 4, 2, 1)
HG_GROUP = 4
HG_UNROLL = 4

ROW_TILE = 512
MOD_ROWS = 16
HALO = 8
VMEM_LIMIT = 56 * 1024 * 1024

_NT = (((1,), (1,)), ((), ()))
_TN = (((0,), (0,)), ((), ()))


def _cparams(*sem):
    return pltpu.CompilerParams(dimension_semantics=sem, vmem_limit_bytes=VMEM_LIMIT)


def _dot(a, b, dims=(((1,), (0,)), ((), ()))):
    return lax.dot_general(a, b, dims, precision=lax.Precision.DEFAULT, preferred_element_type=F32)


def _rms(x, g):
    return x * lax.rsqrt(jnp.mean(x * x, axis=-1, keepdims=True) + EPS) * g


def _modulate(x, g, shift, scale):
    return _rms(x, g) * (1.0 + scale) + shift


def _silu(x):
    return x * jax.nn.sigmoid(x)


def _mod_spec(row_fn, k, d):
    return pl.BlockSpec((None, None, 1, d), lambda *idx: (row_fn(*idx), k, 0, 0))


def _resident(shape):
    nd = len(shape)
    return pl.BlockSpec(shape, lambda *idx: (0,) * nd, pipeline_mode=pl.Buffered(1))


def _mod_kernel(c_ref, w_ref, b_ref, o_ref):
    a = _silu(c_ref[...])
    o_ref[...] = jnp.dot(a, w_ref[...], precision=HIGHEST, preferred_element_type=F32) + b_ref[...]


def _mod_vectors(cc, mod_w, mod_b):
    depth, d, n6 = mod_w.shape
    tn = ROW_TILE
    out = pl.pallas_call(
        _mod_kernel,
        grid=(depth, n6 // tn),
        in_specs=[pl.BlockSpec((MOD_ROWS, d), lambda l, j: (0, 0)),
                  pl.BlockSpec((None, d, tn), lambda l, j: (l, 0, j)),
                  pl.BlockSpec((None, 1, tn), lambda l, j: (l, 0, j))],
        out_specs=pl.BlockSpec((None, MOD_ROWS, tn), lambda l, j: (l, 0, j)),
        out_shape=jax.ShapeDtypeStruct((depth, MOD_ROWS, n6), F32),
        compiler_params=_cparams("parallel", "parallel"),
        name="mod_vectors",
    )(cc, mod_w, mod_b.reshape(depth, 1, n6))
    return out.reshape(depth, MOD_ROWS, 6, 1, d)


def _halo_rows(x_ref, xp_ref, xn_ref, mod):
    i = pl.program_id(1)
    last = pl.num_programs(1) - 1
    hp = mod(xp_ref[...]) * (i > 0).astype(F32)
    hn = mod(xn_ref[...]) * (i < last).astype(F32)
    return jnp.concatenate([hp, mod(x_ref[...]), hn], axis=0).astype(BF16)


def _conv3_rows(a, w, bias, tm):
    ne = tm + 2 * HALO
    prev = pltpu.roll(a, 1, axis=0)[HALO:HALO + tm]
    nxt = pltpu.roll(a, ne - 1, axis=0)[HALO:HALO + tm]
    return prev * w[0:1] + a[HALO:HALO + tm] * w[1:2] + nxt * w[2:3] + bias


def _ab_proj_kernel(*refs, rope, q_lora, kv_lora, heads, scale, ch):
    if rope:
        (x_ref, xp_ref, xn_ref, sh_ref, sc_ref, g_ref, win_ref, qg_ref, wqa_ref, wqb_ref, kvg_ref, wk_ref,
         wv_ref, cw_ref, cb_ref, cos_ref, sin_ref, q_ref, k_ref, v_ref, hy_ref, hyb_ref) = refs
    else:
        (x_ref, xp_ref, xn_ref, sh_ref, sc_ref, g_ref, win_ref, qg_ref, wqa_ref, kvg_ref, wk_ref,
         wv_ref, cw_ref, cb_ref, q_ref, k_ref, v_ref, hy_ref, hyb_ref) = refs
    tm = x_ref.shape[0]
    he = _halo_rows(x_ref, xp_ref, xn_ref, lambda a: _modulate(a, g_ref[...], sh_ref[...], sc_ref[...]))
    ze = _dot(he, win_ref[...])
    o = q_lora + kv_lora
    hy = _conv3_rows(ze[:, o + 2 * HEAD_LANES:], cw_ref[...], cb_ref[...], tm)
    hy_ref[...] = hy
    hyb_ref[...] = hy[:, :ch].astype(BF16)
    z = ze[HALO:HALO + tm]
    qn = _rms(z[:, :q_lora], qg_ref[...]).astype(BF16)
    kvn = _rms(z[:, q_lora:o], kvg_ref[...]).astype(BF16)
    v_ref[...] = _dot(kvn, wv_ref[...]).astype(BF16)
    qa = _dot(qn, wqa_ref[...])
    kk = _dot(kvn, wk_ref[...])
    kr = z[:, o:o + HEAD_LANES]
    if rope:
        cos = cos_ref[...]
        sin = sin_ref[...]
        qb = _dot(qn, wqb_ref[...])
        kr = kr * cos + z[:, o + HEAD_LANES:o + 2 * HEAD_LANES] * sin
    for hh in range(heads):
        sl = slice(hh * HEAD_LANES, (hh + 1) * HEAD_LANES)
        qh = qa[:, sl]
        if rope:
            qh = qh * cos + qb[:, sl] * sin
        q_ref[:, sl] = (qh * scale).astype(BF16)
        k_ref[:, sl] = (kk[:, sl] + kr).astype(BF16)


def _rope_swap(w):
    nf = ROPE_NF
    return jnp.concatenate([-w[:, nf:2 * nf], w[:, :nf], -w[:, 3 * nf:4 * nf], w[:, 2 * nf:3 * nf]], axis=1)


def _ab_weights(w_in, w_q_b, w_kv_b, w_out, q_lora, kv_lora, hy_ch):
    d = w_in.shape[0]
    heads = w_q_b.shape[1] // (MLA_NOPE + MLA_ROPE)
    o = q_lora + kv_lora
    w_kr = w_in[:, o:o + MLA_ROPE]
    zpad = lambda n: jnp.zeros((d, n), F32)
    tail = HEAD_LANES - MLA_NOPE - MLA_ROPE
    win = jnp.concatenate(
        [w_in[:, :o],
         zpad(MLA_NOPE), w_kr, zpad(tail),
         zpad(MLA_NOPE), _rope_swap(w_kr), zpad(tail),
         w_in[:, o + MLA_ROPE:]], axis=1).astype(BF16)
    wq = w_q_b.reshape(q_lora, heads, MLA_NOPE + MLA_ROPE)
    zq = jnp.zeros((q_lora, heads, tail), F32)
    wqa = jnp.concatenate([wq, zq], axis=2).reshape(q_lora, heads * HEAD_LANES).astype(BF16)
    wq_rope = wq[:, :, MLA_NOPE:].reshape(q_lora * heads, MLA_ROPE)
    wqb = jnp.concatenate([jnp.zeros((q_lora, heads, MLA_NOPE), F32),
                           _rope_swap(wq_rope).reshape(q_lora, heads, MLA_ROPE), zq],
                          axis=2).reshape(q_lora, heads * HEAD_LANES).astype(BF16)
    wkv = w_kv_b.reshape(kv_lora, heads, MLA_NOPE + MLA_V)
    zk = jnp.zeros((kv_lora, heads, HEAD_LANES - MLA_NOPE), F32)
    wk = jnp.concatenate([wkv[:, :, :MLA_NOPE], zk], axis=2).reshape(kv_lora, heads * HEAD_LANES).astype(BF16)
    zv = jnp.zeros((kv_lora, heads, HEAD_LANES - MLA_V), F32)
    wv = jnp.concatenate([wkv[:, :, MLA_NOPE:], zv], axis=2).reshape(kv_lora, heads * HEAD_LANES).astype(BF16)
    wo = w_out[:heads * MLA_V].reshape(heads, MLA_V, d)
    wo_a = jnp.concatenate([wo, jnp.zeros((heads, HEAD_LANES - MLA_V, d), F32)],
                           axis=1).reshape(heads * HEAD_LANES, d).astype(BF16)
    wo_h = w_out[heads * MLA_V:].astype(BF16)
    return dict(win=win, wqa=wqa, wqb=wqb, wk=wk, wv=wv, wo_a=wo_a, wo_h=wo_h, heads=heads)


def _rope_tables(n):
    rows = n // GRID_W
    row = jnp.repeat(jnp.arange(rows), GRID_W).astype(F32)
    col = jnp.tile(jnp.arange(GRID_W), rows).astype(F32)
    inv = ROPE_BASE ** (-jnp.arange(ROPE_NF, dtype=F32) / ROPE_NF)
    ar = row[:, None] * inv
    ac = col[:, None] * inv
    ones = jnp.ones((n, MLA_NOPE), F32)
    tail = HEAD_LANES - MLA_NOPE - MLA_ROPE
    cos = jnp.concatenate([ones, jnp.cos(ar), jnp.cos(ar), jnp.cos(ac), jnp.cos(ac),
                           jnp.ones((n, tail), F32)], axis=1)
    sin = jnp.concatenate([0 * ones, jnp.sin(ar), jnp.sin(ar), jnp.sin(ac), jnp.sin(ac),
                           jnp.zeros((n, tail), F32)], axis=1)
    return cos, sin


def _halo_specs(tm, n, d):
    nh = n // HALO
    tpb = tm // HALO
    return [pl.BlockSpec((None, tm, d), lambda bi, i: (bi, i, 0)),
            pl.BlockSpec((None, HALO, d), lambda bi, i: (bi, jnp.maximum(i * tpb - 1, 0), 0)),
            pl.BlockSpec((None, HALO, d), lambda bi, i: (bi, jnp.minimum((i + 1) * tpb, nh - 1), 0))]


def _ab_project(x, mods, row_fn, norm_g, w, q_norm_g, kv_norm_g, rope_tabs, conv_w, conv_b, ch):
    b, n, d = x.shape
    tm = min(ROW_TILE, n)
    heads = w["heads"]
    q_lora = q_norm_g.shape[-1]
    kv_lora = kv_norm_g.shape[-1]
    hw = heads * HEAD_LANES
    hy_cols = conv_w.shape[-1]
    rope = rope_tabs is not None
    tok = lambda width: pl.BlockSpec((None, tm, width), lambda bi, i: (bi, i, 0))
    in_specs = _halo_specs(tm, n, d) + [
        _mod_spec(row_fn, 0, d), _mod_spec(row_fn, 1, d), _resident((1, d)),
        _resident(w["win"].shape), _resident((1, q_lora)), _resident(w["wqa"].shape)]
    args = [x, x, x, mods, mods, norm_g.reshape(1, d), w["win"], q_norm_g.reshape(1, q_lora), w["wqa"]]
    if rope:
        in_specs.append(_resident(w["wqb"].shape))
        args.append(w["wqb"])
    in_specs += [_resident((1, kv_lora)), _resident(w["wk"].shape), _resident(w["wv"].shape),
                 _resident(conv_w.shape), _resident((1, hy_cols))]
    args += [kv_norm_g.reshape(1, kv_lora), w["wk"], w["wv"], conv_w, conv_b.reshape(1, hy_cols)]
    if rope:
        in_specs += [pl.BlockSpec((tm, HEAD_LANES), lambda bi, i: (i, 0))] * 2
        args += list(rope_tabs)
    kern = functools.partial(_ab_proj_kernel, rope=rope, q_lora=q_lora, kv_lora=kv_lora, heads=heads, ch=ch,
                             scale=math.log2(math.e) / math.sqrt(MLA_NOPE + MLA_ROPE))
    return pl.pallas_call(
        kern,
        grid=(b, n // tm),
        in_specs=in_specs,
        out_specs=[tok(hw), tok(hw), tok(hw), tok(hy_cols), tok(ch)],
        out_shape=[jax.ShapeDtypeStruct((b, n, hw), BF16)] * 3 + [jax.ShapeDtypeStruct((b, n, hy_cols), F32),
                                                                   jax.ShapeDtypeStruct((b, n, ch), BF16)],
        compiler_params=_cparams("parallel", "parallel"),
        name="ab_project",
    )(*args)


def _attn_kernel(*refs, sub, nseg):
    q_ref, o_ref = refs[0], refs[-1]
    ks = [r[...] for r in refs[1:1 + nseg]]
    vs = [r[...] for r in refs[1 + nseg:1 + 2 * nseg]]
    for r in range(0, q_ref.shape[0], sub):
        q = q_ref[r:r + sub, :]
        ss = [_dot(q, k, _NT) for k in ks]
        m = functools.reduce(jnp.maximum, [jnp.max(s, axis=-1, keepdims=True) for s in ss])
        ps = [jnp.exp2(s - m) for s in ss]
        l = sum(jnp.sum(p, axis=-1, keepdims=True) for p in ps)
        o = sum(_dot(p.astype(BF16), v) for p, v in zip(ps, vs))
        o_ref[r:r + sub, :] = (o / l).astype(o_ref.dtype)


def _attention(q, ks, vs, heads):
    b, n, hw = q.shape
    tq = min(ATTN_TQ, n)
    tile = pl.BlockSpec((None, tq, HEAD_LANES), lambda bi, h, i: (bi, i, h))
    whole = lambda a: pl.BlockSpec((None, a.shape[1], HEAD_LANES), lambda bi, h, i: (bi, 0, h))
    return pl.pallas_call(
        functools.partial(_attn_kernel, sub=min(ATTN_SUB, tq), nseg=len(ks)),
        grid=(b, heads, n // tq),
        in_specs=[tile] + [whole(a) for a in ks] + [whole(a) for a in vs],
        out_specs=tile,
        out_shape=jax.ShapeDtypeStruct((b, n, hw), BF16),
        compiler_params=_cparams("parallel", "parallel", "parallel"),
        name="attention",
    )(q, *ks, *vs)


def _hy_filter_kernel(z_ref, t_ref, w1_ref, b1_ref, w2_ref, b2_ref, w3_ref, dl_ref, o_ref, *, ch):
    hdot = lambda a, w: jnp.dot(a, w, precision=HIGHEST, preferred_element_type=F32)
    a = jnp.sin(HY_SIN_FREQ * (hdot(z_ref[...], w1_ref[...]) + b1_ref[...]))
    a = jnp.sin(HY_SIN_FREQ * (hdot(a, w2_ref[...]) + b2_ref[...]))
    f = hdot(a, w3_ref[...])
    dec = jnp.exp(-t_ref[...] * dl_ref[...])
    tl = z_ref.shape[0]
    row = pl.program_id(0) * tl + lax.broadcasted_iota(jnp.int32, (tl, ch), 0)
    for o in range(HY_ORDER):
        hf = f[:, (2 * o) * ch:(2 * o + 1) * ch] * dec
        hb = jnp.where(row > 0, f[:, (2 * o + 1) * ch:(2 * o + 2) * ch] * dec, 0.0)
        o_ref[:, (2 * o) * ch:(2 * o + 1) * ch] = (hf + hb).astype(BF16)
        o_ref[:, (2 * o + 1) * ch:(2 * o + 2) * ch] = (hf - hb).astype(BF16)


def _hy_filter_inputs(length, w1, b1, w2, b2, w3, ch):
    t = jnp.linspace(0.0, 1.0, length, dtype=F32)[:, None]
    wv = 2.0 * math.pi * jnp.arange(length, dtype=F32)[:, None] / length
    f = jnp.linspace(1e-4, HY_BANDS - 1, HY_BANDS, dtype=F32)
    z = jnp.concatenate([t, jnp.cos(f * wv), -jnp.sin(f * wv)], axis=-1)
    emb = z.shape[1]
    z = jnp.pad(z, ((0, 0), (0, HEAD_LANES - emb)))
    w1p = jnp.pad(w1, ((0, HEAD_LANES - emb), (0, 0)))
    deltas = jnp.abs(jnp.linspace(HY_MIN_DECAY, HY_MAX_DECAY, ch, dtype=F32))[None, :]
    fw = w1.shape[1]
    tl = min(ROW_TILE, length)
    return pl.pallas_call(
        functools.partial(_hy_filter_kernel, ch=ch),
        grid=(length // tl,),
        in_specs=[pl.BlockSpec((tl, HEAD_LANES), lambda i: (i, 0)), pl.BlockSpec((tl, 1), lambda i: (i, 0)),
                  _resident((HEAD_LANES, fw)), _resident((1, fw)), _resident((fw, fw)), _resident((1, fw)),
                  _resident(w3.shape), _resident((1, ch))],
        out_specs=pl.BlockSpec((tl, 2 * HY_ORDER * ch), lambda i: (i, 0)),
        out_shape=jax.ShapeDtypeStruct((length, 2 * HY_ORDER * ch), BF16),
        compiler_params=_cparams("parallel"),
        name="hy_filter",
    )(z, t, w1p, b1.reshape(1, fw), w2, b2.reshape(1, fw), w3, deltas)


def _dft_tables(length):
    n2 = 2 * length
    theta = 2.0 * math.pi / n2
    n = jnp.arange(length, dtype=jnp.int32)[None, :]
    a = 8 * jnp.arange(length // 8, dtype=jnp.int32)[:, None]
    b = jnp.arange(8, dtype=jnp.int32)[:, None]
    ang_a = ((a * n) % n2).astype(F32) * theta
    ang_b = ((b * n) % n2).astype(F32) * theta
    ca, sa = jnp.cos(ang_a)[:, None, :], jnp.sin(ang_a)[:, None, :]
    cb, sb = jnp.cos(ang_b)[None, :, :], jnp.sin(ang_b)[None, :, :]
    fc = (ca * cb - sa * sb).reshape(length, length)
    fs = -(sa * cb + ca * sb).reshape(length, length)
    return fc.astype(BF16), fs.astype(BF16)


def _alternating_sum(x):
    col = lax.broadcasted_iota(jnp.int32, (8, x.shape[0]), 1)
    alt = (1 - 2 * (col & 1)).astype(F32).astype(BF16)
    return _dot(alt, x)[0:1]


def _filter_dft_kernel(hs_ref, hd_ref, fc_ref, fs_ref, o_ref):
    hs = hs_ref[...]
    o_ref[0] = _dot(fc_ref[...], hs)
    ki = _dot(fs_ref[...], hd_ref[...])
    nyq = _alternating_sum(hs)
    tf = ki.shape[0]
    row = pl.program_id(1) * tf + lax.broadcasted_iota(jnp.int32, (tf, 1), 0)
    o_ref[1] = jnp.where(row == 0, nyq, ki)


def _dft_spec_kernel(x_ref, fc_ref, fs_ref, k_ref, o_ref, *, inv_n):
    x = x_ref[...]
    ur = _dot(fc_ref[...], x)
    kr = k_ref[0]
    ki = k_ref[1]
    tf = ur.shape[0]
    row = pl.program_id(1) * tf + lax.broadcasted_iota(jnp.int32, (tf, 1), 0)
    first = row == 0
    ui = jnp.where(first, _alternating_sum(x), _dot(fs_ref[...], x))
    uiki = ui * ki
    yr = ur * kr - jnp.where(first, 0.0, uiki)
    yi = jnp.where(first, uiki, ur * ki + ui * kr)
    scale = jnp.where(first, inv_n, 2.0 * inv_n)
    o_ref[0] = (yr * scale).astype(BF16)
    o_ref[1] = (yi * scale).astype(BF16)


def _filter_spectrum(p, fc, fs, ch):
    length, cols = p.shape
    tf = min(ROW_TILE, length)
    return pl.pallas_call(
        _filter_dft_kernel,
        grid=(HY_ORDER, length // tf),
        in_specs=[pl.BlockSpec((length, ch), lambda o, i: (0, 2 * o)),
                  pl.BlockSpec((length, ch), lambda o, i: (0, 2 * o + 1)),
                  pl.BlockSpec((tf, length), lambda o, i: (i, 0)),
                  pl.BlockSpec((tf, length), lambda o, i: (i, 0))],
        out_specs=pl.BlockSpec((None, 2, tf, ch), lambda o, i: (o, 0, i, 0)),
        out_shape=jax.ShapeDtypeStruct((HY_ORDER, 2, length, ch), F32),
        compiler_params=_cparams("parallel", "arbitrary"),
        name="hy_filter_dft",
    )(p, p, fc, fs)


def _hy_spectral(xb, col, ch, fc, fs, kspec):
    b, length, _ = xb.shape
    tf = min(ROW_TILE, length)
    return pl.pallas_call(
        functools.partial(_dft_spec_kernel, inv_n=1.0 / (2 * length)),
        grid=(b, length // tf),
        in_specs=[pl.BlockSpec((None, length, ch), lambda bi, i: (bi, 0, col)),
                  pl.BlockSpec((tf, length), lambda bi, i: (i, 0)),
                  pl.BlockSpec((tf, length), lambda bi, i: (i, 0)),
                  pl.BlockSpec((2, tf, ch), lambda bi, i: (0, i, 0))],
        out_specs=pl.BlockSpec((None, 2, tf, ch), lambda bi, i: (bi, 0, i, 0)),
        out_shape=jax.ShapeDtypeStruct((b, 2, length, ch), BF16),
        compiler_params=_cparams("parallel", "arbitrary"),
        name="hy_dft_spectral",
    )(xb, fc, fs, kspec)


def _idft_gate_kernel(y_ref, fc_ref, fs_ref, u_ref, xg_ref, d_ref, o_ref, ob_ref):
    tt = fc_ref.shape[0]
    t = pl.program_id(1) * tt + lax.broadcasted_iota(jnp.int32, (tt, 1), 0)
    sign = (1 - 2 * (t & 1)).astype(F32)
    conv = (_dot(fc_ref[...], y_ref[0]) + _dot(fs_ref[...], y_ref[1])
            + sign * y_ref[1, 0:1, :].astype(F32))
    out = xg_ref[...] * (conv + u_ref[...] * d_ref[...])
    o_ref[...] = out
    ob_ref[...] = out.astype(BF16)


def _hy_inverse_gate(y, fc, fs, u, u_col, xg, xg_col, dvec, ch):
    b, _, length, _ = y.shape
    tt = min(ROW_TILE, length)
    out = pl.BlockSpec((None, tt, ch), lambda bi, i: (bi, i, 0))
    return pl.pallas_call(
        _idft_gate_kernel,
        grid=(b, length // tt),
        in_specs=[pl.BlockSpec((None, 2, length, ch), lambda bi, i: (bi, 0, 0, 0)),
                  pl.BlockSpec((tt, length), lambda bi, i: (i, 0)),
                  pl.BlockSpec((tt, length), lambda bi, i: (i, 0)),
                  pl.BlockSpec((None, tt, ch), lambda bi, i: (bi, i, u_col)),
                  pl.BlockSpec((None, tt, ch), lambda bi, i: (bi, i, xg_col)),
                  pl.BlockSpec((1, ch), lambda bi, i: (0, 0))],
        out_specs=[out, out],
        out_shape=[jax.ShapeDtypeStruct((b, length, ch), F32), jax.ShapeDtypeStruct((b, length, ch), BF16)],
        compiler_params=_cparams("parallel", "arbitrary"),
        name="hy_idft_gate",
    )(y, fc, fs, u, xg, dvec.reshape(1, ch))


def _hyena(hyc, hyb, w1, b1, w2, b2, w3, hy_bias):
    ch = hy_bias.shape[-1]
    length = hyc.shape[1]
    fc, fs = _dft_tables(length)
    kspec = _filter_spectrum(_hy_filter_inputs(length, w1, b1, w2, b2, w3, ch), fc, fs, ch)
    y = _hy_spectral(hyb, 0, ch, fc, fs, kspec[0])
    y1, y1b = _hy_inverse_gate(y, fc, fs, hyc, 0, hyc, 1, hy_bias[0], ch)
    y = _hy_spectral(y1b, 0, ch, fc, fs, kspec[1])
    y2, _ = _hy_inverse_gate(y, fc, fs, y1, 0, hyc, 2, hy_bias[1], ch)
    return y2


def _ab_out_kernel(o_ref, y_ref, x_ref, g1_ref, woa_ref, woh_ref, out_ref):
    y = _dot(o_ref[...], woa_ref[...]) + _dot(y_ref[...].astype(BF16), woh_ref[...])
    out_ref[...] = x_ref[...] + g1_ref[...] * y


def _ab_out(o, y2, x, mods, row_fn, w):
    b, n, d = x.shape
    tm = min(ROW_TILE, n)
    tok = lambda width: pl.BlockSpec((None, tm, width), lambda bi, i: (bi, i, 0))
    return pl.pallas_call(
        _ab_out_kernel,
        grid=(b, n // tm),
        in_specs=[tok(o.shape[-1]), tok(y2.shape[-1]), tok(d), _mod_spec(row_fn, 2, d),
                  _resident(w["wo_a"].shape), _resident(w["wo_h"].shape)],
        out_specs=tok(d),
        out_shape=jax.ShapeDtypeStruct((b, n, d), F32),
        compiler_params=_cparams("parallel", "parallel"),
        name="ab_out",
    )(o, y2, x, mods, w["wo_a"], w["wo_h"])


FFN_ROWS = 512
FFN_CHUNK_MAX = 2816


def _ffn_kernel(*refs, hidden, hc, final):
    if final:
        (x_ref, xp_ref, xn_ref, sh_ref, sc_ref, g2_ref, ng_ref, wup_ref, cw_ref, cb_ref, wdn_ref,
         fg_ref, o_ref) = refs
    else:
        (x_ref, xp_ref, xn_ref, sh_ref, sc_ref, g2_ref, ng_ref, wup_ref, cw_ref, cb_ref, wdn_ref,
         o_ref) = refs
    x = x_ref[...]
    tm, d = x.shape
    he = _halo_rows(x_ref, xp_ref, xn_ref, lambda a: _modulate(a, ng_ref[...], sh_ref[...], sc_ref[...]))

    def conv(a, col):
        return _conv3_rows(a, cw_ref[:, col:col + hc], cb_ref[:, col:col + hc], tm)

    acc = jnp.zeros((tm, d), F32)
    for j in range(hidden // hc):
        gate = conv(_dot(he, wup_ref[:, j * hc:(j + 1) * hc]), j * hc)
        val = conv(_dot(he, wup_ref[:, hidden + j * hc:hidden + (j + 1) * hc]), hidden + j * hc)
        mid = (_silu(gate) * val).astype(BF16)
        acc = acc + _dot(mid, wdn_ref[j * hc:(j + 1) * hc, :])
    out = x + g2_ref[...] * acc
    if final:
        out = _rms(out, fg_ref[...])
    o_ref[...] = out


def _ffn_chunk(hidden):
    for hc in range(min(FFN_CHUNK_MAX, hidden) // HEAD_LANES * HEAD_LANES, 0, -HEAD_LANES):
        if hidden % hc == 0:
            return hc
    raise ValueError(f"FFN hidden width {hidden} is not a multiple of {HEAD_LANES}")


def _conv_ffn(x, mods, row_fn, norm_g, w_up, conv_w, conv_b, w_down, final_g=None):
    b, n, d = x.shape
    hidden = w_down.shape[0]
    tm = min(FFN_ROWS, n)
    final = final_g is not None
    tok = pl.BlockSpec((None, tm, d), lambda bi, i: (bi, i, 0))
    in_specs = _halo_specs(tm, n, d) + [
        _mod_spec(row_fn, 3, d), _mod_spec(row_fn, 4, d), _mod_spec(row_fn, 5, d),
        _resident((1, d)), _resident(w_up.shape), _resident(conv_w.shape),
        _resident((1, 2 * hidden)), _resident(w_down.shape)]
    args = [x, x, x, mods, mods, mods, norm_g.reshape(1, d), w_up, conv_w, conv_b.reshape(1, 2 * hidden), w_down]
    if final:
        in_specs.append(_resident((1, d)))
        args.append(final_g.reshape(1, d))
    return pl.pallas_call(
        functools.partial(_ffn_kernel, hidden=hidden, hc=_ffn_chunk(hidden), final=final),
        grid=(b, n // tm),
        in_specs=in_specs,
        out_specs=tok,
        out_shape=jax.ShapeDtypeStruct((b, n, d), F32),
        compiler_params=_cparams("parallel", "parallel"),
        name="conv_ffn",
    )(*args)


def _hg_proj_kernel(x_ref, sh_ref, sc_ref, g_ref, w_ref, o_ref, *, width):
    h = _modulate(x_ref[...], g_ref[...], sh_ref[...], sc_ref[...]).astype(BF16)
    z = _dot(h, w_ref[...])
    o_ref[:, :width] = _silu(z[:, :width])
    o_ref[:, width:] = z[:, width:]


def _hg_project(x, mods, row_fn, norm_g, w_in):
    b, n, d = x.shape
    cols = w_in.shape[1]
    tm = min(ROW_TILE, n)
    tok = lambda width: pl.BlockSpec((None, tm, width), lambda bi, i: (bi, i, 0))
    return pl.pallas_call(
        functools.partial(_hg_proj_kernel, width=cols // 5),
        grid=(b, n // tm),
        in_specs=[tok(d), _mod_spec(row_fn, 0, d), _mod_spec(row_fn, 1, d), _resident((1, d)),
                  _resident(w_in.shape)],
        out_specs=tok(cols),
        out_shape=jax.ShapeDtypeStruct((b, n, cols), F32),
        compiler_params=_cparams("parallel", "parallel"),
        name="hg_project",
    )(x, mods, mods, norm_g.reshape(1, d), w_in)


def _scan_tables(direction):
    c = HG_CHUNK
    idx = np.arange(c)
    level = np.full((c, c), -1, np.int32)
    for li, m in enumerate(HG_LEVELS):
        same = (idx[:, None] // (2 * m)) == (idx[None, :] // (2 * m))
        up = (idx % (2 * m)) >= m
        pair = same & (up[:, None] & ~up[None, :] if direction == 0 else ~up[:, None] & up[None, :])
        level[pair] = li
    level[idx, idx] = len(HG_LEVELS)
    if direction == 0:
        to_q = (idx[None, :] <= idx[:, None])
    else:
        to_q = (idx[None, :] >= idx[:, None])
    em = np.concatenate([to_q.astype(np.float32), np.ones((8, c), np.float32)], axis=0)
    return jnp.asarray(np.concatenate([em, em], axis=1), BF16), jnp.asarray(level)


def _level_exponent(run, m, rev):
    c, w = run.shape
    off = m if rev else m - 1
    if 2 * m >= 8:
        parts = [jnp.broadcast_to(run[a + off:a + off + 1, :], (2 * m, w)) for a in range(0, c, 2 * m)]
    elif m == 2:
        low = lax.broadcasted_iota(jnp.int32, (8, w), 0) < 4
        parts = [jnp.where(low, jnp.broadcast_to(run[a + off:a + off + 1, :], (8, w)),
                           jnp.broadcast_to(run[a + 4 + off:a + 5 + off, :], (8, w))) for a in range(0, c, 8)]
    else:
        odd = (lax.broadcasted_iota(jnp.int32, (c, w), 0) & 1) == 1
        if rev:
            parts = [jnp.where(odd, run, pltpu.roll(run, c - 1, axis=0))]
        else:
            parts = [jnp.where(odd, pltpu.roll(run, 1, axis=0), run)]
    ref = parts[0] if len(parts) == 1 else jnp.concatenate(parts, axis=0)
    bits = lax.bitcast_convert_type(run - ref, jnp.uint32) | jnp.uint32(0x80000000)
    return lax.bitcast_convert_type(bits, F32)


def _gla_group(chunks, lb, em2, lv, st, rev):
    c = HG_CHUNK
    e_ = HG_EXPAND
    nl = len(HG_LEVELS)
    log_lb = jnp.log(lb)
    log_1m = jnp.log1p(-lb)
    log_f = []
    keys = []
    for _, fr, _ in chunks:
        t = jnp.exp(-jnp.abs(fr))
        d = 1.0 + t
        keys.append((1.0 - lb) * (jnp.where(fr > 0, t, 1.0) / d))
        b = log_1m + (jnp.minimum(fr, 0.0) - jnp.log(d))
        log_f.append(jnp.maximum(log_lb, b) + jnp.log(1.0 + jnp.exp(-jnp.abs(log_lb - b))))
    g = jnp.concatenate(log_f, axis=1) * math.log2(math.e)
    hi = g.astype(BF16)
    lo = (g - hi.astype(F32)).astype(BF16)
    e_all = _dot(em2, jnp.concatenate([hi, lo], axis=0))
    outs = []
    for gi, (q, _, val) in enumerate(chunks):
        to_q = e_all[:c, gi * e_:(gi + 1) * e_]
        total = e_all[c:c + 1, gi * e_:(gi + 1) * e_]
        key = keys[gi]
        att = jnp.zeros((c, c), F32)
        for li, m in enumerate(HG_LEVELS):
            dec = jnp.exp2(_level_exponent(to_q, m, rev))
            att = jnp.where(lv == li, _dot((q * dec).astype(BF16), (key * dec).astype(BF16), _NT), att)
        att = jnp.where(lv == nl, jnp.sum(q * key, axis=1, keepdims=True), att)
        vb = val.astype(BF16)
        q_in = (q * jnp.exp2(to_q)).astype(BF16)
        outs.append(_dot(att.astype(BF16), vb) + _dot(q_in, st.astype(BF16), _NT))
        k_out = (key * jnp.exp2(total - to_q)).astype(BF16)
        st = st * jnp.exp2(total) + _dot(vb, k_out, _TN)
    return outs, st


def _hg_scan_kernel(*refs, layer, need_ctx):
    (lg_ref, emf_ref, emb_ref, lvf_ref, lvb_ref,
     qc_ref, ffc_ref, fbc_ref, ic_ref, q_ref, ff_ref, fb_ref, i_ref) = refs[:13]
    if need_ctx:
        o_ref, oc_ref, st_ref = refs[13:]
    else:
        o_ref, st_ref = refs[13:]
        oc_ref = None
    c = HG_CHUNK
    lg = lg_ref[...]
    ex = jnp.exp(lg - jnp.max(lg, axis=0, keepdims=True))
    probs = ex / jnp.sum(ex, axis=0, keepdims=True)
    acc = probs[0]
    for l in range(1, layer + 1):
        acc = acc + probs[l]
    lb_all = acc - probs[0]
    lb_f = lb_all[0:1]
    lb_b = lb_all[1:2]

    st_ref[...] = jnp.zeros_like(st_ref)
    o_ref[...] = jnp.zeros_like(o_ref)
    if need_ctx:
        oc_ref[...] = jnp.zeros_like(oc_ref)

    def run(qr, ffr, fbr, ir, outr):
        nc = qr.shape[0] // c
        grp = math.gcd(nc, HG_GROUP)
        ng = nc // grp

        def body(gi, carry):
            rows_f = [pl.multiple_of((gi * grp + k) * c, c) for k in range(grp)]
            rows_b = [pl.multiple_of((nc - 1 - gi * grp - k) * c, c) for k in range(grp)]
            load = lambda gate_ref, r: (qr[pl.ds(r, c), :], gate_ref[pl.ds(r, c), :], ir[pl.ds(r, c), :])
            of, stf = _gla_group([load(ffr, r) for r in rows_f], lb_f, emf_ref[...], lvf_ref[...], st_ref[0], False)
            st_ref[0] = stf
            ob, stb = _gla_group([load(fbr, r) for r in rows_b], lb_b, emb_ref[...], lvb_ref[...], st_ref[1], True)
            st_ref[1] = stb
            if outr is not None:
                for r, o in zip(rows_f + rows_b, of + ob):
                    outr[pl.ds(r, c), :] += o
            return carry

        lax.fori_loop(0, ng, body, 0, unroll=min(HG_UNROLL, ng))

    run(qc_ref, ffc_ref, fbc_ref, ic_ref, oc_ref)
    run(q_ref, ff_ref, fb_ref, i_ref, o_ref)


def _hg_scan(z, zc, lb_logits, layer, need_ctx):
    b, n, cols = z.shape
    nc_ = zc.shape[1]
    width = cols // 5
    heads = width // HG_EXPAND
    e = HG_EXPAND
    depth = lb_logits.shape[0]
    emf, lvf = _scan_tables(0)
    emb, lvb = _scan_tables(1)
    col = lambda rows, group: pl.BlockSpec((None, rows, e), lambda bi, h: (bi, 0, group * heads + h))
    in_specs = [pl.BlockSpec((depth, 2, e), lambda bi, h: (0, 0, h)),
                _resident(emf.shape), _resident(emb.shape), _resident(lvf.shape), _resident(lvb.shape),
                col(nc_, 0), col(nc_, 1), col(nc_, 2), col(nc_, 3),
                col(n, 0), col(n, 1), col(n, 2), col(n, 3)]
    out_specs = [pl.BlockSpec((None, n, e), lambda bi, h: (bi, 0, h))]
    out_shape = [jax.ShapeDtypeStruct((b, n, width), F32)]
    if need_ctx:
        out_specs.append(pl.BlockSpec((None, nc_, e), lambda bi, h: (bi, 0, h)))
        out_shape.append(jax.ShapeDtypeStruct((b, nc_, width), F32))
    res = pl.pallas_call(
        functools.partial(_hg_scan_kernel, layer=layer, need_ctx=need_ctx),
        grid=(b, heads),
        in_specs=in_specs,
        out_specs=out_specs,
        out_shape=out_shape,
        scratch_shapes=[pltpu.VMEM((2, e, e), F32)],
        compiler_params=_cparams("parallel", "parallel"),
        name="hg_scan",
    )(lb_logits, emf, emb, lvf, lvb, zc, zc, zc, zc, z, z, z, z)
    return (res[0], res[1]) if need_ctx else (res[0], None)


def _hg_out_kernel(o_ref, g_ref, x_ref, g1_ref, ng_ref, w_ref, out_ref, *, heads):
    ng = ng_ref[...]
    parts = [_rms(o_ref[:, h * HG_EXPAND:(h + 1) * HG_EXPAND], ng) for h in range(heads)]
    o = jnp.concatenate(parts, axis=1) * _silu(g_ref[...])
    out_ref[...] = x_ref[...] + g1_ref[...] * _dot(o.astype(BF16), w_ref[...])


def _hg_out(o, z, x, mods, row_fn, norm_g, w_out):
    b, n, d = x.shape
    width = o.shape[-1]
    tm = min(ROW_TILE, n)
    tok = lambda wd: pl.BlockSpec((None, tm, wd), lambda bi, i: (bi, i, 0))
    return pl.pallas_call(
        functools.partial(_hg_out_kernel, heads=width // HG_EXPAND),
        grid=(b, n // tm),
        in_specs=[tok(width), pl.BlockSpec((None, tm, width), lambda bi, i: (bi, i, 4)), tok(d),
                  _mod_spec(row_fn, 2, d), _resident((1, HG_EXPAND)), _resident(w_out.shape)],
        out_specs=tok(d),
        out_shape=jax.ShapeDtypeStruct((b, n, d), F32),
        compiler_params=_cparams("parallel", "parallel"),
        name="hg_out",
    )(o, z, x, mods, norm_g.reshape(1, HG_EXPAND), w_out)


def kernel(x, c, ctx, c_ctx, mod_w, mod_b, norm1_g, norm2_g, ffn_w_up, ffn_conv_w, ffn_conv_b, ffn_w_down, ab_w_in, mla_q_norm_g, mla_w_q_b, mla_kv_norm_g, mla_w_kv_b, hy_conv_w, hy_conv_b, hy_w1, hy_b1, hy_w2, hy_b2, hy_w3, hy_bias, ab_w_out, hg_w_in, hg_lb_logits, hg_norm_g, hg_w_out, final_norm_g):
    batch, n, d = x.shape
    depth = mod_w.shape[0]
    assert batch < MOD_ROWS
    ctx_row = batch
    cc = jnp.zeros((MOD_ROWS, d), F32).at[:batch].set(c).at[ctx_row].set(c_ctx)
    mods_all = _mod_vectors(cc, mod_w, mod_b)
    lat_row = lambda bi, *_: bi
    ctx_row_fn = lambda bi, *_: ctx_row
    rope_tabs = _rope_tables(n)

    xc = ctx
    for layer in range(depth):
        need_ctx = layer < depth - 1
        j = layer // 2
        mods = mods_all[layer]
        if layer % 2 == 0:
            q_lora = mla_q_norm_g.shape[-1]
            kv_lora = mla_kv_norm_g.shape[-1]
            hy_ch = hy_bias.shape[-1]
            w = _ab_weights(ab_w_in[j], mla_w_q_b[j], mla_w_kv_b[j], ab_w_out[j], q_lora, kv_lora, hy_ch)
            hy_args = (hy_w1[j], hy_b1[j], hy_w2[j], hy_b2[j], hy_w3[j], hy_bias[j])
            q_c, k_c, v_c, hy_c, hyb_c = _ab_project(xc, mods, ctx_row_fn, norm1_g[layer], w, mla_q_norm_g[j],
                                                     mla_kv_norm_g[j], None, hy_conv_w[j], hy_conv_b[j], hy_ch)
            q, k, v, hy, hyb = _ab_project(x, mods, lat_row, norm1_g[layer], w, mla_q_norm_g[j],
                                           mla_kv_norm_g[j], rope_tabs, hy_conv_w[j], hy_conv_b[j], hy_ch)
            o = _attention(q, [jnp.concatenate([k_c, k], axis=1)], [jnp.concatenate([v_c, v], axis=1)], w["heads"])
            x_new = _ab_out(o, _hyena(hy, hyb, *hy_args), x, mods, lat_row, w)
            if need_ctx:
                o_c = _attention(q_c, [k_c], [v_c], w["heads"])
                xc = _ab_out(o_c, _hyena(hy_c, hyb_c, *hy_args), xc, mods, ctx_row_fn, w)
            x = x_new
        else:
            w_in = hg_w_in[j].astype(BF16)
            z_c = _hg_project(xc, mods, ctx_row_fn, norm1_g[layer], w_in)
            z = _hg_project(x, mods, lat_row, norm1_g[layer], w_in)
            o, o_c = _hg_scan(z, z_c, hg_lb_logits, layer, need_ctx)
            w_out = hg_w_out[j].astype(BF16)
            x = _hg_out(o, z, x, mods, lat_row, hg_norm_g[j], w_out)
            if need_ctx:
                xc = _hg_out(o_c, z_c, xc, mods, ctx_row_fn, hg_norm_g[j], w_out)
        w_up = ffn_w_up[layer].astype(BF16)
        w_dn = ffn_w_down[layer].astype(BF16)
        last = layer == depth - 1
        x = _conv_ffn(x, mods, lat_row, norm2_g[layer], w_up, ffn_conv_w[layer], ffn_conv_b[layer], w_dn,
                      final_norm_g if last else None)
        if need_ctx:
            xc = _conv_ffn(xc, mods, ctx_row_fn, norm2_g[layer], w_up, ffn_conv_w[layer], ffn_conv_b[layer], w_dn)
    return x
```

```python
import functools
import math

import numpy as np
import jax
import jax.numpy as jnp
from jax import lax
from jax.experimental import pallas as pl
from jax.experimental.pallas import tpu as pltpu

F32 = jnp.float32
BF16 = jnp.bfloat16
HIGHEST = lax.Precision.HIGHEST

EPS = 1e-6
GRID_W = 64

MLA_NOPE = 64
MLA_ROPE = 32
MLA_V = 64
ROPE_NF = MLA_ROPE // 4
ROPE_BASE = 10000.0
HEAD_LANES = 128
ATTN_TQ = 4096
ATTN_SUB = 256

HY_ORDER = 2
HY_BANDS = 16
HY_SIN_FREQ = 1.0
HY_DECAY_TARGET = 1e-2
HY_FAST_DECAY = 0.3
HY_SLOW_DECAY = 1.5
HY_MIN_DECAY = math.log(HY_DECAY_TARGET) / HY_SLOW_DECAY
HY_MAX_DECAY = math.log(HY_DECAY_TARGET) / HY_FAST_DECAY

HG_EXPAND = 128
HG_CHUNK = 64
HG_LEVELS = (32, 16, 8, 4, 2, 1)
HG_GROUP = 4
HG_UNROLL = 2

ROW_TILE = 512
MOD_ROWS = 16
HALO = 8
VMEM_LIMIT = 56 * 1024 * 1024

_NT = (((1,), (1,)), ((), ()))
_TN = (((0,), (0,)), ((), ()))


def _cparams(*sem):
    return pltpu.CompilerParams(dimension_semantics=sem, vmem_limit_bytes=VMEM_LIMIT)


def _dot(a, b, dims=(((1,), (0,)), ((), ()))):
    return lax.dot_general(a, b, dims, precision=lax.Precision.DEFAULT, preferred_element_type=F32)


def _rms(x, g):
    return x * lax.rsqrt(jnp.mean(x * x, axis=-1, keepdims=True) + EPS) * g


def _modulate(x, g, shift, scale):
    return _rms(x, g) * (1.0 + scale) + shift


def _silu(x):
    return x * jax.nn.sigmoid(x)


def _mod_spec(row_fn, k, d):
    return pl.BlockSpec((None, None, 1, d), lambda *idx: (row_fn(*idx), k, 0, 0))


def _resident(shape):
    nd = len(shape)
    return pl.BlockSpec(shape, lambda *idx: (0,) * nd, pipeline_mode=pl.Buffered(1))


def _mod_kernel(c_ref, w_ref, b_ref, o_ref):
    a = _silu(c_ref[...])
    o_ref[...] = jnp.dot(a, w_ref[...], precision=HIGHEST, preferred_element_type=F32) + b_ref[...]


def _mod_vectors(cc, mod_w, mod_b):
    depth, d, n6 = mod_w.shape
    tn = ROW_TILE
    out = pl.pallas_call(
        _mod_kernel,
        grid=(depth, n6 // tn),
        in_specs=[pl.BlockSpec((MOD_ROWS, d), lambda l, j: (0, 0)),
                  pl.BlockSpec((None, d, tn), lambda l, j: (l, 0, j)),
                  pl.BlockSpec((None, 1, tn), lambda l, j: (l, 0, j))],
        out_specs=pl.BlockSpec((None, MOD_ROWS, tn), lambda l, j: (l, 0, j)),
        out_shape=jax.ShapeDtypeStruct((depth, MOD_ROWS, n6), F32),
        compiler_params=_cparams("parallel", "parallel"),
        name="mod_vectors",
    )(cc, mod_w, mod_b.reshape(depth, 1, n6))
    return out.reshape(depth, MOD_ROWS, 6, 1, d)


def _halo_rows(x_ref, xp_ref, xn_ref, mod):
    i = pl.program_id(1)
    last = pl.num_programs(1) - 1
    hp = mod(xp_ref[...]) * (i > 0).astype(F32)
    hn = mod(xn_ref[...]) * (i < last).astype(F32)
    return jnp.concatenate([hp, mod(x_ref[...]), hn], axis=0).astype(BF16)


def _conv3_rows(a, w, bias, tm):
    ne = tm + 2 * HALO
    prev = pltpu.roll(a, 1, axis=0)[HALO:HALO + tm]
    nxt = pltpu.roll(a, ne - 1, axis=0)[HALO:HALO + tm]
    return prev * w[0:1] + a[HALO:HALO + tm] * w[1:2] + nxt * w[2:3] + bias


def _ab_proj_kernel(*refs, rope, q_lora, kv_lora, heads, scale, ch):
    refs = refs[:-7] + refs[-5:]
    if rope:
        (x_ref, xp_ref, xn_ref, sh_ref, sc_ref, g_ref, win_ref, qg_ref, wqa_ref, wqb_ref, kvg_ref, wk_ref,
         wv_ref, cw_ref, cb_ref, cos_ref, sin_ref, q_ref, k_ref, v_ref, hy_ref, hyb_ref) = refs
    else:
        (x_ref, xp_ref, xn_ref, sh_ref, sc_ref, g_ref, win_ref, qg_ref, wqa_ref, kvg_ref, wk_ref,
         wv_ref, cw_ref, cb_ref, q_ref, k_ref, v_ref, hy_ref, hyb_ref) = refs
    tm = x_ref.shape[0]
    he = _halo_rows(x_ref, xp_ref, xn_ref, lambda a: _modulate(a, g_ref[...], sh_ref[...], sc_ref[...]))
    ze = _dot(he, win_ref[...])
    o = q_lora + kv_lora
    hy = _conv3_rows(ze[:, o + 2 * HEAD_LANES:], cw_ref[...], cb_ref[...], tm)
    hy_ref[...] = hy
    hyb_ref[...] = hy[:, :ch].astype(BF16)
    z = ze[HALO:HALO + tm]
    qn = _rms(z[:, :q_lora], qg_ref[...]).astype(BF16)
    kvn = _rms(z[:, q_lora:o], kvg_ref[...]).astype(BF16)
    v_ref[...] = _dot(kvn, wv_ref[...]).astype(BF16)
    qa = _dot(qn, wqa_ref[...])
    kk = _dot(kvn, wk_ref[...])
    kr = z[:, o:o + HEAD_LANES]
    if rope:
        cos = cos_ref[...]
        sin = sin_ref[...]
        qb = _dot(qn, wqb_ref[...])
        kr = kr * cos + z[:, o + HEAD_LANES:o + 2 * HEAD_LANES] * sin
    for hh in range(heads):
        sl = slice(hh * HEAD_LANES, (hh + 1) * HEAD_LANES)
        qh = qa[:, sl]
        if rope:
            qh = qh * cos + qb[:, sl] * sin
        q_ref[:, sl] = (qh * scale).astype(BF16)
        k_ref[:, sl] = (kk[:, sl] + kr).astype(BF16)


def _rope_swap(w):
    nf = ROPE_NF
    return jnp.concatenate([-w[:, nf:2 * nf], w[:, :nf], -w[:, 3 * nf:4 * nf], w[:, 2 * nf:3 * nf]], axis=1)


def _ab_weights(w_in, w_q_b, w_kv_b, w_out, q_lora, kv_lora, hy_ch):
    d = w_in.shape[0]
    heads = w_q_b.shape[1] // (MLA_NOPE + MLA_ROPE)
    o = q_lora + kv_lora
    w_kr = w_in[:, o:o + MLA_ROPE]
    zpad = lambda n: jnp.zeros((d, n), F32)
    tail = HEAD_LANES - MLA_NOPE - MLA_ROPE
    win = jnp.concatenate(
        [w_in[:, :o],
         zpad(MLA_NOPE), w_kr, zpad(tail),
         zpad(MLA_NOPE), _rope_swap(w_kr), zpad(tail),
         w_in[:, o + MLA_ROPE:]], axis=1).astype(BF16)
    wq = w_q_b.reshape(q_lora, heads, MLA_NOPE + MLA_ROPE)
    zq = jnp.zeros((q_lora, heads, tail), F32)
    wqa = jnp.concatenate([wq, zq], axis=2).reshape(q_lora, heads * HEAD_LANES).astype(BF16)
    wq_rope = wq[:, :, MLA_NOPE:].reshape(q_lora * heads, MLA_ROPE)
    wqb = jnp.concatenate([jnp.zeros((q_lora, heads, MLA_NOPE), F32),
                           _rope_swap(wq_rope).reshape(q_lora, heads, MLA_ROPE), zq],
                          axis=2).reshape(q_lora, heads * HEAD_LANES).astype(BF16)
    wkv = w_kv_b.reshape(kv_lora, heads, MLA_NOPE + MLA_V)
    zk = jnp.zeros((kv_lora, heads, HEAD_LANES - MLA_NOPE), F32)
    wk = jnp.concatenate([wkv[:, :, :MLA_NOPE], zk], axis=2).reshape(kv_lora, heads * HEAD_LANES).astype(BF16)
    zv = jnp.zeros((kv_lora, heads, HEAD_LANES - MLA_V), F32)
    wv = jnp.concatenate([wkv[:, :, MLA_NOPE:], zv], axis=2).reshape(kv_lora, heads * HEAD_LANES).astype(BF16)
    wo = w_out[:heads * MLA_V].reshape(heads, MLA_V, d)
    wo_a = jnp.concatenate([wo, jnp.zeros((heads, HEAD_LANES - MLA_V, d), F32)],
                           axis=1).reshape(heads * HEAD_LANES, d).astype(BF16)
    wo_h = w_out[heads * MLA_V:].astype(BF16)
    return dict(win=win, wqa=wqa, wqb=wqb, wk=wk, wv=wv, wo_a=wo_a, wo_h=wo_h, heads=heads)


def _rope_tables(n):
    rows = n // GRID_W
    row = jnp.repeat(jnp.arange(rows), GRID_W).astype(F32)
    col = jnp.tile(jnp.arange(GRID_W), rows).astype(F32)
    inv = ROPE_BASE ** (-jnp.arange(ROPE_NF, dtype=F32) / ROPE_NF)
    ar = row[:, None] * inv
    ac = col[:, None] * inv
    ones = jnp.ones((n, MLA_NOPE), F32)
    tail = HEAD_LANES - MLA_NOPE - MLA_ROPE
    cos = jnp.concatenate([ones, jnp.cos(ar), jnp.cos(ar), jnp.cos(ac), jnp.cos(ac),
                           jnp.ones((n, tail), F32)], axis=1)
    sin = jnp.concatenate([0 * ones, jnp.sin(ar), jnp.sin(ar), jnp.sin(ac), jnp.sin(ac),
                           jnp.zeros((n, tail), F32)], axis=1)
    return cos, sin


def _halo_specs(tm, n, d):
    nh = n // HALO
    tpb = tm // HALO
    return [pl.BlockSpec((None, tm, d), lambda bi, i: (bi, i, 0)),
            pl.BlockSpec((None, HALO, d), lambda bi, i: (bi, jnp.maximum(i * tpb - 1, 0), 0)),
            pl.BlockSpec((None, HALO, d), lambda bi, i: (bi, jnp.minimum((i + 1) * tpb, nh - 1), 0))]


def _ab_project(x, mods, row_fn, norm_g, w, q_norm_g, kv_norm_g, rope_tabs, conv_w, conv_b, ch, kv_into, kv_row0):
    b, n, d = x.shape
    tm = min(ROW_TILE, n)
    assert kv_row0 % tm == 0
    heads = w["heads"]
    q_lora = q_norm_g.shape[-1]
    kv_lora = kv_norm_g.shape[-1]
    hw = heads * HEAD_LANES
    hy_cols = conv_w.shape[-1]
    rope = rope_tabs is not None
    tok = lambda width: pl.BlockSpec((None, tm, width), lambda bi, i: (bi, i, 0))
    in_specs = _halo_specs(tm, n, d) + [
        _mod_spec(row_fn, 0, d), _mod_spec(row_fn, 1, d), _resident((1, d)),
        _resident(w["win"].shape), _resident((1, q_lora)), _resident(w["wqa"].shape)]
    args = [x, x, x, mods, mods, norm_g.reshape(1, d), w["win"], q_norm_g.reshape(1, q_lora), w["wqa"]]
    if rope:
        in_specs.append(_resident(w["wqb"].shape))
        args.append(w["wqb"])
    in_specs += [_resident((1, kv_lora)), _resident(w["wk"].shape), _resident(w["wv"].shape),
                 _resident(conv_w.shape), _resident((1, hy_cols))]
    args += [kv_norm_g.reshape(1, kv_lora), w["wk"], w["wv"], conv_w, conv_b.reshape(1, hy_cols)]
    if rope:
        in_specs += [pl.BlockSpec((tm, HEAD_LANES), lambda bi, i: (i, 0))] * 2
        args += list(rope_tabs)
    aliases = {len(args): 1, len(args) + 1: 2}
    in_specs += [pl.BlockSpec(memory_space=pl.ANY)] * 2
    args += list(kv_into)
    kern = functools.partial(_ab_proj_kernel, rope=rope, q_lora=q_lora, kv_lora=kv_lora, heads=heads, ch=ch,
                             scale=math.log2(math.e) / math.sqrt(MLA_NOPE + MLA_ROPE))
    kv_spec = pl.BlockSpec((None, tm, hw), lambda bi, i: (bi, i + kv_row0 // tm, 0))
    kv_shape = jax.ShapeDtypeStruct(kv_into[0].shape, BF16)
    return pl.pallas_call(
        kern,
        grid=(b, n // tm),
        in_specs=in_specs,
        out_specs=[tok(hw), kv_spec, kv_spec, tok(hy_cols), tok(ch)],
        out_shape=[jax.ShapeDtypeStruct((b, n, hw), BF16), kv_shape, kv_shape,
                   jax.ShapeDtypeStruct((b, n, hy_cols), F32), jax.ShapeDtypeStruct((b, n, ch), BF16)],
        input_output_aliases=aliases,
        compiler_params=_cparams("parallel", "parallel"),
        name="ab_project",
    )(*args)


def _attn_kernel(*refs, sub, nseg):
    q_ref, o_ref = refs[0], refs[-1]
    ks = [r[...] for r in refs[1:1 + nseg]]
    vs = [r[...] for r in refs[1 + nseg:1 + 2 * nseg]]
    for r in range(0, q_ref.shape[0], sub):
        q = q_ref[r:r + sub, :]
        ss = [_dot(q, k, _NT) for k in ks]
        m = functools.reduce(jnp.maximum, [jnp.max(s, axis=-1, keepdims=True) for s in ss])
        ps = [jnp.exp2(s - m) for s in ss]
        l = sum(jnp.sum(p, axis=-1, keepdims=True) for p in ps)
        o = sum(_dot(p.astype(BF16), v) for p, v in zip(ps, vs))
        o_ref[r:r + sub, :] = (o / l).astype(o_ref.dtype)


def _attention(q, ks, vs, heads, kv_rows=None, kv_row0=0):
    b, n, hw = q.shape
    tq = min(ATTN_TQ, n)
    tile = pl.BlockSpec((None, tq, HEAD_LANES), lambda bi, h, i: (bi, i, h))
    rows = lambda a: kv_rows or a.shape[1]
    assert all(kv_row0 % rows(a) == 0 for a in ks)
    whole = lambda a: pl.BlockSpec((None, rows(a), HEAD_LANES), lambda bi, h, i: (bi, kv_row0 // rows(a), h))
    return pl.pallas_call(
        functools.partial(_attn_kernel, sub=min(ATTN_SUB, tq), nseg=len(ks)),
        grid=(b, heads, n // tq),
        in_specs=[tile] + [whole(a) for a in ks] + [whole(a) for a in vs],
        out_specs=tile,
        out_shape=jax.ShapeDtypeStruct((b, n, hw), BF16),
        compiler_params=_cparams("parallel", "parallel", "parallel"),
        name="attention",
    )(q, *ks, *vs)


def _hy_filter_kernel(z_ref, t_ref, w1_ref, b1_ref, w2_ref, b2_ref, w3_ref, dl_ref, o_ref, *, ch):
    hdot = lambda a, w: jnp.dot(a, w, precision=HIGHEST, preferred_element_type=F32)
    a = jnp.sin(HY_SIN_FREQ * (hdot(z_ref[...], w1_ref[...]) + b1_ref[...]))
    a = jnp.sin(HY_SIN_FREQ * (hdot(a, w2_ref[...]) + b2_ref[...]))
    f = hdot(a, w3_ref[...])
    dec = jnp.exp(-t_ref[...] * dl_ref[...])
    tl = z_ref.shape[0]
    row = pl.program_id(0) * tl + lax.broadcasted_iota(jnp.int32, (tl, ch), 0)
    for o in range(HY_ORDER):
        hf = f[:, (2 * o) * ch:(2 * o + 1) * ch] * dec
        hb = jnp.where(row > 0, f[:, (2 * o + 1) * ch:(2 * o + 2) * ch] * dec, 0.0)
        o_ref[:, (2 * o) * ch:(2 * o + 1) * ch] = (hf + hb).astype(BF16)
        o_ref[:, (2 * o + 1) * ch:(2 * o + 2) * ch] = (hf - hb).astype(BF16)


def _hy_filter_inputs(length, w1, b1, w2, b2, w3, ch):
    t = jnp.linspace(0.0, 1.0, length, dtype=F32)[:, None]
    wv = 2.0 * math.pi * jnp.arange(length, dtype=F32)[:, None] / length
    f = jnp.linspace(1e-4, HY_BANDS - 1, HY_BANDS, dtype=F32)
    z = jnp.concatenate([t, jnp.cos(f * wv), -jnp.sin(f * wv)], axis=-1)
    emb = z.shape[1]
    z = jnp.pad(z, ((0, 0), (0, HEAD_LANES - emb)))
    w1p = jnp.pad(w1, ((0, HEAD_LANES - emb), (0, 0)))
    deltas = jnp.abs(jnp.linspace(HY_MIN_DECAY, HY_MAX_DECAY, ch, dtype=F32))[None, :]
    fw = w1.shape[1]
    tl = min(ROW_TILE, length)
    return pl.pallas_call(
        functools.partial(_hy_filter_kernel, ch=ch),
        grid=(length // tl,),
        in_specs=[pl.BlockSpec((tl, HEAD_LANES), lambda i: (i, 0)), pl.BlockSpec((tl, 1), lambda i: (i, 0)),
                  _resident((HEAD_LANES, fw)), _resident((1, fw)), _resident((fw, fw)), _resident((1, fw)),
                  _resident(w3.shape), _resident((1, ch))],
        out_specs=pl.BlockSpec((tl, 2 * HY_ORDER * ch), lambda i: (i, 0)),
        out_shape=jax.ShapeDtypeStruct((length, 2 * HY_ORDER * ch), BF16),
        compiler_params=_cparams("parallel"),
        name="hy_filter",
    )(z, t, w1p, b1.reshape(1, fw), w2, b2.reshape(1, fw), w3, deltas)


def _dft_tables(length):
    n2 = 2 * length
    theta = 2.0 * math.pi / n2
    n = jnp.arange(length, dtype=jnp.int32)[None, :]
    a = 8 * jnp.arange(length // 8, dtype=jnp.int32)[:, None]
    b = jnp.arange(8, dtype=jnp.int32)[:, None]
    ang_a = ((a * n) % n2).astype(F32) * theta
    ang_b = ((b * n) % n2).astype(F32) * theta
    ca, sa = jnp.cos(ang_a)[:, None, :], jnp.sin(ang_a)[:, None, :]
    cb, sb = jnp.cos(ang_b)[None, :, :], jnp.sin(ang_b)[None, :, :]
    fc = (ca * cb - sa * sb).reshape(length, length)
    fs = -(sa * cb + ca * sb).reshape(length, length)
    return fc.astype(BF16), fs.astype(BF16)


def _alternating_sum(x):
    col = lax.broadcasted_iota(jnp.int32, (8, x.shape[0]), 1)
    alt = (1 - 2 * (col & 1)).astype(F32).astype(BF16)
    return _dot(alt, x)[0:1]


def _filter_dft_kernel(hs_ref, hd_ref, fc_ref, fs_ref, o_ref):
    hs = hs_ref[...]
    o_ref[0] = _dot(fc_ref[...], hs)
    ki = _dot(fs_ref[...], hd_ref[...])
    nyq = _alternating_sum(hs)
    tf = ki.shape[0]
    row = pl.program_id(1) * tf + lax.broadcasted_iota(jnp.int32, (tf, 1), 0)
    o_ref[1] = jnp.where(row == 0, nyq, ki)


def _dft_spec_kernel(x_ref, fc_ref, fs_ref, k_ref, o_ref, *, inv_n):
    x = x_ref[...]
    ur = _dot(fc_ref[...], x)
    kr = k_ref[0]
    ki = k_ref[1]
    tf = ur.shape[0]
    row = pl.program_id(1) * tf + lax.broadcasted_iota(jnp.int32, (tf, 1), 0)
    first = row == 0
    ui = jnp.where(first, _alternating_sum(x), _dot(fs_ref[...], x))
    uiki = ui * ki
    yr = ur * kr - jnp.where(first, 0.0, uiki)
    yi = jnp.where(first, uiki, ur * ki + ui * kr)
    scale = jnp.where(first, inv_n, 2.0 * inv_n)
    o_ref[0] = (yr * scale).astype(BF16)
    o_ref[1] = (yi * scale).astype(BF16)


def _filter_spectrum(p, fc, fs, ch):
    length, cols = p.shape
    tf = min(ROW_TILE, length)
    return pl.pallas_call(
        _filter_dft_kernel,
        grid=(HY_ORDER, length // tf),
        in_specs=[pl.BlockSpec((length, ch), lambda o, i: (0, 2 * o)),
                  pl.BlockSpec((length, ch), lambda o, i: (0, 2 * o + 1)),
                  pl.BlockSpec((tf, length), lambda o, i: (i, 0)),
                  pl.BlockSpec((tf, length), lambda o, i: (i, 0))],
        out_specs=pl.BlockSpec((None, 2, tf, ch), lambda o, i: (o, 0, i, 0)),
        out_shape=jax.ShapeDtypeStruct((HY_ORDER, 2, length, ch), F32),
        compiler_params=_cparams("parallel", "arbitrary"),
        name="hy_filter_dft",
    )(p, p, fc, fs)


def _hy_spectral(xb, col, ch, fc, fs, kspec):
    b, length, _ = xb.shape
    tf = min(ROW_TILE, length)
    return pl.pallas_call(
        functools.partial(_dft_spec_kernel, inv_n=1.0 / (2 * length)),
        grid=(b, length // tf),
        in_specs=[pl.BlockSpec((None, length, ch), lambda bi, i: (bi, 0, col)),
                  pl.BlockSpec((tf, length), lambda bi, i: (i, 0)),
                  pl.BlockSpec((tf, length), lambda bi, i: (i, 0)),
                  pl.BlockSpec((2, tf, ch), lambda bi, i: (0, i, 0))],
        out_specs=pl.BlockSpec((None, 2, tf, ch), lambda bi, i: (bi, 0, i, 0)),
        out_shape=jax.ShapeDtypeStruct((b, 2, length, ch), BF16),
        compiler_params=_cparams("parallel", "arbitrary"),
        name="hy_dft_spectral",
    )(xb, fc, fs, kspec)


def _idft_gate_kernel(y_ref, fc_ref, fs_ref, u_ref, xg_ref, d_ref, o_ref, ob_ref):
    tt = fc_ref.shape[0]
    t = pl.program_id(1) * tt + lax.broadcasted_iota(jnp.int32, (tt, 1), 0)
    sign = (1 - 2 * (t & 1)).astype(F32)
    conv = (_dot(fc_ref[...], y_ref[0]) + _dot(fs_ref[...], y_ref[1])
            + sign * y_ref[1, 0:1, :].astype(F32))
    out = xg_ref[...] * (conv + u_ref[...] * d_ref[...])
    o_ref[...] = out
    ob_ref[...] = out.astype(BF16)


def _hy_inverse_gate(y, fc, fs, u, u_col, xg, xg_col, dvec, ch):
    b, _, length, _ = y.shape
    tt = min(ROW_TILE, length)
    out = pl.BlockSpec((None, tt, ch), lambda bi, i: (bi, i, 0))
    return pl.pallas_call(
        _idft_gate_kernel,
        grid=(b, length // tt),
        in_specs=[pl.BlockSpec((None, 2, length, ch), lambda bi, i: (bi, 0, 0, 0)),
                  pl.BlockSpec((tt, length), lambda bi, i: (i, 0)),
                  pl.BlockSpec((tt, length), lambda bi, i: (i, 0)),
                  pl.BlockSpec((None, tt, ch), lambda bi, i: (bi, i, u_col)),
                  pl.BlockSpec((None, tt, ch), lambda bi, i: (bi, i, xg_col)),
                  pl.BlockSpec((1, ch), lambda bi, i: (0, 0))],
        out_specs=[out, out],
        out_shape=[jax.ShapeDtypeStruct((b, length, ch), F32), jax.ShapeDtypeStruct((b, length, ch), BF16)],
        compiler_params=_cparams("parallel", "arbitrary"),
        name="hy_idft_gate",
    )(y, fc, fs, u, xg, dvec.reshape(1, ch))


def _hyena(hyc, hyb, w1, b1, w2, b2, w3, hy_bias):
    ch = hy_bias.shape[-1]
    length = hyc.shape[1]
    fc, fs = _dft_tables(length)
    kspec = _filter_spectrum(_hy_filter_inputs(length, w1, b1, w2, b2, w3, ch), fc, fs, ch)
    y = _hy_spectral(hyb, 0, ch, fc, fs, kspec[0])
    y1, y1b = _hy_inverse_gate(y, fc, fs, hyc, 0, hyc, 1, hy_bias[0], ch)
    y = _hy_spectral(y1b, 0, ch, fc, fs, kspec[1])
    y2, _ = _hy_inverse_gate(y, fc, fs, y1, 0, hyc, 2, hy_bias[1], ch)
    return y2


def _ab_out_kernel(o_ref, y_ref, x_ref, g1_ref, woa_ref, woh_ref, out_ref):
    y = _dot(o_ref[...], woa_ref[...]) + _dot(y_ref[...].astype(BF16), woh_ref[...])
    out_ref[...] = x_ref[...] + g1_ref[...] * y


def _ab_out(o, y2, x, mods, row_fn, w):
    b, n, d = x.shape
    tm = min(ROW_TILE, n)
    tok = lambda width: pl.BlockSpec((None, tm, width), lambda bi, i: (bi, i, 0))
    return pl.pallas_call(
        _ab_out_kernel,
        grid=(b, n // tm),
        in_specs=[tok(o.shape[-1]), tok(y2.shape[-1]), tok(d), _mod_spec(row_fn, 2, d),
                  _resident(w["wo_a"].shape), _resident(w["wo_h"].shape)],
        out_specs=tok(d),
        out_shape=jax.ShapeDtypeStruct((b, n, d), F32),
        compiler_params=_cparams("parallel", "parallel"),
        name="ab_out",
    )(o, y2, x, mods, w["wo_a"], w["wo_h"])


FFN_ROWS = 512
FFN_CHUNK_MAX = 2816


def _ffn_kernel(*refs, hidden, hc, final):
    if final:
        (x_ref, xp_ref, xn_ref, sh_ref, sc_ref, g2_ref, ng_ref, wup_ref, cw_ref, cb_ref, wdn_ref,
         fg_ref, o_ref) = refs
    else:
        (x_ref, xp_ref, xn_ref, sh_ref, sc_ref, g2_ref, ng_ref, wup_ref, cw_ref, cb_ref, wdn_ref,
         o_ref) = refs
    x = x_ref[...]
    tm, d = x.shape
    he = _halo_rows(x_ref, xp_ref, xn_ref, lambda a: _modulate(a, ng_ref[...], sh_ref[...], sc_ref[...]))

    def conv(a, col):
        return _conv3_rows(a, cw_ref[:, col:col + hc], cb_ref[:, col:col + hc], tm)

    acc = jnp.zeros((tm, d), F32)
    for j in range(hidden // hc):
        gate = conv(_dot(he, wup_ref[:, j * hc:(j + 1) * hc]), j * hc)
        val = conv(_dot(he, wup_ref[:, hidden + j * hc:hidden + (j + 1) * hc]), hidden + j * hc)
        mid = (_silu(gate) * val).astype(BF16)
        acc = acc + _dot(mid, wdn_ref[j * hc:(j + 1) * hc, :])
    out = x + g2_ref[...] * acc
    if final:
        out = _rms(out, fg_ref[...])
    o_ref[...] = out


def _ffn_chunk(hidden):
    for hc in range(min(FFN_CHUNK_MAX, hidden) // HEAD_LANES * HEAD_LANES, 0, -HEAD_LANES):
        if hidden % hc == 0:
            return hc
    raise ValueError(f"FFN hidden width {hidden} is not a multiple of {HEAD_LANES}")


def _conv_ffn(x, mods, row_fn, norm_g, w_up, conv_w, conv_b, w_down, final_g=None):
    b, n, d = x.shape
    hidden = w_down.shape[0]
    tm = min(FFN_ROWS, n)
    final = final_g is not None
    tok = pl.BlockSpec((None, tm, d), lambda bi, i: (bi, i, 0))
    in_specs = _halo_specs(tm, n, d) + [
        _mod_spec(row_fn, 3, d), _mod_spec(row_fn, 4, d), _mod_spec(row_fn, 5, d),
        _resident((1, d)), _resident(w_up.shape), _resident(conv_w.shape),
        _resident((1, 2 * hidden)), _resident(w_down.shape)]
    args = [x, x, x, mods, mods, mods, norm_g.reshape(1, d), w_up, conv_w, conv_b.reshape(1, 2 * hidden), w_down]
    if final:
        in_specs.append(_resident((1, d)))
        args.append(final_g.reshape(1, d))
    return pl.pallas_call(
        functools.partial(_ffn_kernel, hidden=hidden, hc=_ffn_chunk(hidden), final=final),
        grid=(b, n // tm),
        in_specs=in_specs,
        out_specs=tok,
        out_shape=jax.ShapeDtypeStruct((b, n, d), F32),
        compiler_params=_cparams("parallel", "parallel"),
        name="conv_ffn",
    )(*args)


def _hg_proj_kernel(x_ref, sh_ref, sc_ref, g_ref, w_ref, o_ref, *, width):
    h = _modulate(x_ref[...], g_ref[...], sh_ref[...], sc_ref[...]).astype(BF16)
    z = _dot(h, w_ref[...])
    o_ref[:, :width] = _silu(z[:, :width])
    o_ref[:, width:] = z[:, width:]


def _hg_project(x, mods, row_fn, norm_g, w_in):
    b, n, d = x.shape
    cols = w_in.shape[1]
    tm = min(ROW_TILE, n)
    tok = lambda width: pl.BlockSpec((None, tm, width), lambda bi, i: (bi, i, 0))
    return pl.pallas_call(
        functools.partial(_hg_proj_kernel, width=cols // 5),
        grid=(b, n // tm),
        in_specs=[tok(d), _mod_spec(row_fn, 0, d), _mod_spec(row_fn, 1, d), _resident((1, d)),
                  _resident(w_in.shape)],
        out_specs=tok(cols),
        out_shape=jax.ShapeDtypeStruct((b, n, cols), F32),
        compiler_params=_cparams("parallel", "parallel"),
        name="hg_project",
    )(x, mods, mods, norm_g.reshape(1, d), w_in)


def _scan_tables(direction):
    c = HG_CHUNK
    idx = np.arange(c)
    level = np.full((c, c), -1, np.int32)
    for li, m in enumerate(HG_LEVELS):
        same = (idx[:, None] // (2 * m)) == (idx[None, :] // (2 * m))
        up = (idx % (2 * m)) >= m
        pair = same & (up[:, None] & ~up[None, :] if direction == 0 else ~up[:, None] & up[None, :])
        level[pair] = li
    level[idx, idx] = len(HG_LEVELS)
    if direction == 0:
        to_q = (idx[None, :] <= idx[:, None])
    else:
        to_q = (idx[None, :] >= idx[:, None])
    em = np.concatenate([to_q.astype(np.float32), np.ones((8, c), np.float32)], axis=0)
    return jnp.asarray(np.concatenate([em, em], axis=1), BF16), jnp.asarray(level)


def _level_exponent(run, m, rev):
    c, w = run.shape
    off = m if rev else m - 1
    if 2 * m >= 8:
        parts = [jnp.broadcast_to(run[a + off:a + off + 1, :], (2 * m, w)) for a in range(0, c, 2 * m)]
    elif m == 2:
        low = lax.broadcasted_iota(jnp.int32, (8, w), 0) < 4
        parts = [jnp.where(low, jnp.broadcast_to(run[a + off:a + off + 1, :], (8, w)),
                           jnp.broadcast_to(run[a + 4 + off:a + 5 + off, :], (8, w))) for a in range(0, c, 8)]
    else:
        odd = (lax.broadcasted_iota(jnp.int32, (c, w), 0) & 1) == 1
        if rev:
            parts = [jnp.where(odd, run, pltpu.roll(run, c - 1, axis=0))]
        else:
            parts = [jnp.where(odd, pltpu.roll(run, 1, axis=0), run)]
    ref = parts[0] if len(parts) == 1 else jnp.concatenate(parts, axis=0)
    bits = lax.bitcast_convert_type(run - ref, jnp.uint32) | jnp.uint32(0x80000000)
    return lax.bitcast_convert_type(bits, F32)


def _gla_group(chunks, lb, em2, lv, st, rev):
    c = HG_CHUNK
    e_ = HG_EXPAND
    nl = len(HG_LEVELS)
    log_lb = jnp.log(lb)
    log_1m = jnp.log1p(-lb)
    log_f = []
    keys = []
    for _, fr, _ in chunks:
        t = jnp.exp(-jnp.abs(fr))
        d = 1.0 + t
        keys.append((1.0 - lb) * (jnp.where(fr > 0, t, 1.0) / d))
        b = log_1m + (jnp.minimum(fr, 0.0) - jnp.log(d))
        log_f.append(jnp.maximum(log_lb, b) + jnp.log(1.0 + jnp.exp(-jnp.abs(log_lb - b))))
    g = jnp.concatenate(log_f, axis=1) * math.log2(math.e)
    hi = g.astype(BF16)
    lo = (g - hi.astype(F32)).astype(BF16)
    e_all = _dot(em2, jnp.concatenate([hi, lo], axis=0))
    outs = []
    for gi, (q, _, val) in enumerate(chunks):
        to_q = e_all[:c, gi * e_:(gi + 1) * e_]
        total = e_all[c:c + 1, gi * e_:(gi + 1) * e_]
        key = keys[gi]
        att = jnp.zeros((c, c), F32)
        for li, m in enumerate(HG_LEVELS):
            dec = jnp.exp2(_level_exponent(to_q, m, rev))
            att = jnp.where(lv == li, _dot((q * dec).astype(BF16), (key * dec).astype(BF16), _NT), att)
        att = jnp.where(lv == nl, jnp.sum(q * key, axis=1, keepdims=True), att)
        vb = val.astype(BF16)
        q_in = (q * jnp.exp2(to_q)).astype(BF16)
        outs.append(_dot(att.astype(BF16), vb) + _dot(q_in, st.astype(BF16), _NT))
        k_out = (key * jnp.exp2(total - to_q)).astype(BF16)
        st = st * jnp.exp2(total) + _dot(vb, k_out, _TN)
    return outs, st


def _hg_scan_kernel(*refs, layer, need_ctx):
    (lg_ref, emf_ref, emb_ref, lvf_ref, lvb_ref,
     qc_ref, ffc_ref, fbc_ref, ic_ref, q_ref, ff_ref, fb_ref, i_ref) = refs[:13]
    if need_ctx:
        o_ref, oc_ref, st_ref = refs[13:]
    else:
        o_ref, st_ref = refs[13:]
        oc_ref = None
    c = HG_CHUNK
    lg = lg_ref[...]
    ex = jnp.exp(lg - jnp.max(lg, axis=0, keepdims=True))
    probs = ex / jnp.sum(ex, axis=0, keepdims=True)
    acc = probs[0]
    for l in range(1, layer + 1):
        acc = acc + probs[l]
    lb_all = acc - probs[0]
    lb_f = lb_all[0:1]
    lb_b = lb_all[1:2]

    st_ref[...] = jnp.zeros_like(st_ref)
    o_ref[...] = jnp.zeros_like(o_ref)
    if need_ctx:
        oc_ref[...] = jnp.zeros_like(oc_ref)

    def run(qr, ffr, fbr, ir, outr):
        nc = qr.shape[0] // c
        grp = math.gcd(nc, HG_GROUP)
        ng = nc // grp

        def body(gi, carry):
            rows_f = [pl.multiple_of((gi * grp + k) * c, c) for k in range(grp)]
            rows_b = [pl.multiple_of((nc - 1 - gi * grp - k) * c, c) for k in range(grp)]
            load = lambda gate_ref, r: (qr[pl.ds(r, c), :], gate_ref[pl.ds(r, c), :], ir[pl.ds(r, c), :])
            of, stf = _gla_group([load(ffr, r) for r in rows_f], lb_f, emf_ref[...], lvf_ref[...], st_ref[0], False)
            st_ref[0] = stf
            ob, stb = _gla_group([load(fbr, r) for r in rows_b], lb_b, emb_ref[...], lvb_ref[...], st_ref[1], True)
            st_ref[1] = stb
            if outr is not None:
                for r, o in zip(rows_f + rows_b, of + ob):
                    outr[pl.ds(r, c), :] += o
            return carry

        lax.fori_loop(0, ng, body, 0, unroll=min(HG_UNROLL, ng))

    run(qc_ref, ffc_ref, fbc_ref, ic_ref, oc_ref)
    run(q_ref, ff_ref, fb_ref, i_ref, o_ref)


def _hg_scan(z, zc, lb_logits, layer, need_ctx):
    b, n, cols = z.shape
    nc_ = zc.shape[1]
    width = cols // 5
    heads = width // HG_EXPAND
    e = HG_EXPAND
    depth = lb_logits.shape[0]
    emf, lvf = _scan_tables(0)
    emb, lvb = _scan_tables(1)
    col = lambda rows, group: pl.BlockSpec((None, rows, e), lambda bi, h: (bi, 0, group * heads + h))
    in_specs = [pl.BlockSpec((depth, 2, e), lambda bi, h: (0, 0, h)),
                _resident(emf.shape), _resident(emb.shape), _resident(lvf.shape), _resident(lvb.shape),
                col(nc_, 0), col(nc_, 1), col(nc_, 2), col(nc_, 3),
                col(n, 0), col(n, 1), col(n, 2), col(n, 3)]
    out_specs = [pl.BlockSpec((None, n, e), lambda bi, h: (bi, 0, h))]
    out_shape = [jax.ShapeDtypeStruct((b, n, width), F32)]
    if need_ctx:
        out_specs.append(pl.BlockSpec((None, nc_, e), lambda bi, h: (bi, 0, h)))
        out_shape.append(jax.ShapeDtypeStruct((b, nc_, width), F32))
    res = pl.pallas_call(
        functools.partial(_hg_scan_kernel, layer=layer, need_ctx=need_ctx),
        grid=(b, heads),
        in_specs=in_specs,
        out_specs=out_specs,
        out_shape=out_shape,
        scratch_shapes=[pltpu.VMEM((2, e, e), F32)],
        compiler_params=_cparams("parallel", "parallel"),
        name="hg_scan",
    )(lb_logits, emf, emb, lvf, lvb, zc, zc, zc, zc, z, z, z, z)
    return (res[0], res[1]) if need_ctx else (res[0], None)


def _hg_out_kernel(o_ref, g_ref, x_ref, g1_ref, ng_ref, w_ref, out_ref, *, heads):
    ng = ng_ref[...]
    parts = [_rms(o_ref[:, h * HG_EXPAND:(h + 1) * HG_EXPAND], ng) for h in range(heads)]
    o = jnp.concatenate(parts, axis=1) * _silu(g_ref[...])
    out_ref[...] = x_ref[...] + g1_ref[...] * _dot(o.astype(BF16), w_ref[...])


def _hg_out(o, z, x, mods, row_fn, norm_g, w_out):
    b, n, d = x.shape
    width = o.shape[-1]
    tm = min(ROW_TILE, n)
    tok = lambda wd: pl.BlockSpec((None, tm, wd), lambda bi, i: (bi, i, 0))
    return pl.pallas_call(
        functools.partial(_hg_out_kernel, heads=width // HG_EXPAND),
        grid=(b, n // tm),
        in_specs=[tok(width), pl.BlockSpec((None, tm, width), lambda bi, i: (bi, i, 4)), tok(d),
                  _mod_spec(row_fn, 2, d), _resident((1, HG_EXPAND)), _resident(w_out.shape)],
        out_specs=tok(d),
        out_shape=jax.ShapeDtypeStruct((b, n, d), F32),
        compiler_params=_cparams("parallel", "parallel"),
        name="hg_out",
    )(o, z, x, mods, norm_g.reshape(1, HG_EXPAND), w_out)


def kernel(x, c, ctx, c_ctx, mod_w, mod_b, norm1_g, norm2_g, ffn_w_up, ffn_conv_w, ffn_conv_b, ffn_w_down, ab_w_in, mla_q_norm_g, mla_w_q_b, mla_kv_norm_g, mla_w_kv_b, hy_conv_w, hy_conv_b, hy_w1, hy_b1, hy_w2, hy_b2, hy_w3, hy_bias, ab_w_out, hg_w_in, hg_lb_logits, hg_norm_g, hg_w_out, final_norm_g):
    batch, n, d = x.shape
    depth = mod_w.shape[0]
    assert batch < MOD_ROWS
    ctx_row = batch
    cc = jnp.zeros((MOD_ROWS, d), F32).at[:batch].set(c).at[ctx_row].set(c_ctx)
    mods_all = _mod_vectors(cc, mod_w, mod_b)
    lat_row = lambda bi, *_: bi
    ctx_row_fn = lambda bi, *_: ctx_row
    rope_tabs = _rope_tables(n)

    xc = ctx
    for layer in range(depth):
        need_ctx = layer < depth - 1
        j = layer // 2
        mods = mods_all[layer]
        if layer % 2 == 0:
            q_lora = mla_q_norm_g.shape[-1]
            kv_lora = mla_kv_norm_g.shape[-1]
            hy_ch = hy_bias.shape[-1]
            w = _ab_weights(ab_w_in[j], mla_w_q_b[j], mla_w_kv_b[j], ab_w_out[j], q_lora, kv_lora, hy_ch)
            hy_args = (hy_w1[j], hy_b1[j], hy_w2[j], hy_b2[j], hy_w3[j], hy_bias[j])
            n_ctx = xc.shape[1]
            kv0 = jnp.zeros((batch, n + n_ctx, w["heads"] * HEAD_LANES), BF16)
            q, k, v, hy, hyb = _ab_project(x, mods, lat_row, norm1_g[layer], w, mla_q_norm_g[j],
                                           mla_kv_norm_g[j], rope_tabs, hy_conv_w[j], hy_conv_b[j], hy_ch,
                                           (kv0, kv0), 0)
            q_c, k, v, hy_c, hyb_c = _ab_project(xc, mods, ctx_row_fn, norm1_g[layer], w, mla_q_norm_g[j],
                                                 mla_kv_norm_g[j], None, hy_conv_w[j], hy_conv_b[j], hy_ch,
                                                 (k, v), n)
            o = _attention(q, [k], [v], w["heads"])
            x_new = _ab_out(o, _hyena(hy, hyb, *hy_args), x, mods, lat_row, w)
            if need_ctx:
                o_c = _attention(q_c, [k], [v], w["heads"], kv_rows=n_ctx, kv_row0=n)
                xc = _ab_out(o_c, _hyena(hy_c, hyb_c, *hy_args), xc, mods, ctx_row_fn, w)
            x = x_new
        else:
            w_in = hg_w_in[j].astype(BF16)
            z_c = _hg_project(xc, mods, ctx_row_fn, norm1_g[layer], w_in)
            z = _hg_project(x, mods, lat_row, norm1_g[layer], w_in)
            o, o_c = _hg_scan(z, z_c, hg_lb_logits, layer, need_ctx)
            w_out = hg_w_out[j].astype(BF16)
            x = _hg_out(o, z, x, mods, lat_row, hg_norm_g[j], w_out)
            if need_ctx:
                xc = _hg_out(o_c, z_c, xc, mods, ctx_row_fn, hg_norm_g[j], w_out)
        w_up = ffn_w_up[layer].astype(BF16)
        w_dn = ffn_w_down[layer].astype(BF16)
        last = layer == depth - 1
        x = _conv_ffn(x, mods, lat_row, norm2_g[layer], w_up, ffn_conv_w[layer], ffn_conv_b[layer], w_dn,
                      final_norm_g if last else None)
        if need_ctx:
            xc = _conv_ffn(xc, mods, ctx_row_fn, norm2_g[layer], w_up, ffn_conv_w[layer], ffn_conv_b[layer], w_dn)
    return x
```
